```python
import math
import jax, jax.numpy as jnp
from jax import lax
import numpy as np

D_MODEL = 2048
BATCH = 2
SEQ = 8192
DEPTH = 4

N_EVEN = (DEPTH + 1) // 2
N_ODD = DEPTH // 2

S5_WIDTH = D_MODEL // 2
S5_GROUP = 16
S5_GROUPS = S5_WIDTH // S5_GROUP
S5_STATE = 64
DIFF_WIDTH = D_MODEL // 2
DIFF_HEAD_DIM = 128
DIFF_VDIM = 2 * DIFF_HEAD_DIM
DIFF_HEADS = DIFF_WIDTH // DIFF_VDIM

C_HEAD_DIM = 128
C_HEADS = D_MODEL // C_HEAD_DIM
DILATED_BRANCHES = ((128, 1), (512, 4), (2048, 16))

MEM_LEN = 256
XA_HEADS = 4
XA_HEAD_DIM = D_MODEL // XA_HEADS

FFN_HIDDEN = -(-8 * D_MODEL // (3 * 256)) * 256

ROPE_THETA = 500000.0
ROPE_DIM = 128 // 4
Q_BLOCK = 128
NORM_EPS = 1e-6
MASK_VALUE = -1e30

kernel_name = 'hybrid_s5_diffattn_dilated_encoder'

f32 = jnp.float32


def rmsnorm(x, g):
    xf = x.astype(f32)
    y = xf * lax.rsqrt(jnp.mean(xf * xf, axis=-1, keepdims=True) + NORM_EPS)
    return (y * g.astype(f32)).astype(x.dtype)


def rope_tables(seq):
    pos = jnp.arange(seq, dtype=f32)
    inv = ROPE_THETA ** (-jnp.arange(0, ROPE_DIM, 2, dtype=f32) / ROPE_DIM)
    ang = pos[:, None] * inv[None, :]
    return jnp.cos(ang), jnp.sin(ang)


def apply_partial_rope(t, cos, sin):
    seq = t.shape[1]
    bshape = (1, seq) + (1,) * (t.ndim - 3) + (ROPE_DIM // 2,)
    c = cos.reshape(bshape)
    s = sin.reshape(bshape)
    tf = t.astype(f32)
    x1 = tf[..., :ROPE_DIM // 2]
    x2 = tf[..., ROPE_DIM // 2:ROPE_DIM]
    out = jnp.concatenate([x1 * c - x2 * s, x2 * c + x1 * s, tf[..., ROPE_DIM:]], axis=-1)
    return out.astype(t.dtype)


def _ssm_combine(e1, e2):
    a1, b1 = e1
    a2, b2 = e2
    return a1 * a2, a2 * b1 + b2


def s5_mixer(u, lam_re, lam_im, log_step, b_re, b_im, c_re, c_im, d_skip, glu_w, glu_b):
    bsz, seq, _ = u.shape
    uf = u.astype(f32)
    ug = uf.reshape(bsz, seq, S5_GROUPS, S5_GROUP)
    lam = lax.complex(lam_re.astype(f32), lam_im.astype(f32))
    step = jnp.exp(log_step.astype(f32))[..., None]
    lam_bar = jnp.exp(lam * step)
    bmat = lax.complex(b_re.astype(f32), b_im.astype(f32))
    b_bar = ((lam_bar - 1.0) / lam)[..., None] * bmat
    cmat = lax.complex(c_re.astype(f32), c_im.astype(f32))
    y = ug * d_skip.astype(f32).reshape(S5_GROUPS, S5_GROUP)
    for direction in (0, 1):
        bu = jnp.einsum('bsgn,gpn->bsgp', ug, b_bar[direction])
        a = jnp.broadcast_to(lam_bar[direction], bu.shape)
        _, state = lax.associative_scan(_ssm_combine, (a, bu), reverse=(direction == 1), axis=1)
        y = y + jnp.einsum('bsgp,gnp->bsgn', state, cmat[direction]).real
    y = y.reshape(bsz, seq, S5_WIDTH)
    g = jax.nn.gelu(y)
    out = g * jax.nn.sigmoid(g @ glu_w.astype(f32) + glu_b.astype(f32))
    return out.astype(u.dtype)


def differential_attention(q, k, v, lam_vec, subln_g, lambda_init, cos, sin):
    bsz, seq, _ = q.shape
    H, dh = DIFF_HEADS, DIFF_HEAD_DIM
    q = apply_partial_rope(q.reshape(bsz, seq, H, 2, dh), cos, sin)
    k = apply_partial_rope(k.reshape(bsz, seq, H, 2, dh), cos, sin)
    v = v.reshape(bsz, seq, H, DIFF_VDIM)
    lv = lam_vec.astype(f32)
    lam = jnp.exp(jnp.dot(lv[0], lv[1])) - jnp.exp(jnp.dot(lv[2], lv[3])) + lambda_init
    scale = dh ** -0.5
    nb = seq // Q_BLOCK
    qb = jnp.moveaxis(q.reshape(bsz, nb, Q_BLOCK, H, 2, dh), 1, 0)

    def block(qblk):
        s = jnp.einsum('bqhcd,bkhcd->bhcqk', qblk, k, preferred_element_type=f32) * scale
        p = jax.nn.softmax(s, axis=-1)
        w = p[:, :, 0] - lam * p[:, :, 1]
        return jnp.einsum('bhqk,bkhe->bqhe', w.astype(v.dtype), v, preferred_element_type=f32)

    o = lax.map(block, qb)
    o = jnp.moveaxis(o, 0, 1).reshape(bsz, seq, H, DIFF_VDIM)
    o = rmsnorm(o, subln_g) * (1.0 - lambda_init)
    return o.reshape(bsz, seq, DIFF_WIDTH).astype(v.dtype)


def even_mixer(hn, w_in, w_out, lam_re, lam_im, log_step, b_re, b_im, c_re, c_im,
               d_skip, glu_w, glu_b, lam_vec, subln_g, lambda_init, cos, sin):
    proj = hn @ w_in
    u, q, k, v = jnp.split(proj, [S5_WIDTH, S5_WIDTH + DIFF_WIDTH, S5_WIDTH + 2 * DIFF_WIDTH], axis=-1)
    y_s5 = s5_mixer(u, lam_re, lam_im, log_step, b_re, b_im, c_re, c_im, d_skip, glu_w, glu_b)
    y_diff = differential_attention(q, k, v, lam_vec, subln_g, lambda_init, cos, sin)
    return jnp.concatenate([y_s5.astype(hn.dtype), y_diff.astype(hn.dtype)], axis=-1) @ w_out


def dilated_window_branch(q, k, v, dil, half):
    bsz, seq, H, dh = q.shape
    M = seq // dil
    nb = -(-M // Q_BLOCK)
    Mp = nb * Q_BLOCK
    qs = q.reshape(bsz, M, dil, H, dh)
    ks = k.reshape(bsz, M, dil, H, dh)
    vs = v.reshape(bsz, M, dil, H, dh)
    qp = jnp.pad(qs, ((0, 0), (0, Mp - M), (0, 0), (0, 0), (0, 0)))
    kpad = ((0, 0), (half, Mp - M + Q_BLOCK - half), (0, 0), (0, 0), (0, 0))
    kp = jnp.pad(ks, kpad).reshape(bsz, nb + 1, Q_BLOCK, dil, H, dh)
    vp = jnp.pad(vs, kpad).reshape(bsz, nb + 1, Q_BLOCK, dil, H, dh)
    kslab = jnp.concatenate([kp[:, :-1], kp[:, 1:]], axis=2)
    vslab = jnp.concatenate([vp[:, :-1], vp[:, 1:]], axis=2)
    qb = qp.reshape(bsz, nb, Q_BLOCK, dil, H, dh)
    s = jnp.einsum('bnqrhd,bnkrhd->bnrhqk', qb, kslab, preferred_element_type=f32) * (dh ** -0.5)
    i = jnp.arange(Q_BLOCK)[:, None]
    t = jnp.arange(2 * Q_BLOCK)[None, :]
    offset = t - half - i
    kpos = jnp.arange(nb)[:, None, None] * Q_BLOCK + t[None] - half
    valid = (jnp.abs(offset) <= half)[None] & (kpos >= 0) & (kpos < M)
    s = jnp.where(valid[None, :, None, None], s, MASK_VALUE)
    m = jnp.max(s, axis=-1, keepdims=True)
    e = jnp.exp(s - m)
    den = jnp.sum(e, axis=-1, keepdims=True)
    o = jnp.einsum('bnrhqk,bnkrhd->bnqrhd', (e / den).astype(v.dtype), vslab, preferred_element_type=f32)
    lse = (m + jnp.log(den))[..., 0]
    o = o.reshape(bsz, Mp, dil, H, dh)[:, :M].reshape(bsz, seq, H, dh)
    lse = jnp.moveaxis(lse, -1, 2).reshape(bsz, Mp, dil, H)[:, :M].reshape(bsz, seq, H)
    return o, lse


def odd_mixer(hn, w_qkv, w_out, cos, sin):
    bsz, seq, _ = hn.shape
    qkv = (hn @ w_qkv).reshape(bsz, seq, 3, C_HEADS, C_HEAD_DIM)
    q = apply_partial_rope(qkv[:, :, 0], cos, sin)
    k = apply_partial_rope(qkv[:, :, 1], cos, sin)
    v = qkv[:, :, 2]
    outs, lses = [], []
    for window, dil in DILATED_BRANCHES:
        o, lse = dilated_window_branch(q, k, v, dil, window // (2 * dil))
        outs.append(o)
        lses.append(lse)
    alpha = jax.nn.softmax(jnp.stack(lses, axis=0), axis=0)
    o = jnp.sum(alpha[..., None] * jnp.stack(outs, axis=0), axis=0)
    return o.reshape(bsz, seq, D_MODEL).astype(hn.dtype) @ w_out


def memory_cross_attention(hn, mem_n, wq, wkv, wo):
    bsz, seq, _ = hn.shape
    q = (hn @ wq).reshape(bsz, seq, XA_HEADS, XA_HEAD_DIM)
    kv = (mem_n @ wkv).reshape(bsz, mem_n.shape[1], 2, XA_HEADS, XA_HEAD_DIM)
    s = jnp.einsum('bqhd,bkhd->bhqk', q, kv[:, :, 0], preferred_element_type=f32) * (XA_HEAD_DIM ** -0.5)
    p = jax.nn.softmax(s, axis=-1)
    o = jnp.einsum('bhqk,bkhd->bqhd', p.astype(hn.dtype), kv[:, :, 1])
    return o.reshape(bsz, seq, D_MODEL) @ wo


def swiglu(hn, w13, w2):
    a, b = jnp.split(hn @ w13, 2, axis=-1)
    return (jax.nn.silu(a) * b) @ w2


def setup_inputs(seed: int = 0) -> dict:
    key = jax.random.key(seed)
    keys = iter(jax.random.split(key, 40))
    D, G, P, N = D_MODEL, S5_GROUPS, S5_STATE, S5_GROUP

    def nrm(shape, scale):
        return jax.random.normal(next(keys), shape, f32) * scale

    def gain(shape):
        return 1.0 + nrm(shape, 0.02)

    n_idx = jnp.arange(P, dtype=f32)
    inp = {}
    inp['x'] = nrm((BATCH, SEQ, D), 1.0)
    inp['mem'] = nrm((BATCH, MEM_LEN, D), 1.0)
    inp['norm_mix_g'] = gain((DEPTH, D))
    inp['norm_xa_g'] = gain((DEPTH, D))
    inp['norm_mem_g'] = gain((DEPTH, D))
    inp['xa_wq'] = nrm((DEPTH, D, D), D ** -0.5)
    inp['xa_wkv'] = nrm((DEPTH, D, 2 * D), D ** -0.5)
    inp['xa_wo'] = nrm((DEPTH, D, D), D ** -0.5)
    inp['norm_ffn_g'] = gain((DEPTH, D))
    inp['ffn_w13'] = nrm((DEPTH, D, 2 * FFN_HIDDEN), D ** -0.5)
    inp['ffn_w2'] = nrm((DEPTH, FFN_HIDDEN, D), FFN_HIDDEN ** -0.5)
    inp['ab_w_in'] = nrm((N_EVEN, D, S5_WIDTH + 3 * DIFF_WIDTH), D ** -0.5)
    inp['ab_w_out'] = nrm((N_EVEN, S5_WIDTH + DIFF_WIDTH, D), (S5_WIDTH + DIFF_WIDTH) ** -0.5)
    inp['s5_lambda_re'] = -0.5 + nrm((N_EVEN, 2, G, P), 0.01)
    inp['s5_lambda_im'] = math.pi * n_idx + nrm((N_EVEN, 2, G, P), 0.01)
    inp['s5_log_step'] = jax.random.uniform(next(keys), (N_EVEN, 2, G), f32, math.log(1e-3), math.log(1e-1))
    inp['s5_b_re'] = nrm((N_EVEN, 2, G, P, N), (2 * N) ** -0.5)
    inp['s5_b_im'] = nrm((N_EVEN, 2, G, P, N), (2 * N) ** -0.5)
    inp['s5_c_re'] = nrm((N_EVEN, 2, G, N, P), 2.0 * (2 * P) ** -0.5)
    inp['s5_c_im'] = nrm((N_EVEN, 2, G, N, P), 2.0 * (2 * P) ** -0.5)
    inp['s5_d'] = nrm((N_EVEN, S5_WIDTH), 1.0)
    inp['s5_glu_w'] = nrm((N_EVEN, S5_WIDTH, S5_WIDTH), S5_WIDTH ** -0.5)
    inp['s5_glu_b'] = nrm((N_EVEN, S5_WIDTH), 0.02)
    inp['diff_lambda'] = nrm((N_EVEN, 4, DIFF_HEAD_DIM), 0.1)
    inp['diff_subln_g'] = gain((N_EVEN, DIFF_VDIM))
    inp['c_w_qkv'] = nrm((N_ODD, D, 3 * D), D ** -0.5)
    inp['c_w_out'] = nrm((N_ODD, D, D), D ** -0.5)
    inp['final_norm_g'] = gain((D,))
    return inp


def reference(x, mem, norm_mix_g, norm_xa_g, norm_mem_g, xa_wq, xa_wkv, xa_wo, norm_ffn_g,
              ffn_w13, ffn_w2, ab_w_in, ab_w_out, s5_lambda_re, s5_lambda_im, s5_log_step,
              s5_b_re, s5_b_im, s5_c_re, s5_c_im, s5_d, s5_glu_w, s5_glu_b, diff_lambda,
              diff_subln_g, c_w_qkv, c_w_out, final_norm_g):
    cos, sin = rope_tables(x.shape[1])
    h = x
    for layer in range(DEPTH):
        i = layer // 2
        hn = rmsnorm(h, norm_mix_g[layer])
        if layer % 2 == 0:
            lambda_init = 0.8 - 0.6 * math.exp(-0.3 * layer)
            mix = even_mixer(hn, ab_w_in[i], ab_w_out[i], s5_lambda_re[i], s5_lambda_im[i],
                             s5_log_step[i], s5_b_re[i], s5_b_im[i], s5_c_re[i], s5_c_im[i],
                             s5_d[i], s5_glu_w[i], s5_glu_b[i], diff_lambda[i], diff_subln_g[i],
                             lambda_init, cos, sin)
        else:
            mix = odd_mixer(hn, c_w_qkv[i], c_w_out[i], cos, sin)
        h = h + mix
        h = h + memory_cross_attention(rmsnorm(h, norm_xa_g[layer]), rmsnorm(mem, norm_mem_g[layer]),
                                       xa_wq[layer], xa_wkv[layer], xa_wo[layer])
        h = h + swiglu(rmsnorm(h, norm_ffn_g[layer]), ffn_w13[layer], ffn_w2[layer])
    return rmsnorm(h, final_norm_g)
```

```python
import functools
import math

import jax
import jax.numpy as jnp
from jax import lax
from jax.experimental import pallas as pl
from jax.experimental.pallas import tpu as pltpu

F32 = jnp.float32
BF16 = jnp.bfloat16

NORM_EPS = 1e-6
MASK_VALUE = -1e30
ROPE_THETA = 500000.0
ROPE_HALF = 16
HEAD_DIM = 128
XA_HEADS = 4
DILATED_BRANCHES = ((128, 1), (512, 4), (2048, 16))
S5_CHUNK = 32
S5_SEGMENTS = 8
LANES = 128
SUBLANES = 8
VMEM_CAP_BYTES = 56 * 1024 * 1024


def _pick_tile(n, pref, quantum):
    t = (min(n, pref) // quantum) * quantum
    while t >= quantum:
        if n % t == 0:
            return t
        t -= quantum
    return n


def _params(semantics, vmem_bytes):
    return pltpu.CompilerParams(
        dimension_semantics=semantics,
        vmem_limit_bytes=int(min(VMEM_CAP_BYTES, max(32 * 1024 * 1024, vmem_bytes))))


def _rms_normalize(x, g):
    ms = jnp.mean(x * x, axis=-1, keepdims=True)
    return (x * lax.rsqrt(ms + NORM_EPS)) * g


def _rope_lanes(y, cf, sa, sb):
    outs = []
    for c in range(y.shape[1] // LANES):
        yc = y[:, c * LANES:(c + 1) * LANES]
        outs.append(yc * cf + pltpu.roll(yc, LANES - ROPE_HALF, 1) * sa + pltpu.roll(yc, ROPE_HALF, 1) * sb)
    return outs[0] if len(outs) == 1 else jnp.concatenate(outs, axis=1)


def _rope_tables(seq):
    pos = jnp.arange(seq, dtype=F32)
    inv = ROPE_THETA ** (-jnp.arange(0, 2 * ROPE_HALF, 2, dtype=F32) / (2 * ROPE_HALF))
    ang = pos[:, None] * inv[None, :]
    cos, sin = jnp.cos(ang), jnp.sin(ang)
    rest = LANES - 2 * ROPE_HALF
    zeros = jnp.zeros((seq, ROPE_HALF), F32)
    cf = jnp.concatenate([cos, cos, jnp.ones((seq, rest), F32)], axis=1)
    sa = jnp.concatenate([-sin, zeros, jnp.zeros((seq, rest), F32)], axis=1)
    sb = jnp.concatenate([zeros, sin, jnp.zeros((seq, rest), F32)], axis=1)
    return cf, sa, sb


def _norm_matmul_kernel(*refs, segs, tn, use_rope):
    if use_rope:
        x_ref, g_ref, w_ref, cf_ref, sa_ref, sb_ref, o_ref, xn_ref = refs
    else:
        x_ref, g_ref, w_ref, o_ref, xn_ref = refs
    j = pl.program_id(1)

    @pl.when(j == 0)
    def _():
        xn_ref[...] = _rms_normalize(x_ref[...], g_ref[...]).astype(BF16)

    y = jnp.dot(xn_ref[...], w_ref[...], preferred_element_type=F32)
    for lo, hi, rope, scale in segs:
        @pl.when(jnp.logical_and(j >= lo // tn, j < hi // tn))
        def _(rope=rope, scale=scale):
            z = y
            if rope:
                z = _rope_lanes(z, cf_ref[...], sa_ref[...], sb_ref[...])
            if scale != 1.0:
                z = z * scale
            o_ref[...] = z.astype(o_ref.dtype)


def _norm_matmul(x, g, w, *, segs=(), rope_tabs=None, seq=None, tm=1024, tn=1024, name):
    M, D = x.shape
    N = w.shape[1]
    tm = _pick_tile(M if seq is None else seq, tm, SUBLANES)
    tn = _pick_tile(math.gcd(N, *[b for s in segs for b in s[:2]]), tn, LANES)
    full, pos = [], 0
    for lo, hi, rope, scale in sorted(segs):
        if lo > pos:
            full.append((pos, lo, False, 1.0))
        full.append((lo, hi, rope, scale))
        pos = hi
    if pos < N:
        full.append((pos, N, False, 1.0))
    for lo, hi, _, _ in full:
        assert lo % tn == 0 and hi % tn == 0, (lo, hi, tn)
    use_rope = any(s[2] for s in full)
    in_specs = [pl.BlockSpec((tm, D), lambda i, j: (i, 0)),
                pl.BlockSpec((1, D), lambda i, j: (0, 0)),
                pl.BlockSpec((D, tn), lambda i, j: (0, j))]
    args = [x, g.reshape(1, D).astype(F32), w]
    if use_rope:
        nseq = seq // tm
        in_specs += [pl.BlockSpec((tm, LANES), lambda i, j: (i % nseq, 0))] * 3
        args += list(rope_tabs)
    vmem = 2 * tm * D * 4 + tm * D * 2 + 2 * D * tn * 2 + 2 * tm * tn * 2 + 3 * tm * tn * 4 + (4 << 20)
    return pl.pallas_call(
        functools.partial(_norm_matmul_kernel, segs=tuple(full), tn=tn, use_rope=use_rope),
        grid=(M // tm, N // tn),
        in_specs=in_specs,
        out_specs=pl.BlockSpec((tm, tn), lambda i, j: (i, j)),
        out_shape=jax.ShapeDtypeStruct((M, N), BF16),
        scratch_shapes=[pltpu.VMEM((tm, D), BF16)],
        compiler_params=_params(("parallel", "arbitrary"), vmem),
        name=name,
    )(*args)


def _matmul_res_kernel(*refs, n_a):
    a_refs, w_refs = refs[:n_a], refs[n_a:2 * n_a]
    res_ref, o_ref = refs[2 * n_a], refs[2 * n_a + 1]
    acc = res_ref[...]
    for a_ref, w_ref in zip(a_refs, w_refs):
        acc = acc + jnp.dot(a_ref[...], w_ref[...], preferred_element_type=F32)
    o_ref[...] = acc


def _matmul_res(a_list, w, res, *, tm=1024, tn=512, name):
    n_a = len(a_list)
    M, K = a_list[0].shape
    N = w.shape[1]
    tm = _pick_tile(M, tm, SUBLANES)
    tn = _pick_tile(N, tn, LANES)
    in_specs = [pl.BlockSpec((tm, K), lambda i, j: (i, 0)) for _ in range(n_a)]
    in_specs += [pl.BlockSpec((K, tn), lambda i, j, r=r: (r, j)) for r in range(n_a)]
    in_specs += [pl.BlockSpec((tm, tn), lambda i, j: (i, j))]
    vmem = n_a * (2 * tm * K * 2 + 2 * K * tn * 2) + 5 * tm * tn * 4 + (4 << 20)
    return pl.pallas_call(
        functools.partial(_matmul_res_kernel, n_a=n_a),
        grid=(M // tm, N // tn),
        in_specs=in_specs,
        out_specs=pl.BlockSpec((tm, tn), lambda i, j: (i, j)),
        out_shape=jax.ShapeDtypeStruct((M, N), F32),
        compiler_params=_params(("parallel", "parallel"), vmem),
        name=name,
    )(*a_list, *([w] * n_a), res)


def _swiglu_up_kernel(x_ref, g_ref, w1_ref, w3_ref, o_ref, xn_ref):
    @pl.when(pl.program_id(1) == 0)
    def _():
        xn_ref[...] = _rms_normalize(x_ref[...], g_ref[...]).astype(BF16)

    xn = xn_ref[...]
    a = jnp.dot(xn, w1_ref[...], preferred_element_type=F32)
    b = jnp.dot(xn, w3_ref[...], preferred_element_type=F32)
    o_ref[...] = (a * jax.nn.sigmoid(a) * b).astype(o_ref.dtype)


def _swiglu_up(x, g, w13, *, tm=1024, tn=512, name):
    M, D = x.shape
    H = w13.shape[1] // 2
    tm = _pick_tile(M, tm, SUBLANES)
    tn = _pick_tile(H, tn, LANES)
    nj = H // tn
    vmem = 2 * tm * D * 4 + tm * D * 2 + 4 * D * tn * 2 + 2 * tm * tn * 2 + 4 * tm * tn * 4 + (4 << 20)
    return pl.pallas_call(
        _swiglu_up_kernel,
        grid=(M // tm, nj),
        in_specs=[pl.BlockSpec((tm, D), lambda i, j: (i, 0)),
                  pl.BlockSpec((1, D), lambda i, j: (0, 0)),
                  pl.BlockSpec((D, tn), lambda i, j: (0, j)),
                  pl.BlockSpec((D, tn), lambda i, j: (0, j + nj))],
        out_specs=pl.BlockSpec((tm, tn), lambda i, j: (i, j)),
        out_shape=jax.ShapeDtypeStruct((M, H), BF16),
        scratch_shapes=[pltpu.VMEM((tm, D), BF16)],
        compiler_params=_params(("parallel", "arbitrary"), vmem),
        name=name,
    )(x, g.reshape(1, D).astype(F32), w13, w13)


def _cross_attn_kernel(q_ref, k_ref, v_ref, o_ref, *, nh):
    dh = q_ref.shape[1] // nh
    for h in range(nh):
        sl = slice(h * dh, (h + 1) * dh)
        s = lax.dot_general(q_ref[:, sl], k_ref[:, sl], (((1,), (1,)), ((), ())), preferred_element_type=F32)
        m = jnp.max(s, axis=-1, keepdims=True)
        e = jnp.exp(s - m)
        den = jnp.sum(e, axis=-1, keepdims=True)
        o = jnp.dot(e.astype(BF16), v_ref[:, sl], preferred_element_type=F32)
        o_ref[:, sl] = (o / den).astype(o_ref.dtype)


def _cross_attn(q, kv, bsz, *, tq=1024, name):
    T, D = q.shape
    S = T // bsz
    mem_len = kv.shape[0] // bsz
    tq = _pick_tile(S, tq, SUBLANES)
    nq = S // tq
    kv3 = kv.reshape(bsz, mem_len, 2 * D)
    vmem = 4 * tq * D * 2 + 4 * mem_len * D * 2 + 4 * tq * mem_len * 4 + (8 << 20)
    return pl.pallas_call(
        functools.partial(_cross_attn_kernel, nh=XA_HEADS),
        grid=(bsz, nq),
        in_specs=[pl.BlockSpec((tq, D), lambda b, i: (b * nq + i, 0)),
                  pl.BlockSpec((None, mem_len, D), lambda b, i: (b, 0, 0)),
                  pl.BlockSpec((None, mem_len, D), lambda b, i: (b, 0, 1))],
        out_specs=pl.BlockSpec((tq, D), lambda b, i: (b * nq + i, 0)),
        out_shape=jax.ShapeDtypeStruct((T, D), BF16),
        compiler_params=_params(("parallel", "parallel"), vmem),
        name=name,
    )(q, kv3, kv3)


def _diff_attn_kernel(q_ref, k_ref, v_ref, lv_ref, sg_ref, o_ref, *, tk, lambda_init):
    tq = q_ref.shape[0]
    seq = k_ref.shape[0]
    dv = v_ref.shape[1]
    dh = dv // 2
    q = q_ref[...]

    def body(kk, carry):
        off = pl.multiple_of(kk * tk, tk)
        kc = k_ref[pl.ds(off, tk), :]
        vc = v_ref[pl.ds(off, tk), :]
        new = []
        for c in range(2):
            m, l, acc = carry[c]
            s = lax.dot_general(q[:, c * dh:(c + 1) * dh], kc[:, c * dh:(c + 1) * dh],
                                (((1,), (1,)), ((), ())), preferred_element_type=F32)
            m_new = jnp.maximum(m, jnp.max(s, axis=-1, keepdims=True))
            alpha = jnp.exp(m - m_new)
            p = jnp.exp(s - m_new)
            l = alpha * l + jnp.sum(p, axis=-1, keepdims=True)
            acc = alpha * acc + jnp.dot(p.astype(BF16), vc, preferred_element_type=F32)
            new.append((m_new, l, acc))
        return tuple(new)

    init = tuple((jnp.full((tq, 1), MASK_VALUE, F32), jnp.zeros((tq, 1), F32), jnp.zeros((tq, dv), F32))
                 for _ in range(2))
    (_, l1, a1), (_, l2, a2) = lax.fori_loop(0, seq // tk, body, init)
    lv = lv_ref[...]
    lam = (jnp.exp(jnp.sum(lv[0:1] * lv[1:2], axis=-1, keepdims=True))
           - jnp.exp(jnp.sum(lv[2:3] * lv[3:4], axis=-1, keepdims=True)) + lambda_init)
    o = a1 / l1 - lam * (a2 / l2)
    o = _rms_normalize(o, sg_ref[...]) * (1.0 - lambda_init)
    o_ref[...] = o.astype(o_ref.dtype)


def _diff_attn(proj, lam_vec, subln_g, bsz, *, s5_width, n_heads, lambda_init, tq=512, tk=512, name):
    T, ncol = proj.shape
    S = T // bsz
    dv = 2 * HEAD_DIM
    width = n_heads * dv
    q0, k0, v0 = s5_width // dv, (s5_width + width) // dv, (s5_width + 2 * width) // dv
    tq = _pick_tile(S, tq, SUBLANES)
    tk = _pick_tile(S, tk, LANES)
    nq = S // tq
    proj3 = proj.reshape(bsz, S, ncol)
    vmem = 4 * tq * dv * 2 + 4 * S * dv * 2 + 6 * tq * tk * 4 + 8 * tq * dv * 4 + (8 << 20)
    return pl.pallas_call(
        functools.partial(_diff_attn_kernel, tk=tk, lambda_init=lambda_init),
        grid=(bsz, n_heads, nq),
        in_specs=[pl.BlockSpec((None, tq, dv), lambda b, h, i: (b, i, q0 + h)),
                  pl.BlockSpec((None, S, dv), lambda b, h, i: (b, 0, k0 + h)),
                  pl.BlockSpec((None, S, dv), lambda b, h, i: (b, 0, v0 + h)),
                  pl.BlockSpec((4, HEAD_DIM), lambda b, h, i: (0, 0)),
                  pl.BlockSpec((1, dv), lambda b, h, i: (0, 0))],
        out_specs=pl.BlockSpec((tq, dv), lambda b, h, i: (b * nq + i, h)),
        out_shape=jax.ShapeDtypeStruct((T, width), BF16),
        compiler_params=_params(("parallel", "parallel", "parallel"), vmem),
        name=name,
    )(proj3, proj3, proj3, lam_vec.astype(F32), subln_g.reshape(1, dv).astype(F32))


def _s5_tables(lam_re, lam_im, log_step, b_re, b_im, c_re, c_im, d_skip, n_steps):
    hi = lax.Precision.HIGHEST
    L = S5_CHUNK
    _, G, P = lam_re.shape
    N = b_re.shape[-1]
    lam = lax.complex(lam_re.astype(F32), lam_im.astype(F32))
    step = jnp.exp(log_step.astype(F32))[..., None]
    lam_step = lam * step
    lam_bar = jnp.exp(lam_step)
    b_bar = ((lam_bar - 1.0) / lam)[..., None] * lax.complex(b_re.astype(F32), b_im.astype(F32))
    cmat = lax.complex(c_re.astype(F32), c_im.astype(F32))
    tau = jnp.arange(L + 1, dtype=F32)
    pw = jnp.exp(lam_step[:, :, None, :] * tau[None, None, :, None])

    kern = jnp.einsum('dgnp,dgtp,dgpm->dgtnm', cmat, pw[:, :, :L], b_bar, precision=hi).real
    kf, kb = kern[0], kern[1]
    kcat = jnp.concatenate([kb[:, :0:-1], (kf[:, :1] + kb[:, :1]), kf[:, 1:]], axis=1)
    idx = jnp.arange(L)[None, :] - jnp.arange(L)[:, None] + (L - 1)
    toep = kcat[:, idx]
    toep = jnp.transpose(toep, (0, 1, 4, 2, 3)).reshape(G, L * N, L * N)

    win_f = pw[0][:, ::-1][:, 1:, :, None] * b_bar[0][:, None]
    win_b = pw[1][:, :L, :, None] * b_bar[1][:, None]
    def _in_cols(w):
        return jnp.transpose(w, (0, 1, 3, 2)).reshape(G, L * N, P)
    w_in = jnp.concatenate([_in_cols(win_f.real), _in_cols(win_b.real),
                            _in_cols(win_f.imag), _in_cols(win_b.imag)], axis=2)

    wout_f = pw[0][:, 1:, None, :] * cmat[0][:, None]
    wout_b = pw[1][:, ::-1][:, :L, None, :] * cmat[1][:, None]
    def _out_rows(w):
        return jnp.transpose(w, (0, 3, 1, 2)).reshape(G, P, L * N)
    w_out = jnp.concatenate([_out_rows(wout_f.real), _out_rows(wout_b.real),
                             -_out_rows(wout_f.imag), -_out_rows(wout_b.imag)], axis=1)

    def _lanes(z):
        return jnp.concatenate([z[0], z[1]], axis=-1)
    a_chunk = jnp.exp(lam_step * float(L))
    a_seg = jnp.exp(lam_step * float(L * n_steps))
    decay = jnp.stack([_lanes(a_chunk.real), _lanes(a_chunk.imag),
                       _lanes(a_seg.real), _lanes(a_seg.imag)], axis=1)
    jj = jnp.arange(n_steps, dtype=F32)
    pf = jnp.exp(lam_step[0][:, None, :] * (float(L) * jj)[None, :, None])
    pb = jnp.exp(lam_step[1][:, None, :] * (float(L) * jj[::-1])[None, :, None])
    powers = jnp.stack([jnp.concatenate([pf.real, pb.real], axis=-1),
                        jnp.concatenate([pf.imag, pb.imag], axis=-1)], axis=1)
    dvec = jnp.tile(d_skip.astype(F32).reshape(G, 1, N), (1, L, 1)).reshape(G, 1, L * N)
    return toep.astype(BF16), w_in.astype(BF16), w_out.astype(BF16), decay, powers, dvec


def _s5_kernel(u_ref, toep_ref, win_ref, wout_ref, decay_ref, pow_ref, dvec_ref, o_ref, *, n_steps, bsz, gb):
    rb = bsz * SUBLANES
    for gi in range(gb):
        u = u_ref[gi]
        y = jnp.dot(u, toep_ref[gi], preferred_element_type=F32)
        z = jnp.dot(u, win_ref[gi], preferred_element_type=F32)
        half = z.shape[1] // 2
        zr, zi = z[:, :half], z[:, half:]
        dec = decay_ref[gi]
        ar, ai, sr, si = dec[0:1], dec[1:2], dec[2:3], dec[3:4]
        is_fwd = lax.broadcasted_iota(jnp.int32, (rb, half), 1) < half // 2

        xr = jnp.zeros((rb, half), F32)
        xi = jnp.zeros((rb, half), F32)
        ent_r, ent_i = [], []
        for j in range(n_steps):
            jb = n_steps - 1 - j
            ent_r.append(xr)
            ent_i.append(xi)
            zrj = jnp.where(is_fwd, zr[j * rb:(j + 1) * rb], zr[jb * rb:(jb + 1) * rb])
            zij = jnp.where(is_fwd, zi[j * rb:(j + 1) * rb], zi[jb * rb:(jb + 1) * rb])
            xr, xi = ar * xr - ai * xi + zrj, ar * xi + ai * xr + zij

        sub = lax.broadcasted_iota(jnp.int32, (SUBLANES, half), 0)
        fwd8 = lax.broadcasted_iota(jnp.int32, (SUBLANES, half), 1) < half // 2
        keep = (fwd8 & (sub > 0)) | (jnp.logical_not(fwd8) & (sub < SUBLANES - 1))
        car_r, car_i = [], []
        for b in range(bsz):
            er, ei = xr[b * SUBLANES:(b + 1) * SUBLANES], xi[b * SUBLANES:(b + 1) * SUBLANES]
            cr = jnp.zeros((SUBLANES, half), F32)
            ci = jnp.zeros((SUBLANES, half), F32)
            for _ in range(SUBLANES - 1):
                tr, ti = sr * cr - si * ci + er, sr * ci + si * cr + ei
                cr = jnp.where(keep, jnp.where(fwd8, pltpu.roll(tr, 1, 0), pltpu.roll(tr, SUBLANES - 1, 0)), 0.0)
                ci = jnp.where(keep, jnp.where(fwd8, pltpu.roll(ti, 1, 0), pltpu.roll(ti, SUBLANES - 1, 0)), 0.0)
            car_r.append(cr)
            car_i.append(ci)
        gr = car_r[0] if bsz == 1 else jnp.concatenate(car_r, axis=0)
        gi_ = car_i[0] if bsz == 1 else jnp.concatenate(car_i, axis=0)

        pr_all, pi_all = pow_ref[gi, 0], pow_ref[gi, 1]
        rows_r, rows_i = [], []
        for j in range(n_steps):
            jb = n_steps - 1 - j
            lr = jnp.where(is_fwd, ent_r[j], ent_r[jb])
            li = jnp.where(is_fwd, ent_i[j], ent_i[jb])
            pr, pi = pr_all[j:j + 1], pi_all[j:j + 1]
            rows_r.append(lr + pr * gr - pi * gi_)
            rows_i.append(li + pr * gi_ + pi * gr)
        state = jnp.concatenate([jnp.concatenate(rows_r, axis=0), jnp.concatenate(rows_i, axis=0)], axis=1)
        y = y + jnp.dot(state.astype(BF16), wout_ref[gi], preferred_element_type=F32)
        y = y + u.astype(F32) * dvec_ref[gi]
        o_ref[gi] = y.astype(o_ref.dtype)


def _s5_mix(u, tables, bsz, *, gb=4, name):
    toep, w_in, w_out, decay, powers, dvec = tables
    T, width = u.shape
    S = T // bsz
    G = toep.shape[0]
    N = width // G
    L = S5_CHUNK
    n_steps = S // (L * S5_SEGMENTS)
    rows = n_steps * bsz * S5_SEGMENTS
    gb = _pick_tile(G, gb, 1)
    ug = u.reshape(bsz, S5_SEGMENTS, n_steps, L, G, N)
    ug = jnp.transpose(ug, (4, 2, 0, 1, 3, 5)).reshape(G, rows, L * N)
    blk = lambda *shape: pl.BlockSpec((gb,) + shape, lambda g: (g,) + (0,) * len(shape))
    vmem = 2 * gb * (2 * rows * L * N * 2 + (L * N) ** 2 * 2 + 2 * L * N * 4 * 64 * 2) + (24 << 20)
    yg = pl.pallas_call(
        functools.partial(_s5_kernel, n_steps=n_steps, bsz=bsz, gb=gb),
        grid=(G // gb,),
        in_specs=[blk(rows, L * N), blk(L * N, L * N), blk(L * N, w_in.shape[2]), blk(w_out.shape[1], L * N),
                  blk(4, decay.shape[2]), blk(2, n_steps, powers.shape[3]), blk(1, L * N)],
        out_specs=blk(rows, L * N),
        out_shape=jax.ShapeDtypeStruct((G, rows, L * N), BF16),
        compiler_params=_params(("parallel",), vmem),
        name=name,
    )(ug, toep, w_in, w_out, decay, powers, dvec)
    yg = yg.reshape(G, n_steps, bsz, S5_SEGMENTS, L, N)
    return jnp.transpose(yg, (2, 3, 1, 4, 0, 5)).reshape(T, width)


def _s5_glu_kernel(y_ref, w_ref, b_ref, o_ref):
    g = jax.nn.gelu(y_ref[...].astype(F32), approximate=True)
    z = jnp.dot(g.astype(BF16), w_ref[...], preferred_element_type=F32) + b_ref[...]
    o_ref[...] = (g * jax.nn.sigmoid(z)).astype(o_ref.dtype)


def _s5_glu(y, w, b, *, tm=1024, name):
    M, W = y.shape
    tm = _pick_tile(M, tm, SUBLANES)
    vmem = 4 * tm * W * 2 + 2 * W * W * 2 + 4 * tm * W * 4 + (4 << 20)
    return pl.pallas_call(
        _s5_glu_kernel,
        grid=(M // tm,),
        in_specs=[pl.BlockSpec((tm, W), lambda i: (i, 0)),
                  pl.BlockSpec((W, W), lambda i: (0, 0)),
                  pl.BlockSpec((1, W), lambda i: (0, 0))],
        out_specs=pl.BlockSpec((tm, W), lambda i: (i, 0)),
        out_shape=jax.ShapeDtypeStruct((M, W), BF16),
        compiler_params=_params(("parallel",), vmem),
        name=name,
    )(y, w, b.reshape(1, W).astype(F32))


def _dilated_kernel(q_ref, k0, k1, k2, k3, v0, v1, v2, v3, o_ref, lse_ref, *, sub_len, half, nh):
    i = pl.program_id(1)
    tq = q_ref.shape[0]
    kb = k0.shape[0]
    dh = q_ref.shape[1] // nh
    qpos = i * tq + lax.broadcasted_iota(jnp.int32, (tq, 4 * kb), 0)
    kpos = (2 * i - 1) * kb + lax.broadcasted_iota(jnp.int32, (tq, 4 * kb), 1)
    valid = (jnp.abs(kpos - qpos) <= half) & (kpos >= 0) & (kpos < sub_len)
    head_lane = lax.broadcasted_iota(jnp.int32, (tq, nh), 1)
    lse_all = jnp.zeros((tq, nh), F32)
    for h in range(nh):
        sl = slice(h * dh, (h + 1) * dh)
        kc = jnp.concatenate([k0[:, sl], k1[:, sl], k2[:, sl], k3[:, sl]], axis=0)
        vc = jnp.concatenate([v0[:, sl], v1[:, sl], v2[:, sl], v3[:, sl]], axis=0)
        s = lax.dot_general(q_ref[:, sl], kc, (((1,), (1,)), ((), ())), preferred_element_type=F32)
        s = jnp.where(valid, s, MASK_VALUE)
        m = jnp.max(s, axis=-1, keepdims=True)
        e = jnp.exp(s - m)
        den = jnp.sum(e, axis=-1, keepdims=True)
        o = jnp.dot(e.astype(BF16), vc, preferred_element_type=F32)
        o_ref[:, sl] = (o / den).astype(o_ref.dtype)
        lse_all = jnp.where(head_lane == h, m + jnp.log(den), lse_all)
    lse_ref[...] = lse_all


def _dilated_branch(q, k, v, n_seq, *, half, nh, name):
    T, D = q.shape
    sub_len = T // n_seq
    tq, kb = 2 * half, half
    nq, nkb = sub_len // tq, sub_len // kb

    def kv_spec(off):
        return pl.BlockSpec((kb, D), lambda s, i: (s * nkb + jnp.clip(2 * i - 1 + off, 0, nkb - 1), 0))

    vmem = 2 * (2 * tq * D * 2 + 8 * kb * D * 2) + (16 << 20)
    return pl.pallas_call(
        functools.partial(_dilated_kernel, sub_len=sub_len, half=half, nh=nh),
        grid=(n_seq, nq),
        in_specs=[pl.BlockSpec((tq, D), lambda s, i: (s * nq + i, 0))]
                 + [kv_spec(off) for off in range(4)] + [kv_spec(off) for off in range(4)],
        out_specs=[pl.BlockSpec((tq, D), lambda s, i: (s * nq + i, 0)),
                   pl.BlockSpec((tq, nh), lambda s, i: (s * nq + i, 0))],
        out_shape=[jax.ShapeDtypeStruct((T, D), BF16), jax.ShapeDtypeStruct((T, nh), F32)],
        compiler_params=_params(("parallel", "parallel"), vmem),
        name=name,
    )(q, k, k, k, k, v, v, v, v)


def _merge_proj_kernel(o1, o2, o3, l1, l2, l3, w_ref, res_ref, out_ref, a_ref, *, nh):
    @pl.when(pl.program_id(1) == 0)
    def _():
        lse = [l1[...], l2[...], l3[...]]
        mx = jnp.maximum(jnp.maximum(lse[0], lse[1]), lse[2])
        e = [jnp.exp(x - mx) for x in lse]
        inv = 1.0 / (e[0] + e[1] + e[2])
        dh = a_ref.shape[1] // nh
        for h in range(nh):
            sl = slice(h * dh, (h + 1) * dh)
            acc = None
            for eb, ob in zip(e, (o1, o2, o3)):
                term = (eb[:, h:h + 1] * inv[:, h:h + 1]) * ob[:, sl].astype(F32)
                acc = term if acc is None else acc + term
            a_ref[:, sl] = acc.astype(BF16)

    out_ref[...] = res_ref[...] + jnp.dot(a_ref[...], w_ref[...], preferred_element_type=F32)


def _merge_proj(outs, lses, w, res, *, nh, tm=512, tn=1024, name):
    M, D = outs[0].shape
    N = w.shape[1]
    tm = _pick_tile(M, tm, SUBLANES)
    tn = _pick_tile(N, tn, LANES)
    row = lambda width: pl.BlockSpec((tm, width), lambda i, j: (i, 0))
    vmem = 3 * 2 * tm * D * 2 + tm * D * 2 + 2 * D * tn * 2 + 5 * tm * tn * 4 + 4 * tm * D * 4 + (4 << 20)
    return pl.pallas_call(
        functools.partial(_merge_proj_kernel, nh=nh),
        grid=(M // tm, N // tn),
        in_specs=[row(D)] * 3 + [row(nh)] * 3
                 + [pl.BlockSpec((D, tn), lambda i, j: (0, j)), pl.BlockSpec((tm, tn), lambda i, j: (i, j))],
        out_specs=pl.BlockSpec((tm, tn), lambda i, j: (i, j)),
        out_shape=jax.ShapeDtypeStruct((M, N), F32),
        scratch_shapes=[pltpu.VMEM((tm, D), BF16)],
        compiler_params=_params(("parallel", "arbitrary"), vmem),
        name=name,
    )(*outs, *lses, w, res)


def _final_norm_kernel(x_ref, g_ref, o_ref):
    o_ref[...] = _rms_normalize(x_ref[...], g_ref[...])


def _final_norm(x, g, *, tm=1024, name):
    M, D = x.shape
    tm = _pick_tile(M, tm, SUBLANES)
    return pl.pallas_call(
        _final_norm_kernel,
        grid=(M // tm,),
        in_specs=[pl.BlockSpec((tm, D), lambda i: (i, 0)), pl.BlockSpec((1, D), lambda i: (0, 0))],
        out_specs=pl.BlockSpec((tm, D), lambda i: (i, 0)),
        out_shape=jax.ShapeDtypeStruct((M, D), F32),
        compiler_params=_params(("parallel",), 6 * tm * D * 4 + (4 << 20)),
        name=name,
    )(x, g.reshape(1, D).astype(F32))


def _even_mixer(h, g, w_in, w_out, s5_tabs, glu_w, glu_b, lam_vec, subln_g, lambda_init, rope_tabs, bsz, tag):
    T, _ = h.shape
    S = T // bsz
    s5_width = glu_w.shape[0]
    width = (w_in.shape[1] - s5_width) // 3
    n_heads = width // (2 * HEAD_DIM)
    segs = ((s5_width, s5_width + width, True, HEAD_DIM ** -0.5),
            (s5_width + width, s5_width + 2 * width, True, 1.0))
    proj = _norm_matmul(h, g, w_in, segs=segs, rope_tabs=rope_tabs, seq=S, name=f"even_in_{tag}")
    y_s5 = _s5_mix(proj[:, :s5_width], s5_tabs, bsz, name=f"s5_{tag}")
    y_s5 = _s5_glu(y_s5, glu_w, glu_b, name=f"s5_glu_{tag}")
    y_diff = _diff_attn(proj, lam_vec, subln_g, bsz, s5_width=s5_width, n_heads=n_heads,
                        lambda_init=lambda_init, name=f"diff_attn_{tag}")
    return _matmul_res([y_s5, y_diff], w_out, h, tn=1024, name=f"even_out_{tag}")


def _odd_mixer(h, g, w_qkv, w_out, rope_tabs, bsz, tag):
    T, D = h.shape
    S = T // bsz
    nh = D // HEAD_DIM
    segs = ((0, D, True, HEAD_DIM ** -0.5), (D, 2 * D, True, 1.0))
    qkv = _norm_matmul(h, g, w_qkv, segs=segs, rope_tabs=rope_tabs, seq=S, name=f"odd_in_{tag}")
    outs, lses = [], []
    for window, dil in DILATED_BRANCHES:
        half = window // (2 * dil)
        sub_len = S // dil

        def to_res(t):
            if dil == 1:
                return t
            w = t.shape[-1]
            return jnp.transpose(t.reshape(bsz, sub_len, dil, w), (0, 2, 1, 3)).reshape(T, w)

        def from_res(t):
            if dil == 1:
                return t
            w = t.shape[-1]
            return jnp.transpose(t.reshape(bsz, dil, sub_len, w), (0, 2, 1, 3)).reshape(T, w)

        o, lse = _dilated_branch(to_res(qkv[:, :D]), to_res(qkv[:, D:2 * D]), to_res(qkv[:, 2 * D:]),
                                 bsz * dil, half=half, nh=nh, name=f"dilated_{dil}_{tag}")
        outs.append(from_res(o))
        lses.append(from_res(lse))
    return _merge_proj(outs, lses, w_out, h, nh=nh, name=f"odd_out_{tag}")


def kernel(x, mem, norm_mix_g, norm_xa_g, norm_mem_g, xa_wq, xa_wkv, xa_wo, norm_ffn_g, ffn_w13, ffn_w2, ab_w_in, ab_w_out, s5_lambda_re, s5_lambda_im, s5_log_step, s5_b_re, s5_b_im, s5_c_re, s5_c_im, s5_d, s5_glu_w, s5_glu_b, diff_lambda, diff_subln_g, c_w_qkv, c_w_out, final_norm_g):
    bsz, S, D = x.shape
    T = bsz * S
    depth = norm_mix_g.shape[0]
    mem_len = mem.shape[1]
    rope_tabs = _rope_tables(S)
    n_steps = S // (S5_CHUNK * S5_SEGMENTS)
    bf = lambda w: w.astype(BF16)

    h = x.reshape(T, D)
    mem2 = mem.reshape(bsz * mem_len, D)
    for layer in range(depth):
        i = layer // 2
        if layer % 2 == 0:
            lambda_init = 0.8 - 0.6 * math.exp(-0.3 * layer)
            tabs = _s5_tables(s5_lambda_re[i], s5_lambda_im[i], s5_log_step[i], s5_b_re[i], s5_b_im[i],
                              s5_c_re[i], s5_c_im[i], s5_d[i], n_steps)
            h = _even_mixer(h, norm_mix_g[layer], bf(ab_w_in[i]), bf(ab_w_out[i]), tabs, bf(s5_glu_w[i]),
                            s5_glu_b[i], diff_lambda[i], diff_subln_g[i], lambda_init, rope_tabs, bsz, layer)
        else:
            h = _odd_mixer(h, norm_mix_g[layer], bf(c_w_qkv[i]), bf(c_w_out[i]), rope_tabs, bsz, layer)

        q = _norm_matmul(h, norm_xa_g[layer], bf(xa_wq[layer]),
                         segs=((0, D, False, (D // XA_HEADS) ** -0.5),), name=f"xa_q_{layer}")
        kv = _norm_matmul(mem2, norm_mem_g[layer], bf(xa_wkv[layer]), name=f"xa_kv_{layer}")
        o = _cross_attn(q, kv, bsz, name=f"xa_core_{layer}")
        h = _matmul_res([o], bf(xa_wo[layer]), h, tn=1024, name=f"xa_out_{layer}")

        hid = _swiglu_up(h, norm_ffn_g[layer], bf(ffn_w13[layer]), name=f"ffn_up_{layer}")
        h = _matmul_res([hid], bf(ffn_w2[layer]), h, tn=512, name=f"ffn_down_{layer}")
    return _final_norm(h, final_norm_g, name="final_norm").reshape(bsz, S, D)
```

```python
import functools
import math

import jax
import jax.numpy as jnp
from jax import lax
from jax.experimental import pallas as pl
from jax.experimental.pallas import tpu as pltpu

F32 = jnp.float32
BF16 = jnp.bfloat16

NORM_EPS = 1e-6
MASK_VALUE = -1e30
LOG2_E = math.log2(math.e)
ROPE_THETA = 500000.0
ROPE_HALF = 16
HEAD_DIM = 128
XA_HEADS = 4
DILATED_BRANCHES = ((128, 1), (512, 4), (2048, 16))
S5_CHUNK = 32
S5_SEGMENTS = 8
PERM_GROUP = 256
LANES = 128
SUBLANES = 8
VMEM_CAP_BYTES = 56 * 1024 * 1024


def _pick_tile(n, pref, quantum):
    t = (min(n, pref) // quantum) * quantum
    while t >= quantum:
        if n % t == 0:
            return t
        t -= quantum
    return n


def _params(semantics, vmem_bytes):
    return pltpu.CompilerParams(
        dimension_semantics=semantics,
        vmem_limit_bytes=int(min(VMEM_CAP_BYTES, max(32 * 1024 * 1024, vmem_bytes))))


def _rms_normalize(x, g):
    ms = jnp.mean(x * x, axis=-1, keepdims=True)
    return (x * lax.rsqrt(ms + NORM_EPS)) * g


def _rope_lanes(y, cf, sa, sb):
    outs = []
    for c in range(y.shape[1] // LANES):
        yc = y[:, c * LANES:(c + 1) * LANES]
        outs.append(yc * cf + pltpu.roll(yc, LANES - ROPE_HALF, 1) * sa + pltpu.roll(yc, ROPE_HALF, 1) * sb)
    return outs[0] if len(outs) == 1 else jnp.concatenate(outs, axis=1)


def _rope_tables(seq):
    pos = jnp.arange(seq, dtype=F32)
    inv = ROPE_THETA ** (-jnp.arange(0, 2 * ROPE_HALF, 2, dtype=F32) / (2 * ROPE_HALF))
    ang = pos[:, None] * inv[None, :]
    cos, sin = jnp.cos(ang), jnp.sin(ang)
    rest = LANES - 2 * ROPE_HALF
    zeros = jnp.zeros((seq, ROPE_HALF), F32)
    cf = jnp.concatenate([cos, cos, jnp.ones((seq, rest), F32)], axis=1)
    sa = jnp.concatenate([-sin, zeros, jnp.zeros((seq, rest), F32)], axis=1)
    sb = jnp.concatenate([zeros, sin, jnp.zeros((seq, rest), F32)], axis=1)
    return cf, sa, sb


def _norm_matmul_kernel(*refs, segs, tn, use_rope):
    if use_rope:
        x_ref, g_ref, w_ref, cf_ref, sa_ref, sb_ref, o_ref, xn_ref = refs
    else:
        x_ref, g_ref, w_ref, o_ref, xn_ref = refs
    j = pl.program_id(1)

    @pl.when(j == 0)
    def _():
        xn_ref[...] = _rms_normalize(x_ref[...], g_ref[...]).astype(BF16)

    y = jnp.dot(xn_ref[...], w_ref[...], preferred_element_type=F32)
    for lo, hi, rope, scale in segs:
        @pl.when(jnp.logical_and(j >= lo // tn, j < hi // tn))
        def _(rope=rope, scale=scale):
            z = y
            if rope:
                z = _rope_lanes(z, cf_ref[...], sa_ref[...], sb_ref[...])
            if scale != 1.0:
                z = z * scale
            o_ref[...] = z.astype(o_ref.dtype)


def _norm_matmul(x, g, w, *, segs=(), rope_tabs=None, seq=None, tm=1024, tn=1024, name):
    M, D = x.shape
    N = w.shape[1]
    tm = _pick_tile(M if seq is None else seq, tm, SUBLANES)
    tn = _pick_tile(math.gcd(N, *[b for s in segs for b in s[:2]]), tn, LANES)
    full, pos = [], 0
    for lo, hi, rope, scale in sorted(segs):
        if lo > pos:
            full.append((pos, lo, False, 1.0))
        full.append((lo, hi, rope, scale))
        pos = hi
    if pos < N:
        full.append((pos, N, False, 1.0))
    for lo, hi, _, _ in full:
        assert lo % tn == 0 and hi % tn == 0, (lo, hi, tn)
    use_rope = any(s[2] for s in full)
    in_specs = [pl.BlockSpec((tm, D), lambda i, j: (i, 0)),
                pl.BlockSpec((1, D), lambda i, j: (0, 0)),
                pl.BlockSpec((D, tn), lambda i, j: (0, j))]
    args = [x, g.reshape(1, D).astype(F32), w]
    if use_rope:
        nseq = seq // tm
        in_specs += [pl.BlockSpec((tm, LANES), lambda i, j: (i % nseq, 0))] * 3
        args += list(rope_tabs)
    vmem = 2 * tm * D * 4 + tm * D * 2 + 2 * D * tn * 2 + 2 * tm * tn * 2 + 3 * tm * tn * 4 + (4 << 20)
    return pl.pallas_call(
        functools.partial(_norm_matmul_kernel, segs=tuple(full), tn=tn, use_rope=use_rope),
        grid=(M // tm, N // tn),
        in_specs=in_specs,
        out_specs=pl.BlockSpec((tm, tn), lambda i, j: (i, j)),
        out_shape=jax.ShapeDtypeStruct((M, N), BF16),
        scratch_shapes=[pltpu.VMEM((tm, D), BF16)],
        compiler_params=_params(("parallel", "arbitrary"), vmem),
        name=name,
    )(*args)


def _matmul_res_kernel(*refs, n_a):
    a_refs, w_refs = refs[:n_a], refs[n_a:2 * n_a]
    res_ref, o_ref = refs[2 * n_a], refs[2 * n_a + 1]
    acc = res_ref[...]
    for a_ref, w_ref in zip(a_refs, w_refs):
        acc = acc + jnp.dot(a_ref[...], w_ref[...], preferred_element_type=F32)
    o_ref[...] = acc


def _matmul_res(a_list, w, res, *, tm=1024, tn=512, name):
    n_a = len(a_list)
    M, K = a_list[0].shape
    N = w.shape[1]
    tm = _pick_tile(M, tm, SUBLANES)
    tn = _pick_tile(N, tn, LANES)
    in_specs = [pl.BlockSpec((tm, K), lambda i, j: (i, 0)) for _ in range(n_a)]
    in_specs += [pl.BlockSpec((K, tn), lambda i, j, r=r: (r, j)) for r in range(n_a)]
    in_specs += [pl.BlockSpec((tm, tn), lambda i, j: (i, j))]
    vmem = n_a * (2 * tm * K * 2 + 2 * K * tn * 2) + 5 * tm * tn * 4 + (4 << 20)
    return pl.pallas_call(
        functools.partial(_matmul_res_kernel, n_a=n_a),
        grid=(M // tm, N // tn),
        in_specs=in_specs,
        out_specs=pl.BlockSpec((tm, tn), lambda i, j: (i, j)),
        out_shape=jax.ShapeDtypeStruct((M, N), F32),
        compiler_params=_params(("parallel", "parallel"), vmem),
        name=name,
    )(*a_list, *([w] * n_a), res)


def _swiglu_up_kernel(x_ref, g_ref, w1_ref, w3_ref, o_ref, xn_ref):
    @pl.when(pl.program_id(1) == 0)
    def _():
        xn_ref[...] = _rms_normalize(x_ref[...], g_ref[...]).astype(BF16)

    xn = xn_ref[...]
    a = jnp.dot(xn, w1_ref[...], preferred_element_type=F32)
    b = jnp.dot(xn, w3_ref[...], preferred_element_type=F32)
    o_ref[...] = (a * jax.nn.sigmoid(a) * b).astype(o_ref.dtype)


def _swiglu_up(x, g, w13, *, tm=1024, tn=512, name):
    M, D = x.shape
    H = w13.shape[1] // 2
    tm = _pick_tile(M, tm, SUBLANES)
    tn = _pick_tile(H, tn, LANES)
    nj = H // tn
    vmem = 2 * tm * D * 4 + tm * D * 2 + 4 * D * tn * 2 + 2 * tm * tn * 2 + 4 * tm * tn * 4 + (4 << 20)
    return pl.pallas_call(
        _swiglu_up_kernel,
        grid=(M // tm, nj),
        in_specs=[pl.BlockSpec((tm, D), lambda i, j: (i, 0)),
                  pl.BlockSpec((1, D), lambda i, j: (0, 0)),
                  pl.BlockSpec((D, tn), lambda i, j: (0, j)),
                  pl.BlockSpec((D, tn), lambda i, j: (0, j + nj))],
        out_specs=pl.BlockSpec((tm, tn), lambda i, j: (i, j)),
        out_shape=jax.ShapeDtypeStruct((M, H), BF16),
        scratch_shapes=[pltpu.VMEM((tm, D), BF16)],
        compiler_params=_params(("parallel", "arbitrary"), vmem),
        name=name,
    )(x, g.reshape(1, D).astype(F32), w13, w13)


def _cross_attn_kernel(q_ref, k_ref, v_ref, o_ref, *, nh):
    dh = q_ref.shape[1] // nh
    for h in range(nh):
        sl = slice(h * dh, (h + 1) * dh)
        s = lax.dot_general(q_ref[:, sl], k_ref[:, sl], (((1,), (1,)), ((), ())), preferred_element_type=F32)
        m = jnp.max(s, axis=-1, keepdims=True)
        e = jnp.exp(s - m)
        den = jnp.sum(e, axis=-1, keepdims=True)
        o = jnp.dot(e.astype(BF16), v_ref[:, sl], preferred_element_type=F32)
        o_ref[:, sl] = (o / den).astype(o_ref.dtype)


def _cross_attn(q, kv, bsz, *, tq=1024, name):
    T, D = q.shape
    S = T // bsz
    mem_len = kv.shape[0] // bsz
    tq = _pick_tile(S, tq, SUBLANES)
    nq = S // tq
    kv3 = kv.reshape(bsz, mem_len, 2 * D)
    vmem = 4 * tq * D * 2 + 4 * mem_len * D * 2 + 4 * tq * mem_len * 4 + (8 << 20)
    return pl.pallas_call(
        functools.partial(_cross_attn_kernel, nh=XA_HEADS),
        grid=(bsz, nq),
        in_specs=[pl.BlockSpec((tq, D), lambda b, i: (b * nq + i, 0)),
                  pl.BlockSpec((None, mem_len, D), lambda b, i: (b, 0, 0)),
                  pl.BlockSpec((None, mem_len, D), lambda b, i: (b, 0, 1))],
        out_specs=pl.BlockSpec((tq, D), lambda b, i: (b * nq + i, 0)),
        out_shape=jax.ShapeDtypeStruct((T, D), BF16),
        compiler_params=_params(("parallel", "parallel"), vmem),
        name=name,
    )(q, kv3, kv3)


def _diff_attn_kernel(q_ref, k_ref, v_ref, lv_ref, sg_ref, o_ref, vt_ref, *, tk, unroll, lambda_init):
    tq = q_ref.shape[0]
    seq = k_ref.shape[0]
    dv = v_ref.shape[1]
    dh = dv // 2

    @pl.when(pl.program_id(2) == 0)
    def _():
        for c in range(seq // tk):
            vt_ref[:, c * tk:(c + 1) * tk] = v_ref[c * tk:(c + 1) * tk, :].astype(F32).T.astype(BF16)

    q = q_ref[...]
    nk = seq // tk

    def scores(kk):
        kc = k_ref[pl.ds(pl.multiple_of(kk * tk, tk), tk), :]
        return tuple(lax.dot_general(kc[:, c * dh:(c + 1) * dh], q[:, c * dh:(c + 1) * dh],
                                     (((1,), (1,)), ((), ())), preferred_element_type=F32)
                     for c in range(2))

    def body(kk, carry):
        st_next = scores(jnp.minimum(kk + 1, nk - 1))
        vt = vt_ref[:, pl.ds(pl.multiple_of(kk * tk, tk), tk)]
        new = []
        for c in range(2):
            st, m, l, acc = carry[c]
            m_new = jnp.maximum(m, jnp.max(st, axis=0, keepdims=True))
            alpha = jnp.exp2(m - m_new)
            pt = jnp.exp2(st - m_new)
            l = alpha * l + jnp.sum(pt, axis=0, keepdims=True)
            acc = alpha * acc + jnp.dot(vt, pt.astype(BF16), preferred_element_type=F32)
            new.append((st_next[c], m_new, l, acc))
        return tuple(new)

    st0 = scores(0)
    init = tuple((st0[c], jnp.full((1, tq), MASK_VALUE, F32), jnp.zeros((1, tq), F32),
                  jnp.zeros((dv, tq), F32)) for c in range(2))
    (_, _, l1, a1), (_, _, l2, a2) = lax.fori_loop(0, nk, body, init, unroll=unroll)
    lv = lv_ref[...]
    lam = (jnp.exp(jnp.sum(lv[0:1] * lv[1:2], axis=-1, keepdims=True))
           - jnp.exp(jnp.sum(lv[2:3] * lv[3:4], axis=-1, keepdims=True)) + lambda_init)
    ot = a1 / l1 - lam * (a2 / l2)
    ms = jnp.mean(ot * ot, axis=0, keepdims=True)
    ot = ot * lax.rsqrt(ms + NORM_EPS)
    o_ref[...] = (ot.T * sg_ref[...] * (1.0 - lambda_init)).astype(o_ref.dtype)


def _diff_attn(proj, lam_vec, subln_g, bsz, *, s5_width, n_heads, lambda_init, tq=512, tk=512, unroll=1, name):
    T, ncol = proj.shape
    S = T // bsz
    dv = 2 * HEAD_DIM
    width = n_heads * dv
    q0, k0, v0 = s5_width // dv, (s5_width + width) // dv, (s5_width + 2 * width) // dv
    tq = _pick_tile(S, tq, SUBLANES)
    tk = _pick_tile(S, tk, LANES)
    nq = S // tq
    proj3 = proj.reshape(bsz, S, ncol)
    vmem = 4 * tq * dv * 2 + 4 * S * dv * 2 + 6 * tq * tk * 4 + 8 * tq * dv * 4 + (8 << 20)
    return pl.pallas_call(
        functools.partial(_diff_attn_kernel, tk=tk, unroll=unroll, lambda_init=lambda_init),
        grid=(bsz, n_heads, nq),
        in_specs=[pl.BlockSpec((None, tq, dv), lambda b, h, i: (b, i, q0 + h)),
                  pl.BlockSpec((None, S, dv), lambda b, h, i: (b, 0, k0 + h)),
                  pl.BlockSpec((None, S, dv), lambda b, h, i: (b, 0, v0 + h)),
                  pl.BlockSpec((4, HEAD_DIM), lambda b, h, i: (0, 0)),
                  pl.BlockSpec((1, dv), lambda b, h, i: (0, 0))],
        out_specs=pl.BlockSpec((tq, dv), lambda b, h, i: (b * nq + i, h)),
        out_shape=jax.ShapeDtypeStruct((T, width), BF16),
        scratch_shapes=[pltpu.VMEM((dv, S), BF16)],
        compiler_params=_params(("parallel", "parallel", "arbitrary"), vmem),
        name=name,
    )(proj3, proj3, proj3, lam_vec.astype(F32), subln_g.reshape(1, dv).astype(F32))


def _s5_tables(lam_re, lam_im, log_step, b_re, b_im, c_re, c_im, d_skip, n_steps):
    hi = lax.Precision.HIGHEST
    L = S5_CHUNK
    _, G, P = lam_re.shape
    N = b_re.shape[-1]
    lr, li = lam_re.astype(F32), lam_im.astype(F32)
    step = jnp.exp(log_step.astype(F32))[..., None]
    ar, ai = lr * step, li * step

    def cexp(t, axes=()):
        a_r, a_i = jnp.expand_dims(ar, axes), jnp.expand_dims(ai, axes)
        mag = jnp.exp(t * a_r)
        return mag * jnp.cos(t * a_i), mag * jnp.sin(t * a_i)

    def cmul(xr, xi, yr, yi):
        return xr * yr - xi * yi, xr * yi + xi * yr

    er, ei = cexp(1.0)
    den = lr * lr + li * li
    qr, qi = ((er - 1.0) * lr + ei * li) / den, (ei * lr - (er - 1.0) * li) / den
    br, bi = cmul(qr[..., None], qi[..., None], b_re.astype(F32), b_im.astype(F32))
    cr, ci = c_re.astype(F32), c_im.astype(F32)
    tau = jnp.arange(L + 1, dtype=F32)[None, None, :, None]
    pr, pi = cexp(tau, axes=2)

    wr, wi = cmul(pr[:, :, :L, :, None], pi[:, :, :L, :, None], br[:, :, None], bi[:, :, None])
    kern = (jnp.einsum('dgnp,dgtpm->dgtnm', cr, wr, precision=hi)
            - jnp.einsum('dgnp,dgtpm->dgtnm', ci, wi, precision=hi))
    kf, kb = kern[0], kern[1]
    kcat = jnp.concatenate([kb[:, :0:-1], (kf[:, :1] + kb[:, :1]), kf[:, 1:]], axis=1)
    idx = jnp.arange(L)[None, :] - jnp.arange(L)[:, None] + (L - 1)
    toep = kcat[:, idx]
    toep = jnp.transpose(toep, (0, 1, 4, 2, 3)).reshape(G, L * N, L * N)

    def _in_cols(w):
        return jnp.transpose(w, (0, 1, 3, 2)).reshape(G, L * N, P)
    w_in = jnp.concatenate([_in_cols(wr[0][:, ::-1]), _in_cols(wr[1]),
                            _in_cols(wi[0][:, ::-1]), _in_cols(wi[1])], axis=2)

    of_r, of_i = cmul(pr[0][:, 1:, None, :], pi[0][:, 1:, None, :], cr[0][:, None], ci[0][:, None])
    ob_r, ob_i = cmul(pr[1][:, ::-1][:, :L, None, :], pi[1][:, ::-1][:, :L, None, :],
                      cr[1][:, None], ci[1][:, None])
    def _out_rows(w):
        return jnp.transpose(w, (0, 3, 1, 2)).reshape(G, P, L * N)
    w_out = jnp.concatenate([_out_rows(of_r), _out_rows(ob_r), -_out_rows(of_i), -_out_rows(ob_i)], axis=1)

    def _lanes(z):
        return jnp.concatenate([z[0], z[1]], axis=-1)
    a_chunk, a_seg = cexp(float(L)), cexp(float(L * n_steps))
    decay = jnp.stack([_lanes(a_chunk[0]), _lanes(a_chunk[1]), _lanes(a_seg[0]), _lanes(a_seg[1])], axis=1)
    jj = float(L) * jnp.arange(n_steps, dtype=F32)
    jj = jnp.stack([jj, jj[::-1]], axis=0)[:, None, :, None]
    powers = jnp.stack([_lanes(p) for p in cexp(jj, axes=2)], axis=1)
    dvec = jnp.tile(d_skip.astype(F32).reshape(G, 1, N), (1, L, 1)).reshape(G, 1, L * N)
    return toep.astype(BF16), w_in.astype(BF16), w_out.astype(BF16), decay, powers, dvec


def _s5_kernel(u_ref, toep_ref, win_ref, wout_ref, decay_ref, pow_ref, dvec_ref, o_ref, *, n_steps, bsz, gb):
    rb = bsz * SUBLANES
    for gi in range(gb):
        u = u_ref[gi]
        y = jnp.dot(u, toep_ref[gi], preferred_element_type=F32)
        z = jnp.dot(u, win_ref[gi], preferred_element_type=F32)
        half = z.shape[1] // 2
        zr, zi = z[:, :half], z[:, half:]
        dec = decay_ref[gi]
        ar, ai, sr, si = dec[0:1], dec[1:2], dec[2:3], dec[3:4]
        is_fwd = lax.broadcasted_iota(jnp.int32, (rb, half), 1) < half // 2

        xr = jnp.zeros((rb, half), F32)
        xi = jnp.zeros((rb, half), F32)
        ent_r, ent_i = [], []
        for j in range(n_steps):
            jb = n_steps - 1 - j
            ent_r.append(xr)
            ent_i.append(xi)
            zrj = jnp.where(is_fwd, zr[j * rb:(j + 1) * rb], zr[jb * rb:(jb + 1) * rb])
            zij = jnp.where(is_fwd, zi[j * rb:(j + 1) * rb], zi[jb * rb:(jb + 1) * rb])
            xr, xi = ar * xr - ai * xi + zrj, ar * xi + ai * xr + zij

        sub = lax.broadcasted_iota(jnp.int32, (SUBLANES, half), 0)
        fwd8 = lax.broadcasted_iota(jnp.int32, (SUBLANES, half), 1) < half // 2
        keep = (fwd8 & (sub > 0)) | (jnp.logical_not(fwd8) & (sub < SUBLANES - 1))
        car_r, car_i = [], []
        for b in range(bsz):
            er, ei = xr[b * SUBLANES:(b + 1) * SUBLANES], xi[b * SUBLANES:(b + 1) * SUBLANES]
            cr = jnp.zeros((SUBLANES, half), F32)
            ci = jnp.zeros((SUBLANES, half), F32)
            for _ in range(SUBLANES - 1):
                tr, ti = sr * cr - si * ci + er, sr * ci + si * cr + ei
                cr = jnp.where(keep, jnp.where(fwd8, pltpu.roll(tr, 1, 0), pltpu.roll(tr, SUBLANES - 1, 0)), 0.0)
                ci = jnp.where(keep, jnp.where(fwd8, pltpu.roll(ti, 1, 0), pltpu.roll(ti, SUBLANES - 1, 0)), 0.0)
            car_r.append(cr)
            car_i.append(ci)
        gr = car_r[0] if bsz == 1 else jnp.concatenate(car_r, axis=0)
        gi_ = car_i[0] if bsz == 1 else jnp.concatenate(car_i, axis=0)

        pr_all, pi_all = pow_ref[gi, 0], pow_ref[gi, 1]
        rows_r, rows_i = [], []
        for j in range(n_steps):
            jb = n_steps - 1 - j
            lr = jnp.where(is_fwd, ent_r[j], ent_r[jb])
            li = jnp.where(is_fwd, ent_i[j], ent_i[jb])
            pr, pi = pr_all[j:j + 1], pi_all[j:j + 1]
            rows_r.append(lr + pr * gr - pi * gi_)
            rows_i.append(li + pr * gi_ + pi * gr)
        state = jnp.concatenate([jnp.concatenate(rows_r, axis=0), jnp.concatenate(rows_i, axis=0)], axis=1)
        y = y + jnp.dot(state.astype(BF16), wout_ref[gi], preferred_element_type=F32)
        y = y + u.astype(F32) * dvec_ref[gi]
        o_ref[gi] = y.astype(o_ref.dtype)


def _s5_mix(u, tables, bsz, *, gb=4, name):
    toep, w_in, w_out, decay, powers, dvec = tables
    T, width = u.shape
    S = T // bsz
    G = toep.shape[0]
    N = width // G
    L = S5_CHUNK
    n_steps = S // (L * S5_SEGMENTS)
    rows = n_steps * bsz * S5_SEGMENTS
    gb = _pick_tile(G, gb, 1)
    ug = u.reshape(bsz, S5_SEGMENTS, n_steps, L, G, N)
    ug = jnp.transpose(ug, (4, 2, 0, 1, 3, 5)).reshape(G, rows, L * N)
    blk = lambda *shape: pl.BlockSpec((gb,) + shape, lambda g: (g,) + (0,) * len(shape))
    vmem = 2 * gb * (2 * rows * L * N * 2 + (L * N) ** 2 * 2 + 2 * L * N * 4 * 64 * 2) + (24 << 20)
    yg = pl.pallas_call(
        functools.partial(_s5_kernel, n_steps=n_steps, bsz=bsz, gb=gb),
        grid=(G // gb,),
        in_specs=[blk(rows, L * N), blk(L * N, L * N), blk(L * N, w_in.shape[2]), blk(w_out.shape[1], L * N),
                  blk(4, decay.shape[2]), blk(2, n_steps, powers.shape[3]), blk(1, L * N)],
        out_specs=blk(rows, L * N),
        out_shape=jax.ShapeDtypeStruct((G, rows, L * N), BF16),
        compiler_params=_params(("parallel",), vmem),
        name=name,
    )(ug, toep, w_in, w_out, decay, powers, dvec)
    yg = yg.reshape(G, n_steps, bsz, S5_SEGMENTS, L, N)
    return jnp.transpose(yg, (2, 3, 1, 4, 0, 5)).reshape(T, width)


def _s5_glu_kernel(y_ref, w_ref, b_ref, o_ref):
    g = jax.nn.gelu(y_ref[...].astype(F32), approximate=True)
    z = jnp.dot(g.astype(BF16), w_ref[...], preferred_element_type=F32) + b_ref[...]
    o_ref[...] = (g * jax.nn.sigmoid(z)).astype(o_ref.dtype)


def _s5_glu(y, w, b, *, tm=1024, name):
    M, W = y.shape
    tm = _pick_tile(M, tm, SUBLANES)
    vmem = 4 * tm * W * 2 + 2 * W * W * 2 + 4 * tm * W * 4 + (4 << 20)
    return pl.pallas_call(
        _s5_glu_kernel,
        grid=(M // tm,),
        in_specs=[pl.BlockSpec((tm, W), lambda i: (i, 0)),
                  pl.BlockSpec((W, W), lambda i: (0, 0)),
                  pl.BlockSpec((1, W), lambda i: (0, 0))],
        out_specs=pl.BlockSpec((tm, W), lambda i: (i, 0)),
        out_shape=jax.ShapeDtypeStruct((M, W), BF16),
        compiler_params=_params(("parallel",), vmem),
        name=name,
    )(y, w, b.reshape(1, W).astype(F32))


def _dilated_kernel(q_ref, k0, k1, k2, k3, v0, v1, v2, v3, o_ref, lse_ref, *, sub_len, half, nh):
    i = pl.program_id(1)
    tq = q_ref.shape[0]
    kb = k0.shape[0]
    dh = q_ref.shape[1] // nh
    qpos = i * tq + lax.broadcasted_iota(jnp.int32, (tq, 4 * kb), 0)
    kpos = (2 * i - 1) * kb + lax.broadcasted_iota(jnp.int32, (tq, 4 * kb), 1)
    valid = (jnp.abs(kpos - qpos) <= half) & (kpos >= 0) & (kpos < sub_len)
    head_lane = lax.broadcasted_iota(jnp.int32, (tq, nh), 1)
    lse_all = jnp.zeros((tq, nh), F32)
    for h in range(nh):
        sl = slice(h * dh, (h + 1) * dh)
        kc = jnp.concatenate([k0[:, sl], k1[:, sl], k2[:, sl], k3[:, sl]], axis=0)
        vc = jnp.concatenate([v0[:, sl], v1[:, sl], v2[:, sl], v3[:, sl]], axis=0)
        s = lax.dot_general(q_ref[:, sl], kc, (((1,), (1,)), ((), ())), preferred_element_type=F32)
        s = jnp.where(valid, s, MASK_VALUE)
        m = jnp.max(s, axis=-1, keepdims=True)
        e = jnp.exp(s - m)
        den = jnp.sum(e, axis=-1, keepdims=True)
        o = jnp.dot(e.astype(BF16), vc, preferred_element_type=F32)
        o_ref[:, sl] = (o / den).astype(o_ref.dtype)
        lse_all = jnp.where(head_lane == h, m + jnp.log(den), lse_all)
    lse_ref[...] = lse_all


def _dilated_branch(q, k, v, cols, D, n_seq, *, half, nh, name):
    T = q.shape[0]
    sub_len = T // n_seq
    tq, kb = 2 * half, half
    nq, nkb = sub_len // tq, sub_len // kb
    cq, ck, cv = cols

    def kv_spec(off, c):
        return pl.BlockSpec((kb, D), lambda s, i: (s * nkb + jnp.clip(2 * i - 1 + off, 0, nkb - 1), c))

    vmem = 2 * (2 * tq * D * 2 + 8 * kb * D * 2) + (16 << 20)
    return pl.pallas_call(
        functools.partial(_dilated_kernel, sub_len=sub_len, half=half, nh=nh),
        grid=(n_seq, nq),
        in_specs=[pl.BlockSpec((tq, D), lambda s, i: (s * nq + i, cq))]
                 + [kv_spec(off, ck) for off in range(4)] + [kv_spec(off, cv) for off in range(4)],
        out_specs=[pl.BlockSpec((tq, D), lambda s, i: (s * nq + i, 0)),
                   pl.BlockSpec((tq, nh), lambda s, i: (s * nq + i, 0))],
        out_shape=[jax.ShapeDtypeStruct((T, D), BF16), jax.ShapeDtypeStruct((T, nh), F32)],
        compiler_params=_params(("parallel", "parallel"), vmem),
        name=name,
    )(q, k, k, k, k, v, v, v, v)


def _residue_perm(dil):
    per = PERM_GROUP // dil
    rows = jnp.arange(PERM_GROUP)
    src = (rows % per) * dil + rows // per
    return (src[:, None] == rows[None, :]).astype(BF16)


def _to_residue_kernel(*refs, dil, n_in):
    p_ref, x_refs, o_refs = refs[0], refs[1:1 + n_in], refs[1 + n_in:]
    per = PERM_GROUP // dil
    for x_ref, o_ref in zip(x_refs, o_refs):
        for g in range(x_ref.shape[0] // PERM_GROUP):
            y = jnp.dot(p_ref[...], x_ref[g * PERM_GROUP:(g + 1) * PERM_GROUP, :],
                        preferred_element_type=F32).astype(o_ref.dtype)
            for r in range(dil):
                o_ref[r, g * per:(g + 1) * per, :] = y[r * per:(r + 1) * per]


def _to_residue(x, cols, width, dil, bsz, *, tb=1024, name):
    T = x.shape[0]
    S = T // bsz
    tb = _pick_tile(S, tb, PERM_GROUP)
    nb = S // tb
    n_in = len(cols)
    out4 = jax.ShapeDtypeStruct((bsz, dil, S // dil, width), x.dtype)
    outs = pl.pallas_call(
        functools.partial(_to_residue_kernel, dil=dil, n_in=n_in),
        grid=(bsz, nb),
        in_specs=[pl.BlockSpec((PERM_GROUP, PERM_GROUP), lambda b, i: (0, 0))]
                 + [pl.BlockSpec((tb, width), lambda b, i, c=c: (b * nb + i, c)) for c in cols],
        out_specs=[pl.BlockSpec((None, dil, tb // dil, width), lambda b, i: (b, 0, i, 0))] * n_in,
        out_shape=[out4] * n_in,
        compiler_params=_params(("parallel", "parallel"), n_in * 4 * tb * width * 2 + (8 << 20)),
        name=name,
    )(_residue_perm(dil), *([x] * n_in))
    return [o.reshape(T, width) for o in outs]


def _from_residue_kernel(p_ref, x_ref, o_ref, *, dil):
    per = PERM_GROUP // dil
    for g in range(o_ref.shape[0] // PERM_GROUP):
        xg = jnp.concatenate([x_ref[r, g * per:(g + 1) * per, :] for r in range(dil)], axis=0)
        o_ref[g * PERM_GROUP:(g + 1) * PERM_GROUP, :] = jnp.dot(
            p_ref[...], xg, preferred_element_type=F32).astype(o_ref.dtype)


def _from_residue(x, dil, bsz, *, tb=1024, name):
    T, width = x.shape
    S = T // bsz
    tb = _pick_tile(S, tb, PERM_GROUP)
    nb = S // tb
    return pl.pallas_call(
        functools.partial(_from_residue_kernel, dil=dil),
        grid=(bsz, nb),
        in_specs=[pl.BlockSpec((PERM_GROUP, PERM_GROUP), lambda b, i: (0, 0)),
                  pl.BlockSpec((None, dil, tb // dil, width), lambda b, i: (b, 0, i, 0))],
        out_specs=pl.BlockSpec((tb, width), lambda b, i: (b * nb + i, 0)),
        out_shape=jax.ShapeDtypeStruct((T, width), x.dtype),
        compiler_params=_params(("parallel", "parallel"), 4 * tb * width * 2 + (8 << 20)),
        name=name,
    )(_residue_perm(dil).T, x.reshape(bsz, dil, S // dil, width))


def _merge_proj_kernel(o1, o2, o3, l1, l2, l3, w_ref, res_ref, out_ref, a_ref, *, nh):
    @pl.when(pl.program_id(1) == 0)
    def _():
        lse = [l1[...], l2[...], l3[...]]
        mx = jnp.maximum(jnp.maximum(lse[0], lse[1]), lse[2])
        e = [jnp.exp(x - mx) for x in lse]
        inv = 1.0 / (e[0] + e[1] + e[2])
        dh = a_ref.shape[1] // nh
        for h in range(nh):
            sl = slice(h * dh, (h + 1) * dh)
            acc = None
            for eb, ob in zip(e, (o1, o2, o3)):
                term = (eb[:, h:h + 1] * inv[:, h:h + 1]) * ob[:, sl].astype(F32)
                acc = term if acc is None else acc + term
            a_ref[:, sl] = acc.astype(BF16)

    out_ref[...] = res_ref[...] + jnp.dot(a_ref[...], w_ref[...], preferred_element_type=F32)


def _merge_proj(outs, lses, w, res, *, nh, tm=512, tn=1024, name):
    M, D = outs[0].shape
    N = w.shape[1]
    tm = _pick_tile(M, tm, SUBLANES)
    tn = _pick_tile(N, tn, LANES)
    row = lambda width: pl.BlockSpec((tm, width), lambda i, j: (i, 0))
    vmem = 3 * 2 * tm * D * 2 + tm * D * 2 + 2 * D * tn * 2 + 5 * tm * tn * 4 + 4 * tm * D * 4 + (4 << 20)
    return pl.pallas_call(
        functools.partial(_merge_proj_kernel, nh=nh),
        grid=(M // tm, N // tn),
        in_specs=[row(D)] * 3 + [row(nh)] * 3
                 + [pl.BlockSpec((D, tn), lambda i, j: (0, j)), pl.BlockSpec((tm, tn), lambda i, j: (i, j))],
        out_specs=pl.BlockSpec((tm, tn), lambda i, j: (i, j)),
        out_shape=jax.ShapeDtypeStruct((M, N), F32),
        scratch_shapes=[pltpu.VMEM((tm, D), BF16)],
        compiler_params=_params(("parallel", "arbitrary"), vmem),
        name=name,
    )(*outs, *lses, w, res)


def _final_norm_kernel(x_ref, g_ref, o_ref):
    o_ref[...] = _rms_normalize(x_ref[...], g_ref[...])


def _final_norm(x, g, *, tm=1024, name):
    M, D = x.shape
    tm = _pick_tile(M, tm, SUBLANES)
    return pl.pallas_call(
        _final_norm_kernel,
        grid=(M // tm,),
        in_specs=[pl.BlockSpec((tm, D), lambda i: (i, 0)), pl.BlockSpec((1, D), lambda i: (0, 0))],
        out_specs=pl.BlockSpec((tm, D), lambda i: (i, 0)),
        out_shape=jax.ShapeDtypeStruct((M, D), F32),
        compiler_params=_params(("parallel",), 6 * tm * D * 4 + (4 << 20)),
        name=name,
    )(x, g.reshape(1, D).astype(F32))


def _even_mixer(h, g, w_in, w_out, s5_tabs, glu_w, glu_b, lam_vec, subln_g, lambda_init, rope_tabs, bsz, tag):
    T, _ = h.shape
    S = T // bsz
    s5_width = glu_w.shape[0]
    width = (w_in.shape[1] - s5_width) // 3
    n_heads = width // (2 * HEAD_DIM)
    segs = ((s5_width, s5_width + width, True, HEAD_DIM ** -0.5 * LOG2_E),
            (s5_width + width, s5_width + 2 * width, True, 1.0))
    proj = _norm_matmul(h, g, w_in, segs=segs, rope_tabs=rope_tabs, seq=S, name=f"even_in_{tag}")
    y_s5 = _s5_mix(proj[:, :s5_width], s5_tabs, bsz, name=f"s5_{tag}")
    y_s5 = _s5_glu(y_s5, glu_w, glu_b, name=f"s5_glu_{tag}")
    y_diff = _diff_attn(proj, lam_vec, subln_g, bsz, s5_width=s5_width, n_heads=n_heads,
                        lambda_init=lambda_init, unroll=8, name=f"diff_attn_{tag}")
    return _matmul_res([y_s5, y_diff], w_out, h, tn=1024, name=f"even_out_{tag}")


def _odd_mixer(h, g, w_qkv, w_out, rope_tabs, bsz, tag):
    T, D = h.shape
    S = T // bsz
    nh = D // HEAD_DIM
    segs = ((0, D, True, HEAD_DIM ** -0.5), (D, 2 * D, True, 1.0))
    qkv = _norm_matmul(h, g, w_qkv, segs=segs, rope_tabs=rope_tabs, seq=S, name=f"odd_in_{tag}")
    outs, lses = [], []
    for window, dil in DILATED_BRANCHES:
        half = window // (2 * dil)
        sub_len = S // dil

        if dil == 1:
            o, lse = _dilated_branch(qkv, qkv, qkv, (0, 1, 2), D, bsz, half=half, nh=nh,
                                     name=f"dilated_{dil}_{tag}")
        else:
            qr, kr, vr = _to_residue(qkv, (0, 1, 2), D, dil, bsz, name=f"to_residue_{dil}_{tag}")
            o, lse = _dilated_branch(qr, kr, vr, (0, 0, 0), D, bsz * dil, half=half, nh=nh,
                                     name=f"dilated_{dil}_{tag}")
            o = _from_residue(o, dil, bsz, name=f"from_residue_{dil}_{tag}")
            lse = jnp.transpose(lse.reshape(bsz, dil, sub_len, nh), (0, 2, 1, 3)).reshape(T, nh)
        outs.append(o)
        lses.append(lse)
    return _merge_proj(outs, lses, w_out, h, nh=nh, name=f"odd_out_{tag}")


def kernel(x, mem, norm_mix_g, norm_xa_g, norm_mem_g, xa_wq, xa_wkv, xa_wo, norm_ffn_g, ffn_w13, ffn_w2, ab_w_in, ab_w_out, s5_lambda_re, s5_lambda_im, s5_log_step, s5_b_re, s5_b_im, s5_c_re, s5_c_im, s5_d, s5_glu_w, s5_glu_b, diff_lambda, diff_subln_g, c_w_qkv, c_w_out, final_norm_g):
    bsz, S, D = x.shape
    T = bsz * S
    depth = norm_mix_g.shape[0]
    mem_len = mem.shape[1]
    rope_tabs = _rope_tables(S)
    n_steps = S // (S5_CHUNK * S5_SEGMENTS)
    bf = lambda w: w.astype(BF16)

    h = x.reshape(T, D)
    mem2 = mem.reshape(bsz * mem_len, D)
    for layer in range(depth):
        i = layer // 2
        if layer % 2 == 0:
            lambda_init = 0.8 - 0.6 * math.exp(-0.3 * layer)
            tabs = _s5_tables(s5_lambda_re[i], s5_lambda_im[i], s5_log_step[i], s5_b_re[i], s5_b_im[i],
                              s5_c_re[i], s5_c_im[i], s5_d[i], n_steps)
            h = _even_mixer(h, norm_mix_g[layer], bf(ab_w_in[i]), bf(ab_w_out[i]), tabs, bf(s5_glu_w[i]),
                            s5_glu_b[i], diff_lambda[i], diff_subln_g[i], lambda_init, rope_tabs, bsz, layer)
        else:
            h = _odd_mixer(h, norm_mix_g[layer], bf(c_w_qkv[i]), bf(c_w_out[i]), rope_tabs, bsz, layer)

        q = _norm_matmul(h, norm_xa_g[layer], bf(xa_wq[layer]),
                         segs=((0, D, False, (D // XA_HEADS) ** -0.5),), name=f"xa_q_{layer}")
        kv = _norm_matmul(mem2, norm_mem_g[layer], bf(xa_wkv[layer]), name=f"xa_kv_{layer}")
        o = _cross_attn(q, kv, bsz, name=f"xa_core_{layer}")
        h = _matmul_res([o], bf(xa_wo[layer]), h, tn=1024, name=f"xa_out_{layer}")

        hid = _swiglu_up(h, norm_ffn_g[layer], bf(ffn_w13[layer]), name=f"ffn_up_{layer}")
        h = _matmul_res([hid], bf(ffn_w2[layer]), h, tn=512, name=f"ffn_down_{layer}")
    return _final_norm(h, final_norm_g, name="final_norm").reshape(bsz, S, D)
```

```python
import functools
import math

import jax
import jax.numpy as jnp
from jax import lax
from jax.experimental import pallas as pl
from jax.experimental.pallas import tpu as pltpu

F32 = jnp.float32
BF16 = jnp.bfloat16

NORM_EPS = 1e-6
MASK_VALUE = -1e30
LOG2_E = math.log2(math.e)
ROPE_THETA = 500000.0
ROPE_HALF = 16
HEAD_DIM = 128
XA_HEADS = 4
DILATED_BRANCHES = ((128, 1), (512, 4), (2048, 16))
S5_CHUNK = 32
S5_SEGMENTS = 8
PERM_GROUP = 256
LANES = 128
SUBLANES = 8
VMEM_CAP_BYTES = 56 * 1024 * 1024


def _pick_tile(n, pref, quantum):
    t = (min(n, pref) // quantum) * quantum
    while t >= quantum:
        if n % t == 0:
            return t
        t -= quantum
    return n


def _params(semantics, vmem_bytes):
    return pltpu.CompilerParams(
        dimension_semantics=semantics,
        vmem_limit_bytes=int(min(VMEM_CAP_BYTES, max(32 * 1024 * 1024, vmem_bytes))))


def _rms_normalize(x, g):
    ms = jnp.mean(x * x, axis=-1, keepdims=True)
    return (x * lax.rsqrt(ms + NORM_EPS)) * g


def _rope_lanes(y, cf, sf):
    outs = []
    for c in range(y.shape[1] // LANES):
        yc = y[:, c * LANES:(c + 1) * LANES]
        outs.append(yc * cf + pltpu.roll(yc, LANES // 2, 1) * sf)
    return outs[0] if len(outs) == 1 else jnp.concatenate(outs, axis=1)


def _rope_columns(w, lo, hi):
    mid = w[..., lo:hi]
    comp = mid.reshape(mid.shape[:-1] + ((hi - lo) // LANES, LANES))
    r, h = ROPE_HALF, LANES // 2
    comp = jnp.concatenate([comp[..., :r], comp[..., 2 * r:h + r], comp[..., r:2 * r], comp[..., h + r:]], axis=-1)
    return jnp.concatenate([w[..., :lo], comp.reshape(mid.shape), w[..., hi:]], axis=-1)


def _rope_tables(seq):
    pos = jnp.arange(seq, dtype=F32)
    inv = ROPE_THETA ** (-jnp.arange(0, 2 * ROPE_HALF, 2, dtype=F32) / (2 * ROPE_HALF))
    ang = pos[:, None] * inv[None, :]
    cos, sin = jnp.cos(ang), jnp.sin(ang)
    gap = LANES // 2 - ROPE_HALF
    cf = jnp.concatenate([cos, jnp.ones((seq, gap), F32), cos, jnp.ones((seq, gap), F32)], axis=1)
    sf = jnp.concatenate([-sin, jnp.zeros((seq, gap), F32), sin, jnp.zeros((seq, gap), F32)], axis=1)
    return cf, sf


def _norm_matmul_kernel(*refs, segs, tn, use_rope):
    if use_rope:
        x_ref, g_ref, w_ref, cf_ref, sf_ref, o_ref, xn_ref = refs
    else:
        x_ref, g_ref, w_ref, o_ref, xn_ref = refs
    j = pl.program_id(1)

    @pl.when(j == 0)
    def _():
        xn_ref[...] = _rms_normalize(x_ref[...], g_ref[...]).astype(BF16)

    y = jnp.dot(xn_ref[...], w_ref[...], preferred_element_type=F32)
    for lo, hi, rope, scale in segs:
        @pl.when(jnp.logical_and(j >= lo // tn, j < hi // tn))
        def _(rope=rope, scale=scale):
            z = y
            if rope:
                z = _rope_lanes(z, cf_ref[...], sf_ref[...])
            if scale != 1.0:
                z = z * scale
            o_ref[...] = z.astype(o_ref.dtype)


def _norm_matmul(x, g, w, *, segs=(), rope_tabs=None, seq=None, tm=1024, tn=1024, name):
    M, D = x.shape
    w, layer = w
    N = w.shape[2]
    tm = _pick_tile(M if seq is None else seq, tm, SUBLANES)
    tn = _pick_tile(math.gcd(N, *[b for s in segs for b in s[:2]]), tn, LANES)
    full, pos = [], 0
    for lo, hi, rope, scale in sorted(segs):
        if lo > pos:
            full.append((pos, lo, False, 1.0))
        full.append((lo, hi, rope, scale))
        pos = hi
    if pos < N:
        full.append((pos, N, False, 1.0))
    for lo, hi, _, _ in full:
        assert lo % tn == 0 and hi % tn == 0, (lo, hi, tn)
    use_rope = any(s[2] for s in full)
    in_specs = [pl.BlockSpec((tm, D), lambda i, j: (i, 0)),
                pl.BlockSpec((1, D), lambda i, j: (0, 0)),
                pl.BlockSpec((None, D, tn), lambda i, j: (layer, 0, j))]
    args = [x, g.reshape(1, D).astype(F32), w]
    if use_rope:
        nseq = seq // tm
        in_specs += [pl.BlockSpec((tm, LANES), lambda i, j: (i % nseq, 0))] * 2
        args += list(rope_tabs)
    vmem = 2 * tm * D * 4 + tm * D * 2 + 2 * D * tn * 2 + 2 * tm * tn * 2 + 3 * tm * tn * 4 + (4 << 20)
    return pl.pallas_call(
        functools.partial(_norm_matmul_kernel, segs=tuple(full), tn=tn, use_rope=use_rope),
        grid=(M // tm, N // tn),
        in_specs=in_specs,
        out_specs=pl.BlockSpec((tm, tn), lambda i, j: (i, j)),
        out_shape=jax.ShapeDtypeStruct((M, N), BF16),
        scratch_shapes=[pltpu.VMEM((tm, D), BF16)],
        compiler_params=_params(("parallel", "arbitrary"), vmem),
        name=name,
    )(*args)


def _matmul_res_kernel(*refs, n_a):
    a_refs, w_refs = refs[:n_a], refs[n_a:2 * n_a]
    res_ref, o_ref = refs[2 * n_a], refs[2 * n_a + 1]
    acc = res_ref[...]
    for a_ref, w_ref in zip(a_refs, w_refs):
        acc = acc + jnp.dot(a_ref[...], w_ref[...], preferred_element_type=F32)
    o_ref[...] = acc


def _matmul_res(a_list, w, res, *, tm=1024, tn=512, name):
    n_a = len(a_list)
    M, K = a_list[0].shape
    w, layer = w
    N = w.shape[2]
    tm = _pick_tile(M, tm, SUBLANES)
    tn = _pick_tile(N, tn, LANES)
    in_specs = [pl.BlockSpec((tm, K), lambda i, j: (i, 0)) for _ in range(n_a)]
    in_specs += [pl.BlockSpec((None, K, tn), lambda i, j, r=r: (layer, r, j)) for r in range(n_a)]
    in_specs += [pl.BlockSpec((tm, tn), lambda i, j: (i, j))]
    vmem = n_a * (2 * tm * K * 2 + 2 * K * tn * 2) + 5 * tm * tn * 4 + (4 << 20)
    return pl.pallas_call(
        functools.partial(_matmul_res_kernel, n_a=n_a),
        grid=(M // tm, N // tn),
        in_specs=in_specs,
        out_specs=pl.BlockSpec((tm, tn), lambda i, j: (i, j)),
        out_shape=jax.ShapeDtypeStruct((M, N), F32),
        compiler_params=_params(("parallel", "parallel"), vmem),
        name=name,
    )(*a_list, *([w] * n_a), res)


def _swiglu_up_kernel(x_ref, g_ref, w1_ref, w3_ref, o_ref, xn_ref):
    @pl.when(pl.program_id(1) == 0)
    def _():
        xn_ref[...] = _rms_normalize(x_ref[...], g_ref[...]).astype(BF16)

    xn = xn_ref[...]
    a = jnp.dot(xn, w1_ref[...], preferred_element_type=F32)
    b = jnp.dot(xn, w3_ref[...], preferred_element_type=F32)
    o_ref[...] = (a * jax.nn.sigmoid(a) * b).astype(o_ref.dtype)


def _swiglu_up(x, g, w13, *, tm=1024, tn=512, name):
    M, D = x.shape
    w13, layer = w13
    H = w13.shape[2] // 2
    tm = _pick_tile(M, tm, SUBLANES)
    tn = _pick_tile(H, tn, LANES)
    nj = H // tn
    vmem = 2 * tm * D * 4 + tm * D * 2 + 4 * D * tn * 2 + 2 * tm * tn * 2 + 4 * tm * tn * 4 + (4 << 20)
    return pl.pallas_call(
        _swiglu_up_kernel,
        grid=(M // tm, nj),
        in_specs=[pl.BlockSpec((tm, D), lambda i, j: (i, 0)),
                  pl.BlockSpec((1, D), lambda i, j: (0, 0)),
                  pl.BlockSpec((None, D, tn), lambda i, j: (layer, 0, j)),
                  pl.BlockSpec((None, D, tn), lambda i, j: (layer, 0, j + nj))],
        out_specs=pl.BlockSpec((tm, tn), lambda i, j: (i, j)),
        out_shape=jax.ShapeDtypeStruct((M, H), BF16),
        scratch_shapes=[pltpu.VMEM((tm, D), BF16)],
        compiler_params=_params(("parallel", "arbitrary"), vmem),
        name=name,
    )(x, g.reshape(1, D).astype(F32), w13, w13)


def _cross_attn_kernel(q_ref, k_ref, v_ref, o_ref, *, nh):
    dh = q_ref.shape[1] // nh
    for h in range(nh):
        sl = slice(h * dh, (h + 1) * dh)
        s = lax.dot_general(q_ref[:, sl], k_ref[:, sl], (((1,), (1,)), ((), ())), preferred_element_type=F32)
        m = jnp.max(s, axis=-1, keepdims=True)
        e = jnp.exp(s - m)
        den = jnp.sum(e, axis=-1, keepdims=True)
        o = jnp.dot(e.astype(BF16), v_ref[:, sl], preferred_element_type=F32)
        o_ref[:, sl] = (o / den).astype(o_ref.dtype)


def _cross_attn(q, kv, bsz, *, tq=1024, name):
    T, D = q.shape
    S = T // bsz
    mem_len = kv.shape[0] // bsz
    tq = _pick_tile(S, tq, SUBLANES)
    nq = S // tq
    kv3 = kv.reshape(bsz, mem_len, 2 * D)
    vmem = 4 * tq * D * 2 + 4 * mem_len * D * 2 + 4 * tq * mem_len * 4 + (8 << 20)
    return pl.pallas_call(
        functools.partial(_cross_attn_kernel, nh=XA_HEADS),
        grid=(bsz, nq),
        in_specs=[pl.BlockSpec((tq, D), lambda b, i: (b * nq + i, 0)),
                  pl.BlockSpec((None, mem_len, D), lambda b, i: (b, 0, 0)),
                  pl.BlockSpec((None, mem_len, D), lambda b, i: (b, 0, 1))],
        out_specs=pl.BlockSpec((tq, D), lambda b, i: (b * nq + i, 0)),
        out_shape=jax.ShapeDtypeStruct((T, D), BF16),
        compiler_params=_params(("parallel", "parallel"), vmem),
        name=name,
    )(q, kv3, kv3)


def _diff_attn_kernel(q_ref, k_ref, v_ref, lv_ref, sg_ref, o_ref, vt_ref, *, tk, unroll, lambda_init):
    tq = q_ref.shape[0]
    seq = k_ref.shape[0]
    dv = v_ref.shape[1]
    dh = dv // 2

    @pl.when(pl.program_id(2) == 0)
    def _():
        for c in range(seq // tk):
            vt_ref[:, c * tk:(c + 1) * tk] = v_ref[c * tk:(c + 1) * tk, :].astype(F32).T.astype(BF16)

    q = q_ref[...]
    nk = seq // tk

    def scores(kk):
        kc = k_ref[pl.ds(pl.multiple_of(kk * tk, tk), tk), :]
        return tuple(lax.dot_general(kc[:, c * dh:(c + 1) * dh], q[:, c * dh:(c + 1) * dh],
                                     (((1,), (1,)), ((), ())), preferred_element_type=F32)
                     for c in range(2))

    def body(kk, carry):
        st_next = scores(jnp.minimum(kk + 1, nk - 1))
        vt = vt_ref[:, pl.ds(pl.multiple_of(kk * tk, tk), tk)]
        new = []
        for c in range(2):
            st, m, l, acc = carry[c]
            m_new = jnp.maximum(m, jnp.max(st, axis=0, keepdims=True))
            alpha = jnp.exp2(m - m_new)
            pt = jnp.exp2(st - m_new)
            l = alpha * l + jnp.sum(pt, axis=0, keepdims=True)
            acc = alpha * acc + jnp.dot(vt, pt.astype(BF16), preferred_element_type=F32)
            new.append((st_next[c], m_new, l, acc))
        return tuple(new)

    st0 = scores(0)
    init = tuple((st0[c], jnp.full((1, tq), MASK_VALUE, F32), jnp.zeros((1, tq), F32),
                  jnp.zeros((dv, tq), F32)) for c in range(2))
    (_, _, l1, a1), (_, _, l2, a2) = lax.fori_loop(0, nk, body, init, unroll=unroll)
    lv = lv_ref[...]
    lam = (jnp.exp(jnp.sum(lv[0:1] * lv[1:2], axis=-1, keepdims=True))
           - jnp.exp(jnp.sum(lv[2:3] * lv[3:4], axis=-1, keepdims=True)) + lambda_init)
    ot = a1 / l1 - lam * (a2 / l2)
    ms = jnp.mean(ot * ot, axis=0, keepdims=True)
    ot = ot * lax.rsqrt(ms + NORM_EPS)
    o_ref[...] = (ot.T * sg_ref[...] * (1.0 - lambda_init)).astype(o_ref.dtype)


def _diff_attn(proj, lam_vec, subln_g, bsz, *, s5_width, n_heads, lambda_init, tq=512, tk=512, unroll=1, name):
    T, ncol = proj.shape
    S = T // bsz
    dv = 2 * HEAD_DIM
    width = n_heads * dv
    q0, k0, v0 = s5_width // dv, (s5_width + width) // dv, (s5_width + 2 * width) // dv
    tq = _pick_tile(S, tq, SUBLANES)
    tk = _pick_tile(S, tk, LANES)
    nq = S // tq
    proj3 = proj.reshape(bsz, S, ncol)
    vmem = 4 * tq * dv * 2 + 4 * S * dv * 2 + 6 * tq * tk * 4 + 8 * tq * dv * 4 + (8 << 20)
    return pl.pallas_call(
        functools.partial(_diff_attn_kernel, tk=tk, unroll=unroll, lambda_init=lambda_init),
        grid=(bsz, n_heads, nq),
        in_specs=[pl.BlockSpec((None, tq, dv), lambda b, h, i: (b, i, q0 + h)),
                  pl.BlockSpec((None, S, dv), lambda b, h, i: (b, 0, k0 + h)),
                  pl.BlockSpec((None, S, dv), lambda b, h, i: (b, 0, v0 + h)),
                  pl.BlockSpec((4, HEAD_DIM), lambda b, h, i: (0, 0)),
                  pl.BlockSpec((1, dv), lambda b, h, i: (0, 0))],
        out_specs=pl.BlockSpec((tq, dv), lambda b, h, i: (b * nq + i, h)),
        out_shape=jax.ShapeDtypeStruct((T, width), BF16),
        scratch_shapes=[pltpu.VMEM((dv, S), BF16)],
        compiler_params=_params(("parallel", "parallel", "arbitrary"), vmem),
        name=name,
    )(proj3, proj3, proj3, lam_vec.astype(F32), subln_g.reshape(1, dv).astype(F32))


def _s5_tables(lam_re, lam_im, log_step, b_re, b_im, c_re, c_im, d_skip, n_steps):
    L = S5_CHUNK
    _, G, P = lam_re.shape
    N = b_re.shape[-1]
    lr, li = lam_re.astype(F32), lam_im.astype(F32)
    step = jnp.exp(log_step.astype(F32))[..., None]
    ar, ai = lr * step, li * step

    def cexp(t, axes=()):
        a_r, a_i = jnp.expand_dims(ar, axes), jnp.expand_dims(ai, axes)
        mag = jnp.exp(t * a_r)
        return mag * jnp.cos(t * a_i), mag * jnp.sin(t * a_i)

    def cmul(xr, xi, yr, yi):
        return xr * yr - xi * yi, xr * yi + xi * yr

    er, ei = cexp(1.0)
    den = lr * lr + li * li
    qr, qi = ((er - 1.0) * lr + ei * li) / den, (ei * lr - (er - 1.0) * li) / den
    br, bi = cmul(qr[..., None], qi[..., None], b_re.astype(F32), b_im.astype(F32))
    cr, ci = c_re.astype(F32), c_im.astype(F32)
    tau = jnp.arange(L + 1, dtype=F32)[None, None, :, None]
    pr, pi = cexp(tau, axes=2)

    wr, wi = cmul(pr[:, :, :L, :, None], pi[:, :, :L, :, None], br[:, :, None], bi[:, :, None])
    kb_cat = jnp.transpose(jnp.concatenate([jnp.swapaxes(br, 2, 3), -jnp.swapaxes(bi, 2, 3)], axis=3),
                           (1, 0, 2, 3))
    pcr, pci = cmul(pr[:, :, :L, None, :], pi[:, :, :L, None, :], cr[:, :, None], ci[:, :, None])
    def _kc_rows(w):
        w = jnp.stack([w[0], w[1][:, ::-1]], axis=1)
        return jnp.transpose(w, (0, 1, 4, 2, 3)).reshape(G, 2, P, L * N)
    kc_cat = jnp.concatenate([_kc_rows(pcr), _kc_rows(pci)], axis=2)

    def _in_cols(w):
        return jnp.transpose(w, (0, 1, 3, 2)).reshape(G, L * N, P)
    w_in = jnp.concatenate([_in_cols(wr[0][:, ::-1]), _in_cols(wr[1]),
                            _in_cols(wi[0][:, ::-1]), _in_cols(wi[1])], axis=2)

    of_r, of_i = cmul(pr[0][:, 1:, None, :], pi[0][:, 1:, None, :], cr[0][:, None], ci[0][:, None])
    ob_r, ob_i = cmul(pr[1][:, ::-1][:, :L, None, :], pi[1][:, ::-1][:, :L, None, :],
                      cr[1][:, None], ci[1][:, None])
    def _out_rows(w):
        return jnp.transpose(w, (0, 3, 1, 2)).reshape(G, P, L * N)
    w_out = jnp.concatenate([_out_rows(of_r), _out_rows(ob_r), -_out_rows(of_i), -_out_rows(ob_i)], axis=1)

    def _lanes(z):
        return jnp.concatenate([z[0], z[1]], axis=-1)
    a_chunk, a_seg = cexp(float(L)), cexp(float(L * n_steps))
    decay = jnp.stack([_lanes(a_chunk[0]), _lanes(a_chunk[1]), _lanes(a_seg[0]), _lanes(a_seg[1])], axis=1)
    jj = float(L) * jnp.arange(n_steps, dtype=F32)
    jj = jnp.stack([jj, jj[::-1]], axis=0)[:, None, :, None]
    powers = jnp.stack([_lanes(p) for p in cexp(jj, axes=2)], axis=1)
    dvec = jnp.tile(d_skip.astype(F32).reshape(G, 1, N), (1, L, 1)).reshape(G, 1, L * N)
    return kb_cat, kc_cat, w_in.astype(BF16), w_out.astype(BF16), decay, powers, dvec


def _s5_kernel(u_ref, kb_ref, kc_ref, win_ref, wout_ref, decay_ref, pow_ref, dvec_ref, o_ref, toep_ref,
               *, n_steps, bsz, gb):
    rb = bsz * SUBLANES
    n_ch = kb_ref.shape[2]
    width = kc_ref.shape[3]
    chunk = width // n_ch
    lane = lax.broadcasted_iota(jnp.int32, (n_ch, width), 1)
    for gi in range(gb):
        kf = jnp.dot(kb_ref[gi, 0], kc_ref[gi, 0], preferred_element_type=F32, precision=lax.Precision.HIGHEST)
        kb = jnp.dot(kb_ref[gi, 1], kc_ref[gi, 1], preferred_element_type=F32, precision=lax.Precision.HIGHEST)
        for s in range(chunk):
            fwd = kf if s == 0 else jnp.where(lane >= s * n_ch, pltpu.roll(kf, s * n_ch, 1), 0.0)
            back = chunk - 1 - s
            bwd = kb if back == 0 else jnp.where(lane < (s + 1) * n_ch, pltpu.roll(kb, width - back * n_ch, 1), 0.0)
            toep_ref[gi, s * n_ch:(s + 1) * n_ch, :] = (fwd + bwd).astype(BF16)
        u = u_ref[gi]
        y = jnp.dot(u, toep_ref[gi], preferred_element_type=F32)
        z = jnp.dot(u, win_ref[gi], preferred_element_type=F32)
        half = z.shape[1] // 2
        zr, zi = z[:, :half], z[:, half:]
        dec = decay_ref[gi]
        ar, ai, sr, si = dec[0:1], dec[1:2], dec[2:3], dec[3:4]
        is_fwd = lax.broadcasted_iota(jnp.int32, (rb, half), 1) < half // 2

        xr = jnp.zeros((rb, half), F32)
        xi = jnp.zeros((rb, half), F32)
        ent_r, ent_i = [], []
        for j in range(n_steps):
            jb = n_steps - 1 - j
            ent_r.append(xr)
            ent_i.append(xi)
            zrj = jnp.where(is_fwd, zr[j * rb:(j + 1) * rb], zr[jb * rb:(jb + 1) * rb])
            zij = jnp.where(is_fwd, zi[j * rb:(j + 1) * rb], zi[jb * rb:(jb + 1) * rb])
            xr, xi = ar * xr - ai * xi + zrj, ar * xi + ai * xr + zij

        sub = lax.broadcasted_iota(jnp.int32, (SUBLANES, half), 0)
        fwd8 = lax.broadcasted_iota(jnp.int32, (SUBLANES, half), 1) < half // 2
        keep = (fwd8 & (sub > 0)) | (jnp.logical_not(fwd8) & (sub < SUBLANES - 1))
        car_r, car_i = [], []
        for b in range(bsz):
            er, ei = xr[b * SUBLANES:(b + 1) * SUBLANES], xi[b * SUBLANES:(b + 1) * SUBLANES]
            cr = jnp.zeros((SUBLANES, half), F32)
            ci = jnp.zeros((SUBLANES, half), F32)
            for _ in range(SUBLANES - 1):
                tr, ti = sr * cr - si * ci + er, sr * ci + si * cr + ei
                cr = jnp.where(keep, jnp.where(fwd8, pltpu.roll(tr, 1, 0), pltpu.roll(tr, SUBLANES - 1, 0)), 0.0)
                ci = jnp.where(keep, jnp.where(fwd8, pltpu.roll(ti, 1, 0), pltpu.roll(ti, SUBLANES - 1, 0)), 0.0)
            car_r.append(cr)
            car_i.append(ci)
        gr = car_r[0] if bsz == 1 else jnp.concatenate(car_r, axis=0)
        gi_ = car_i[0] if bsz == 1 else jnp.concatenate(car_i, axis=0)

        pr_all, pi_all = pow_ref[gi, 0], pow_ref[gi, 1]
        rows_r, rows_i = [], []
        for j in range(n_steps):
            jb = n_steps - 1 - j
            lr = jnp.where(is_fwd, ent_r[j], ent_r[jb])
            li = jnp.where(is_fwd, ent_i[j], ent_i[jb])
            pr, pi = pr_all[j:j + 1], pi_all[j:j + 1]
            rows_r.append(lr + pr * gr - pi * gi_)
            rows_i.append(li + pr * gi_ + pi * gr)
        state = jnp.concatenate([jnp.concatenate(rows_r, axis=0), jnp.concatenate(rows_i, axis=0)], axis=1)
        y = y + jnp.dot(state.astype(BF16), wout_ref[gi], preferred_element_type=F32)
        y = y + u.astype(F32) * dvec_ref[gi]
        o_ref[gi] = y.astype(o_ref.dtype)


def _s5_mix(u, tables, bsz, *, gb=4, name):
    kb_cat, kc_cat, w_in, w_out, decay, powers, dvec = tables
    T, width = u.shape
    S = T // bsz
    G = w_in.shape[0]
    N = width // G
    L = S5_CHUNK
    n_steps = S // (L * S5_SEGMENTS)
    rows = n_steps * bsz * S5_SEGMENTS
    gb = _pick_tile(G, gb, 1)
    ug = u.reshape(bsz, S5_SEGMENTS, n_steps, L, G, N)
    ug = jnp.transpose(ug, (4, 2, 0, 1, 3, 5)).reshape(G, rows, L * N)
    blk = lambda *shape: pl.BlockSpec((gb,) + shape, lambda g: (g,) + (0,) * len(shape))
    vmem = gb * (4 * rows * L * N * 2 + (L * N) ** 2 * 2 + 4 * kc_cat.shape[2] * L * N * 4
                 + 8 * L * N * w_in.shape[2] * 2) + (24 << 20)
    yg = pl.pallas_call(
        functools.partial(_s5_kernel, n_steps=n_steps, bsz=bsz, gb=gb),
        grid=(G // gb,),
        in_specs=[blk(rows, L * N), blk(*kb_cat.shape[1:]), blk(*kc_cat.shape[1:]),
                  blk(L * N, w_in.shape[2]), blk(w_out.shape[1], L * N),
                  blk(4, decay.shape[2]), blk(2, n_steps, powers.shape[3]), blk(1, L * N)],
        out_specs=blk(rows, L * N),
        out_shape=jax.ShapeDtypeStruct((G, rows, L * N), BF16),
        scratch_shapes=[pltpu.VMEM((gb, L * N, L * N), BF16)],
        compiler_params=_params(("parallel",), vmem),
        name=name,
    )(ug, kb_cat, kc_cat, w_in, w_out, decay, powers, dvec)
    yg = yg.reshape(G, n_steps, bsz, S5_SEGMENTS, L, N)
    return jnp.transpose(yg, (2, 3, 1, 4, 0, 5)).reshape(T, width)


def _s5_glu_kernel(y_ref, w_ref, b_ref, o_ref):
    g = jax.nn.gelu(y_ref[...].astype(F32), approximate=True)
    z = jnp.dot(g.astype(BF16), w_ref[...], preferred_element_type=F32) + b_ref[...]
    o_ref[...] = (g * jax.nn.sigmoid(z)).astype(o_ref.dtype)


def _s5_glu(y, w, b, *, tm=1024, name):
    M, W = y.shape
    w, layer = w
    tm = _pick_tile(M, tm, SUBLANES)
    vmem = 4 * tm * W * 2 + 2 * W * W * 2 + 4 * tm * W * 4 + (4 << 20)
    return pl.pallas_call(
        _s5_glu_kernel,
        grid=(M // tm,),
        in_specs=[pl.BlockSpec((tm, W), lambda i: (i, 0)),
                  pl.BlockSpec((None, W, W), lambda i: (layer, 0, 0)),
                  pl.BlockSpec((1, W), lambda i: (0, 0))],
        out_specs=pl.BlockSpec((tm, W), lambda i: (i, 0)),
        out_shape=jax.ShapeDtypeStruct((M, W), BF16),
        compiler_params=_params(("parallel",), vmem),
        name=name,
    )(y, w, b.reshape(1, W).astype(F32))


def _dilated_kernel(q_ref, k0, k1, k2, k3, v0, v1, v2, v3, o_ref, lse_ref, *, sub_len, half, nh):
    i = pl.program_id(1)
    tq = q_ref.shape[0]
    kb = k0.shape[0]
    dh = q_ref.shape[1] // nh
    qpos = i * tq + lax.broadcasted_iota(jnp.int32, (tq, 4 * kb), 0)
    kpos = (2 * i - 1) * kb + lax.broadcasted_iota(jnp.int32, (tq, 4 * kb), 1)
    valid = (jnp.abs(kpos - qpos) <= half) & (kpos >= 0) & (kpos < sub_len)
    head_lane = lax.broadcasted_iota(jnp.int32, (tq, nh), 1)
    lse_all = jnp.zeros((tq, nh), F32)
    for h in range(nh):
        sl = slice(h * dh, (h + 1) * dh)
        kc = jnp.concatenate([k0[:, sl], k1[:, sl], k2[:, sl], k3[:, sl]], axis=0)
        vc = jnp.concatenate([v0[:, sl], v1[:, sl], v2[:, sl], v3[:, sl]], axis=0)
        s = lax.dot_general(q_ref[:, sl], kc, (((1,), (1,)), ((), ())), preferred_element_type=F32)
        s = jnp.where(valid, s, MASK_VALUE)
        m = jnp.max(s, axis=-1, keepdims=True)
        e = jnp.exp(s - m)
        den = jnp.sum(e, axis=-1, keepdims=True)
        o = jnp.dot(e.astype(BF16), vc, preferred_element_type=F32)
        o_ref[:, sl] = (o / den).astype(o_ref.dtype)
        lse_all = jnp.where(head_lane == h, m + jnp.log(den), lse_all)
    lse_ref[...] = lse_all


def _dilated_branch(q, k, v, cols, D, n_seq, *, half, nh, name):
    T = q.shape[0]
    sub_len = T // n_seq
    tq, kb = 2 * half, half
    nq, nkb = sub_len // tq, sub_len // kb
    cq, ck, cv = cols

    def kv_spec(off, c):
        return pl.BlockSpec((kb, D), lambda s, i: (s * nkb + jnp.clip(2 * i - 1 + off, 0, nkb - 1), c))

    vmem = 2 * (2 * tq * D * 2 + 8 * kb * D * 2) + (16 << 20)
    return pl.pallas_call(
        functools.partial(_dilated_kernel, sub_len=sub_len, half=half, nh=nh),
        grid=(n_seq, nq),
        in_specs=[pl.BlockSpec((tq, D), lambda s, i: (s * nq + i, cq))]
                 + [kv_spec(off, ck) for off in range(4)] + [kv_spec(off, cv) for off in range(4)],
        out_specs=[pl.BlockSpec((tq, D), lambda s, i: (s * nq + i, 0)),
                   pl.BlockSpec((tq, nh), lambda s, i: (s * nq + i, 0))],
        out_shape=[jax.ShapeDtypeStruct((T, D), BF16), jax.ShapeDtypeStruct((T, nh), F32)],
        compiler_params=_params(("parallel", "parallel"), vmem),
        name=name,
    )(q, k, k, k, k, v, v, v, v)


def _residue_perm(dil):
    per = PERM_GROUP // dil
    rows = jnp.arange(PERM_GROUP)
    src = (rows % per) * dil + rows // per
    return (src[:, None] == rows[None, :]).astype(BF16)


def _to_residue_kernel(*refs, dils, n_in):
    n_d = len(dils)
    p_refs, x_refs, o_refs = refs[:n_d], refs[n_d:n_d + n_in], refs[n_d + n_in:]
    for xi, x_ref in enumerate(x_refs):
        for g in range(x_ref.shape[0] // PERM_GROUP):
            xg = x_ref[g * PERM_GROUP:(g + 1) * PERM_GROUP, :]
            for di, dil in enumerate(dils):
                o_ref = o_refs[di * n_in + xi]
                per = PERM_GROUP // dil
                y = jnp.dot(p_refs[di][...], xg, preferred_element_type=F32).astype(o_ref.dtype)
                for r in range(dil):
                    o_ref[r, g * per:(g + 1) * per, :] = y[r * per:(r + 1) * per]


def _to_residue(x, cols, width, dils, bsz, *, tb=512, name):
    T = x.shape[0]
    S = T // bsz
    tb = _pick_tile(S, tb, PERM_GROUP)
    nb = S // tb
    n_in = len(cols)
    perm_spec = pl.BlockSpec((PERM_GROUP, PERM_GROUP), lambda b, i: (0, 0))
    outs = pl.pallas_call(
        functools.partial(_to_residue_kernel, dils=tuple(dils), n_in=n_in),
        grid=(bsz, nb),
        in_specs=[perm_spec] * len(dils)
                 + [pl.BlockSpec((tb, width), lambda b, i, c=c: (b * nb + i, c)) for c in cols],
        out_specs=[pl.BlockSpec((None, dil, tb // dil, width), lambda b, i: (b, 0, i, 0))
                   for dil in dils for _ in cols],
        out_shape=[jax.ShapeDtypeStruct((bsz, dil, S // dil, width), x.dtype) for dil in dils for _ in cols],
        compiler_params=_params(("parallel", "parallel"),
                                (1 + len(dils)) * n_in * 2 * tb * width * 2 + (12 << 20)),
        name=name,
    )(*[_residue_perm(dil) for dil in dils], *([x] * n_in))
    outs = [o.reshape(T, width) for o in outs]
    return [outs[di * n_in:(di + 1) * n_in] for di in range(len(dils))]


def _from_residue_kernel(p_ref, x_ref, o_ref, *, dil):
    per = PERM_GROUP // dil
    for g in range(o_ref.shape[0] // PERM_GROUP):
        xg = jnp.concatenate([x_ref[r, g * per:(g + 1) * per, :] for r in range(dil)], axis=0)
        o_ref[g * PERM_GROUP:(g + 1) * PERM_GROUP, :] = jnp.dot(
            p_ref[...], xg, preferred_element_type=F32).astype(o_ref.dtype)


def _from_residue(x, dil, bsz, *, tb=1024, name):
    T, width = x.shape
    S = T // bsz
    tb = _pick_tile(S, tb, PERM_GROUP)
    nb = S // tb
    return pl.pallas_call(
        functools.partial(_from_residue_kernel, dil=dil),
        grid=(bsz, nb),
        in_specs=[pl.BlockSpec((PERM_GROUP, PERM_GROUP), lambda b, i: (0, 0)),
                  pl.BlockSpec((None, dil, tb // dil, width), lambda b, i: (b, 0, i, 0))],
        out_specs=pl.BlockSpec((tb, width), lambda b, i: (b * nb + i, 0)),
        out_shape=jax.ShapeDtypeStruct((T, width), x.dtype),
        compiler_params=_params(("parallel", "parallel"), 4 * tb * width * 2 + (8 << 20)),
        name=name,
    )(_residue_perm(dil).T, x.reshape(bsz, dil, S // dil, width))


def _merge_proj_kernel(o1, o2, o3, l1, l2, l3, w_ref, res_ref, out_ref, a_ref, *, nh):
    @pl.when(pl.program_id(1) == 0)
    def _():
        lse = [l1[...], l2[...], l3[...]]
        mx = jnp.maximum(jnp.maximum(lse[0], lse[1]), lse[2])
        e = [jnp.exp(x - mx) for x in lse]
        inv = 1.0 / (e[0] + e[1] + e[2])
        w1, w2 = e[0] * inv, e[1] * inv
        dh = a_ref.shape[1] // nh
        for h in range(nh):
            sl = slice(h * dh, (h + 1) * dh)
            b1, b2, b3 = o1[:, sl].astype(F32), o2[:, sl].astype(F32), o3[:, sl].astype(F32)
            a_ref[:, sl] = (b3 + w1[:, h:h + 1] * (b1 - b3) + w2[:, h:h + 1] * (b2 - b3)).astype(BF16)

    out_ref[...] = res_ref[...] + jnp.dot(a_ref[...], w_ref[...], preferred_element_type=F32)


def _merge_proj(outs, lses, w, res, *, nh, tm=512, tn=1024, name):
    M, D = outs[0].shape
    w, layer = w
    N = w.shape[2]
    tm = _pick_tile(M, tm, SUBLANES)
    tn = _pick_tile(N, tn, LANES)
    row = lambda width: pl.BlockSpec((tm, width), lambda i, j: (i, 0))
    vmem = 3 * 2 * tm * D * 2 + tm * D * 2 + 2 * D * tn * 2 + 5 * tm * tn * 4 + 4 * tm * D * 4 + (4 << 20)
    return pl.pallas_call(
        functools.partial(_merge_proj_kernel, nh=nh),
        grid=(M // tm, N // tn),
        in_specs=[row(D)] * 3 + [row(nh)] * 3
                 + [pl.BlockSpec((None, D, tn), lambda i, j: (layer, 0, j)),
                    pl.BlockSpec((tm, tn), lambda i, j: (i, j))],
        out_specs=pl.BlockSpec((tm, tn), lambda i, j: (i, j)),
        out_shape=jax.ShapeDtypeStruct((M, N), F32),
        scratch_shapes=[pltpu.VMEM((tm, D), BF16)],
        compiler_params=_params(("parallel", "arbitrary"), vmem),
        name=name,
    )(*outs, *lses, w, res)


def _final_norm_kernel(x_ref, g_ref, o_ref):
    o_ref[...] = _rms_normalize(x_ref[...], g_ref[...])


def _final_norm(x, g, *, tm=1024, name):
    M, D = x.shape
    tm = _pick_tile(M, tm, SUBLANES)
    return pl.pallas_call(
        _final_norm_kernel,
        grid=(M // tm,),
        in_specs=[pl.BlockSpec((tm, D), lambda i: (i, 0)), pl.BlockSpec((1, D), lambda i: (0, 0))],
        out_specs=pl.BlockSpec((tm, D), lambda i: (i, 0)),
        out_shape=jax.ShapeDtypeStruct((M, D), F32),
        compiler_params=_params(("parallel",), 6 * tm * D * 4 + (4 << 20)),
        name=name,
    )(x, g.reshape(1, D).astype(F32))


def _even_mixer(h, g, w_in, w_out, s5_tabs, glu_w, glu_b, lam_vec, subln_g, lambda_init, rope_tabs, bsz, tag):
    T, _ = h.shape
    S = T // bsz
    s5_width = glu_w[0].shape[1]
    width = (w_in[0].shape[2] - s5_width) // 3
    n_heads = width // (2 * HEAD_DIM)
    segs = ((s5_width, s5_width + width, True, HEAD_DIM ** -0.5 * LOG2_E),
            (s5_width + width, s5_width + 2 * width, True, 1.0))
    proj = _norm_matmul(h, g, w_in, segs=segs, rope_tabs=rope_tabs, seq=S, name=f"even_in_{tag}")
    y_s5 = _s5_mix(proj[:, :s5_width], s5_tabs, bsz, name=f"s5_{tag}")
    y_s5 = _s5_glu(y_s5, glu_w, glu_b, name=f"s5_glu_{tag}")
    y_diff = _diff_attn(proj, lam_vec, subln_g, bsz, s5_width=s5_width, n_heads=n_heads,
                        lambda_init=lambda_init, unroll=8, name=f"diff_attn_{tag}")
    return _matmul_res([y_s5, y_diff], w_out, h, tn=1024, name=f"even_out_{tag}")


def _odd_mixer(h, g, w_qkv, w_out, rope_tabs, bsz, tag):
    T, D = h.shape
    S = T // bsz
    nh = D // HEAD_DIM
    segs = ((0, D, True, HEAD_DIM ** -0.5), (D, 2 * D, True, 1.0))
    qkv = _norm_matmul(h, g, w_qkv, segs=segs, rope_tabs=rope_tabs, seq=S, name=f"odd_in_{tag}")
    outs, lses = [], []
    dils = [dil for _, dil in DILATED_BRANCHES if dil > 1]
    residue = dict(zip(dils, _to_residue(qkv, (0, 1, 2), D, dils, bsz, name=f"to_residue_{tag}")))
    for window, dil in DILATED_BRANCHES:
        half = window // (2 * dil)
        sub_len = S // dil

        if dil == 1:
            o, lse = _dilated_branch(qkv, qkv, qkv, (0, 1, 2), D, bsz, half=half, nh=nh,
                                     name=f"dilated_{dil}_{tag}")
        else:
            qr, kr, vr = residue[dil]
            o, lse = _dilated_branch(qr, kr, vr, (0, 0, 0), D, bsz * dil, half=half, nh=nh,
                                     name=f"dilated_{dil}_{tag}")
            o = _from_residue(o, dil, bsz, name=f"from_residue_{dil}_{tag}")
            lse = jnp.transpose(lse.reshape(bsz, dil, sub_len, nh), (0, 2, 1, 3)).reshape(T, nh)
        outs.append(o)
        lses.append(lse)
    return _merge_proj(outs, lses, w_out, h, nh=nh, name=f"odd_out_{tag}")


def kernel(x, mem, norm_mix_g, norm_xa_g, norm_mem_g, xa_wq, xa_wkv, xa_wo, norm_ffn_g, ffn_w13, ffn_w2, ab_w_in, ab_w_out, s5_lambda_re, s5_lambda_im, s5_log_step, s5_b_re, s5_b_im, s5_c_re, s5_c_im, s5_d, s5_glu_w, s5_glu_b, diff_lambda, diff_subln_g, c_w_qkv, c_w_out, final_norm_g):
    bsz, S, D = x.shape
    T = bsz * S
    depth = norm_mix_g.shape[0]
    mem_len = mem.shape[1]
    rope_tabs = _rope_tables(S)
    n_steps = S // (S5_CHUNK * S5_SEGMENTS)
    s5_width = s5_glu_w.shape[1]
    diff_width = (ab_w_in.shape[2] - s5_width) // 3
    ab_w_in = _rope_columns(ab_w_in, s5_width, s5_width + 2 * diff_width)
    c_w_qkv = _rope_columns(c_w_qkv, 0, 2 * D)
    ab_w_in, ab_w_out, s5_glu_w, c_w_qkv, c_w_out, xa_wq, xa_wkv, xa_wo, ffn_w13, ffn_w2 = (
        w.astype(BF16) for w in (ab_w_in, ab_w_out, s5_glu_w, c_w_qkv, c_w_out, xa_wq, xa_wkv, xa_wo,
                                 ffn_w13, ffn_w2))

    h = x.reshape(T, D)
    mem2 = mem.reshape(bsz * mem_len, D)
    for layer in range(depth):
        i = layer // 2
        if layer % 2 == 0:
            lambda_init = 0.8 - 0.6 * math.exp(-0.3 * layer)
            tabs = _s5_tables(s5_lambda_re[i], s5_lambda_im[i], s5_log_step[i], s5_b_re[i], s5_b_im[i],
                              s5_c_re[i], s5_c_im[i], s5_d[i], n_steps)
            h = _even_mixer(h, norm_mix_g[layer], (ab_w_in, i), (ab_w_out, i), tabs, (s5_glu_w, i),
                            s5_glu_b[i], diff_lambda[i], diff_subln_g[i], lambda_init, rope_tabs, bsz, layer)
        else:
            h = _odd_mixer(h, norm_mix_g[layer], (c_w_qkv, i), (c_w_out, i), rope_tabs, bsz, layer)

        q = _norm_matmul(h, norm_xa_g[layer], (xa_wq, layer),
                         segs=((0, D, False, (D // XA_HEADS) ** -0.5),), name=f"xa_q_{layer}")
        kv = _norm_matmul(mem2, norm_mem_g[layer], (xa_wkv, layer), name=f"xa_kv_{layer}")
        o = _cross_attn(q, kv, bsz, name=f"xa_core_{layer}")
        h = _matmul_res([o], (xa_wo, layer), h, tn=1024, name=f"xa_out_{layer}")

        hid = _swiglu_up(h, norm_ffn_g[layer], (ffn_w13, layer), name=f"ffn_up_{layer}")
        h = _matmul_res([hid], (ffn_w2, layer), h, tn=512, name=f"ffn_down_{layer}")
    return _final_norm(h, final_norm_g, name="final_norm").reshape(bsz, S, D)
```

```python
import functools
import math

import jax
import jax.numpy as jnp
from jax import lax
from jax.experimental import pallas as pl
from jax.experimental.pallas import tpu as pltpu

F32 = jnp.float32
BF16 = jnp.bfloat16

NORM_EPS = 1e-6
MASK_VALUE = -1e30
LOG2_E = math.log2(math.e)
LN_2 = math.log(2.0)
ROPE_THETA = 500000.0
ROPE_HALF = 16
HEAD_DIM = 128
XA_HEADS = 4
DILATED_BRANCHES = ((128, 1), (512, 4), (2048, 16))
S5_CHUNK = 32
S5_SEGMENTS = 8
PERM_GROUP = 256
LANES = 128
SUBLANES = 8
VMEM_CAP_BYTES = 56 * 1024 * 1024


def _pick_tile(n, pref, quantum):
    t = (min(n, pref) // quantum) * quantum
    while t >= quantum:
        if n % t == 0:
            return t
        t -= quantum
    return n


def _params(semantics, vmem_bytes):
    return pltpu.CompilerParams(
        dimension_semantics=semantics,
        vmem_limit_bytes=int(min(VMEM_CAP_BYTES, max(32 * 1024 * 1024, vmem_bytes))))


def _rms_normalize(x, g):
    ms = jnp.mean(x * x, axis=-1, keepdims=True)
    return (x * lax.rsqrt(ms + NORM_EPS)) * g


def _rope_lanes(y, cf, sf):
    outs = []
    for c in range(y.shape[1] // LANES):
        yc = y[:, c * LANES:(c + 1) * LANES]
        outs.append(yc * cf + pltpu.roll(yc, LANES // 2, 1) * sf)
    return outs[0] if len(outs) == 1 else jnp.concatenate(outs, axis=1)


def _rope_columns(w, lo, hi):
    mid = w[..., lo:hi]
    comp = mid.reshape(mid.shape[:-1] + ((hi - lo) // LANES, LANES))
    r, h = ROPE_HALF, LANES // 2
    comp = jnp.concatenate([comp[..., :r], comp[..., 2 * r:h + r], comp[..., r:2 * r], comp[..., h + r:]], axis=-1)
    return jnp.concatenate([w[..., :lo], comp.reshape(mid.shape), w[..., hi:]], axis=-1)


def _rope_tables(seq):
    pos = jnp.arange(seq, dtype=F32)
    inv = ROPE_THETA ** (-jnp.arange(0, 2 * ROPE_HALF, 2, dtype=F32) / (2 * ROPE_HALF))
    ang = pos[:, None] * inv[None, :]
    cos, sin = jnp.cos(ang), jnp.sin(ang)
    gap = LANES // 2 - ROPE_HALF
    cf = jnp.concatenate([cos, jnp.ones((seq, gap), F32), cos, jnp.ones((seq, gap), F32)], axis=1)
    sf = jnp.concatenate([-sin, jnp.zeros((seq, gap), F32), sin, jnp.zeros((seq, gap), F32)], axis=1)
    return cf, sf


def _norm_matmul_kernel(*refs, segs, tn, use_rope):
    if use_rope:
        x_ref, g_ref, w_ref, cf_ref, sf_ref, o_ref, xn_ref = refs
    else:
        x_ref, g_ref, w_ref, o_ref, xn_ref = refs
    j = pl.program_id(1)

    @pl.when(j == 0)
    def _():
        xn_ref[...] = _rms_normalize(x_ref[...], g_ref[...]).astype(BF16)

    y = jnp.dot(xn_ref[...], w_ref[...], preferred_element_type=F32)
    for lo, hi, rope, scale in segs:
        @pl.when(jnp.logical_and(j >= lo // tn, j < hi // tn))
        def _(rope=rope, scale=scale):
            z = y
            if rope:
                z = _rope_lanes(z, cf_ref[...], sf_ref[...])
            if scale != 1.0:
                z = z * scale
            o_ref[...] = z.astype(o_ref.dtype)


def _norm_matmul(x, g, w, *, segs=(), rope_tabs=None, seq=None, tm=1024, tn=1024, name):
    M, D = x.shape
    w, layer = w
    N = w.shape[2]
    tm = _pick_tile(M if seq is None else seq, tm, SUBLANES)
    tn = _pick_tile(math.gcd(N, *[b for s in segs for b in s[:2]]), tn, LANES)
    full, pos = [], 0
    for lo, hi, rope, scale in sorted(segs):
        if lo > pos:
            full.append((pos, lo, False, 1.0))
        full.append((lo, hi, rope, scale))
        pos = hi
    if pos < N:
        full.append((pos, N, False, 1.0))
    for lo, hi, _, _ in full:
        assert lo % tn == 0 and hi % tn == 0, (lo, hi, tn)
    use_rope = any(s[2] for s in full)
    in_specs = [pl.BlockSpec((tm, D), lambda i, j: (i, 0)),
                pl.BlockSpec((1, D), lambda i, j: (0, 0)),
                pl.BlockSpec((None, D, tn), lambda i, j: (layer, 0, j))]
    args = [x, g.reshape(1, D).astype(F32), w]
    if use_rope:
        nseq = seq // tm
        in_specs += [pl.BlockSpec((tm, LANES), lambda i, j: (i % nseq, 0))] * 2
        args += list(rope_tabs)
    vmem = 2 * tm * D * 4 + tm * D * 2 + 2 * D * tn * 2 + 2 * tm * tn * 2 + 3 * tm * tn * 4 + (4 << 20)
    return pl.pallas_call(
        functools.partial(_norm_matmul_kernel, segs=tuple(full), tn=tn, use_rope=use_rope),
        grid=(M // tm, N // tn),
        in_specs=in_specs,
        out_specs=pl.BlockSpec((tm, tn), lambda i, j: (i, j)),
        out_shape=jax.ShapeDtypeStruct((M, N), BF16),
        scratch_shapes=[pltpu.VMEM((tm, D), BF16)],
        compiler_params=_params(("parallel", "arbitrary"), vmem),
        name=name,
    )(*args)


def _matmul_res_kernel(*refs, n_a):
    a_refs, w_refs = refs[:n_a], refs[n_a:2 * n_a]
    res_ref, o_ref = refs[2 * n_a], refs[2 * n_a + 1]
    acc = res_ref[...]
    for a_ref, w_ref in zip(a_refs, w_refs):
        acc = acc + jnp.dot(a_ref[...], w_ref[...], preferred_element_type=F32)
    o_ref[...] = acc


def _matmul_res(a_list, w, res, *, tm=1024, tn=512, name):
    n_a = len(a_list)
    M, K = a_list[0].shape
    w, layer = w
    N = w.shape[2]
    tm = _pick_tile(M, tm, SUBLANES)
    tn = _pick_tile(N, tn, LANES)
    in_specs = [pl.BlockSpec((tm, K), lambda i, j: (i, 0)) for _ in range(n_a)]
    in_specs += [pl.BlockSpec((None, K, tn), lambda i, j, r=r: (layer, r, j)) for r in range(n_a)]
    in_specs += [pl.BlockSpec((tm, tn), lambda i, j: (i, j))]
    vmem = n_a * (2 * tm * K * 2 + 2 * K * tn * 2) + 5 * tm * tn * 4 + (4 << 20)
    return pl.pallas_call(
        functools.partial(_matmul_res_kernel, n_a=n_a),
        grid=(M // tm, N // tn),
        in_specs=in_specs,
        out_specs=pl.BlockSpec((tm, tn), lambda i, j: (i, j)),
        out_shape=jax.ShapeDtypeStruct((M, N), F32),
        compiler_params=_params(("parallel", "parallel"), vmem),
        name=name,
    )(*a_list, *([w] * n_a), res)


def _swiglu_up_kernel(x_ref, g_ref, w1_ref, w3_ref, o_ref, xn_ref):
    @pl.when(pl.program_id(1) == 0)
    def _():
        xn_ref[...] = _rms_normalize(x_ref[...], g_ref[...]).astype(BF16)

    xn = xn_ref[...]
    a = jnp.dot(xn, w1_ref[...], preferred_element_type=F32)
    b = jnp.dot(xn, w3_ref[...], preferred_element_type=F32)
    o_ref[...] = (a * jax.nn.sigmoid(a) * b).astype(o_ref.dtype)


def _swiglu_up(x, g, w13, *, tm=1024, tn=512, name):
    M, D = x.shape
    w13, layer = w13
    H = w13.shape[2] // 2
    tm = _pick_tile(M, tm, SUBLANES)
    tn = _pick_tile(H, tn, LANES)
    nj = H // tn
    vmem = 2 * tm * D * 4 + tm * D * 2 + 4 * D * tn * 2 + 2 * tm * tn * 2 + 4 * tm * tn * 4 + (4 << 20)
    return pl.pallas_call(
        _swiglu_up_kernel,
        grid=(M // tm, nj),
        in_specs=[pl.BlockSpec((tm, D), lambda i, j: (i, 0)),
                  pl.BlockSpec((1, D), lambda i, j: (0, 0)),
                  pl.BlockSpec((None, D, tn), lambda i, j: (layer, 0, j)),
                  pl.BlockSpec((None, D, tn), lambda i, j: (layer, 0, j + nj))],
        out_specs=pl.BlockSpec((tm, tn), lambda i, j: (i, j)),
        out_shape=jax.ShapeDtypeStruct((M, H), BF16),
        scratch_shapes=[pltpu.VMEM((tm, D), BF16)],
        compiler_params=_params(("parallel", "arbitrary"), vmem),
        name=name,
    )(x, g.reshape(1, D).astype(F32), w13, w13)


def _cross_attn_kernel(q_ref, k_ref, v_ref, o_ref, *, nh):
    dh = q_ref.shape[1] // nh
    for h in range(nh):
        sl = slice(h * dh, (h + 1) * dh)
        s = lax.dot_general(q_ref[:, sl], k_ref[:, sl], (((1,), (1,)), ((), ())), preferred_element_type=F32)
        m = jnp.max(s, axis=-1, keepdims=True)
        e = jnp.exp(s - m)
        den = jnp.sum(e, axis=-1, keepdims=True)
        o = jnp.dot(e.astype(BF16), v_ref[:, sl], preferred_element_type=F32)
        o_ref[:, sl] = (o / den).astype(o_ref.dtype)


def _cross_attn(q, kv, bsz, *, tq=1024, name):
    T, D = q.shape
    S = T // bsz
    mem_len = kv.shape[0] // bsz
    tq = _pick_tile(S, tq, SUBLANES)
    nq = S // tq
    kv3 = kv.reshape(bsz, mem_len, 2 * D)
    vmem = 4 * tq * D * 2 + 4 * mem_len * D * 2 + 4 * tq * mem_len * 4 + (8 << 20)
    return pl.pallas_call(
        functools.partial(_cross_attn_kernel, nh=XA_HEADS),
        grid=(bsz, nq),
        in_specs=[pl.BlockSpec((tq, D), lambda b, i: (b * nq + i, 0)),
                  pl.BlockSpec((None, mem_len, D), lambda b, i: (b, 0, 0)),
                  pl.BlockSpec((None, mem_len, D), lambda b, i: (b, 0, 1))],
        out_specs=pl.BlockSpec((tq, D), lambda b, i: (b * nq + i, 0)),
        out_shape=jax.ShapeDtypeStruct((T, D), BF16),
        compiler_params=_params(("parallel", "parallel"), vmem),
        name=name,
    )(q, kv3, kv3)


def _diff_attn_kernel(q_ref, k_ref, v_ref, lv_ref, sg_ref, o_ref, vt_ref, *, tk, unroll, lambda_init):
    tq = q_ref.shape[0]
    seq = k_ref.shape[0]
    dv = v_ref.shape[1]
    dh = dv // 2

    @pl.when(pl.program_id(2) == 0)
    def _():
        for c in range(seq // tk):
            vt_ref[:, c * tk:(c + 1) * tk] = v_ref[c * tk:(c + 1) * tk, :].astype(F32).T.astype(BF16)

    q = q_ref[...]
    nk = seq // tk

    def scores(kk):
        kc = k_ref[pl.ds(pl.multiple_of(kk * tk, tk), tk), :]
        return tuple(lax.dot_general(kc[:, c * dh:(c + 1) * dh], q[:, c * dh:(c + 1) * dh],
                                     (((1,), (1,)), ((), ())), preferred_element_type=F32)
                     for c in range(2))

    def body(kk, carry):
        st_next = scores(jnp.minimum(kk + 1, nk - 1))
        vt = vt_ref[:, pl.ds(pl.multiple_of(kk * tk, tk), tk)]
        new = []
        for c in range(2):
            st, m, l, acc = carry[c]
            m_new = jnp.maximum(m, jnp.max(st, axis=0, keepdims=True))
            alpha = jnp.exp2(m - m_new)
            pt = jnp.exp2(st - m_new)
            l = alpha * l + jnp.sum(pt, axis=0, keepdims=True)
            acc = alpha * acc + jnp.dot(vt, pt.astype(BF16), preferred_element_type=F32)
            new.append((st_next[c], m_new, l, acc))
        return tuple(new)

    st0 = scores(0)
    init = tuple((st0[c], jnp.full((1, tq), MASK_VALUE, F32), jnp.zeros((1, tq), F32),
                  jnp.zeros((dv, tq), F32)) for c in range(2))
    (_, _, l1, a1), (_, _, l2, a2) = lax.fori_loop(0, nk, body, init, unroll=unroll)
    lv = lv_ref[...]
    lam = (jnp.exp(jnp.sum(lv[0:1] * lv[1:2], axis=-1, keepdims=True))
           - jnp.exp(jnp.sum(lv[2:3] * lv[3:4], axis=-1, keepdims=True)) + lambda_init)
    ot = a1 / l1 - lam * (a2 / l2)
    ms = jnp.mean(ot * ot, axis=0, keepdims=True)
    ot = ot * lax.rsqrt(ms + NORM_EPS)
    o_ref[...] = (ot.T * sg_ref[...] * (1.0 - lambda_init)).astype(o_ref.dtype)


def _diff_attn(proj, lam_vec, subln_g, bsz, *, s5_width, n_heads, lambda_init, tq=512, tk=512, unroll=1, name):
    T, ncol = proj.shape
    S = T // bsz
    dv = 2 * HEAD_DIM
    width = n_heads * dv
    q0, k0, v0 = s5_width // dv, (s5_width + width) // dv, (s5_width + 2 * width) // dv
    tq = _pick_tile(S, tq, SUBLANES)
    tk = _pick_tile(S, tk, LANES)
    nq = S // tq
    proj3 = proj.reshape(bsz, S, ncol)
    vmem = 4 * tq * dv * 2 + 5 * S * dv * 2 + unroll * (3 * tq * tk * 4 + 2 * tq * dv * 4) + (8 << 20)
    return pl.pallas_call(
        functools.partial(_diff_attn_kernel, tk=tk, unroll=unroll, lambda_init=lambda_init),
        grid=(bsz, n_heads, nq),
        in_specs=[pl.BlockSpec((None, tq, dv), lambda b, h, i: (b, i, q0 + h)),
                  pl.BlockSpec((None, S, dv), lambda b, h, i: (b, 0, k0 + h)),
                  pl.BlockSpec((None, S, dv), lambda b, h, i: (b, 0, v0 + h)),
                  pl.BlockSpec((4, HEAD_DIM), lambda b, h, i: (0, 0)),
                  pl.BlockSpec((1, dv), lambda b, h, i: (0, 0))],
        out_specs=pl.BlockSpec((tq, dv), lambda b, h, i: (b * nq + i, h)),
        out_shape=jax.ShapeDtypeStruct((T, width), BF16),
        scratch_shapes=[pltpu.VMEM((dv, S), BF16)],
        compiler_params=_params(("parallel", "parallel", "arbitrary"), vmem),
        name=name,
    )(proj3, proj3, proj3, lam_vec.astype(F32), subln_g.reshape(1, dv).astype(F32))


def _s5_tables(lam_re, lam_im, log_step, b_re, b_im, c_re, c_im, d_skip, n_steps):
    L = S5_CHUNK
    _, G, P = lam_re.shape
    N = b_re.shape[-1]
    lr, li = lam_re.astype(F32), lam_im.astype(F32)
    step = jnp.exp(log_step.astype(F32))[..., None]
    ar, ai = lr * step, li * step

    def cexp(t, axes=()):
        a_r, a_i = jnp.expand_dims(ar, axes), jnp.expand_dims(ai, axes)
        mag = jnp.exp(t * a_r)
        return mag * jnp.cos(t * a_i), mag * jnp.sin(t * a_i)

    def cmul(xr, xi, yr, yi):
        return xr * yr - xi * yi, xr * yi + xi * yr

    er, ei = cexp(1.0)
    den = lr * lr + li * li
    qr, qi = ((er - 1.0) * lr + ei * li) / den, (ei * lr - (er - 1.0) * li) / den
    br, bi = cmul(qr[..., None], qi[..., None], b_re.astype(F32), b_im.astype(F32))
    cr, ci = c_re.astype(F32), c_im.astype(F32)
    tau = jnp.arange(L + 1, dtype=F32)[None, None, :, None]
    pr, pi = cexp(tau, axes=2)

    wr, wi = cmul(pr[:, :, :L, :, None], pi[:, :, :L, :, None], br[:, :, None], bi[:, :, None])
    kb_cat = jnp.transpose(jnp.concatenate([jnp.swapaxes(br, 2, 3), -jnp.swapaxes(bi, 2, 3)], axis=3),
                           (1, 0, 2, 3))
    pcr, pci = cmul(pr[:, :, :L, None, :], pi[:, :, :L, None, :], cr[:, :, None], ci[:, :, None])
    def _kc_rows(w):
        w = jnp.stack([w[0], w[1][:, ::-1]], axis=1)
        return jnp.transpose(w, (0, 1, 4, 2, 3)).reshape(G, 2, P, L * N)
    kc_cat = jnp.concatenate([_kc_rows(pcr), _kc_rows(pci)], axis=2)

    def _in_cols(w):
        return jnp.transpose(w, (0, 1, 3, 2)).reshape(G, L * N, P)
    w_in = jnp.concatenate([_in_cols(wr[0][:, ::-1]), _in_cols(wr[1]),
                            _in_cols(wi[0][:, ::-1]), _in_cols(wi[1])], axis=2)

    of_r, of_i = cmul(pr[0][:, 1:, None, :], pi[0][:, 1:, None, :], cr[0][:, None], ci[0][:, None])
    ob_r, ob_i = cmul(pr[1][:, ::-1][:, :L, None, :], pi[1][:, ::-1][:, :L, None, :],
                      cr[1][:, None], ci[1][:, None])
    def _out_rows(w):
        return jnp.transpose(w, (0, 3, 1, 2)).reshape(G, P, L * N)
    w_out = jnp.concatenate([_out_rows(of_r), _out_rows(ob_r), -_out_rows(of_i), -_out_rows(ob_i)], axis=1)

    def _lanes(z):
        return jnp.concatenate([z[0], z[1]], axis=-1)
    a_chunk, a_seg = cexp(float(L)), cexp(float(L * n_steps))
    decay = jnp.stack([_lanes(a_chunk[0]), _lanes(a_chunk[1]), _lanes(a_seg[0]), _lanes(a_seg[1])], axis=1)
    jj = float(L) * jnp.arange(n_steps, dtype=F32)
    jj = jnp.stack([jj, jj[::-1]], axis=0)[:, None, :, None]
    powers = jnp.stack([_lanes(p) for p in cexp(jj, axes=2)], axis=1)
    dvec = jnp.tile(d_skip.astype(F32).reshape(G, 1, N), (1, L, 1)).reshape(G, 1, L * N)
    return kb_cat, kc_cat, w_in.astype(BF16), w_out.astype(BF16), decay, powers, dvec


def _s5_kernel(u_ref, kb_ref, kc_ref, win_ref, wout_ref, decay_ref, pow_ref, dvec_ref, o_ref, toep_ref,
               *, n_steps, bsz, gb):
    rb = bsz * SUBLANES
    n_ch = kb_ref.shape[2]
    width = kc_ref.shape[3]
    chunk = width // n_ch
    lane = lax.broadcasted_iota(jnp.int32, (n_ch, width), 1)
    for gi in range(gb):
        kf = jnp.dot(kb_ref[gi, 0], kc_ref[gi, 0], preferred_element_type=F32, precision=lax.Precision.HIGHEST)
        kb = jnp.dot(kb_ref[gi, 1], kc_ref[gi, 1], preferred_element_type=F32, precision=lax.Precision.HIGHEST)
        for s in range(chunk):
            fwd = kf if s == 0 else jnp.where(lane >= s * n_ch, pltpu.roll(kf, s * n_ch, 1), 0.0)
            back = chunk - 1 - s
            bwd = kb if back == 0 else jnp.where(lane < (s + 1) * n_ch, pltpu.roll(kb, width - back * n_ch, 1), 0.0)
            toep_ref[gi, s * n_ch:(s + 1) * n_ch, :] = (fwd + bwd).astype(BF16)
        u = u_ref[gi]
        y = jnp.dot(u, toep_ref[gi], preferred_element_type=F32)
        z = jnp.dot(u, win_ref[gi], preferred_element_type=F32)
        half = z.shape[1] // 2
        zr, zi = z[:, :half], z[:, half:]
        dec = decay_ref[gi]
        ar, ai, sr, si = dec[0:1], dec[1:2], dec[2:3], dec[3:4]
        is_fwd = lax.broadcasted_iota(jnp.int32, (rb, half), 1) < half // 2

        xr = jnp.zeros((rb, half), F32)
        xi = jnp.zeros((rb, half), F32)
        ent_r, ent_i = [], []
        for j in range(n_steps):
            jb = n_steps - 1 - j
            ent_r.append(xr)
            ent_i.append(xi)
            zrj = jnp.where(is_fwd, zr[j * rb:(j + 1) * rb], zr[jb * rb:(jb + 1) * rb])
            zij = jnp.where(is_fwd, zi[j * rb:(j + 1) * rb], zi[jb * rb:(jb + 1) * rb])
            xr, xi = ar * xr - ai * xi + zrj, ar * xi + ai * xr + zij

        sub = lax.broadcasted_iota(jnp.int32, (SUBLANES, half), 0)
        fwd8 = lax.broadcasted_iota(jnp.int32, (SUBLANES, half), 1) < half // 2
        keep = (fwd8 & (sub > 0)) | (jnp.logical_not(fwd8) & (sub < SUBLANES - 1))
        car_r, car_i = [], []
        for b in range(bsz):
            er, ei = xr[b * SUBLANES:(b + 1) * SUBLANES], xi[b * SUBLANES:(b + 1) * SUBLANES]
            cr = jnp.zeros((SUBLANES, half), F32)
            ci = jnp.zeros((SUBLANES, half), F32)
            for _ in range(SUBLANES - 1):
                tr, ti = sr * cr - si * ci + er, sr * ci + si * cr + ei
                cr = jnp.where(keep, jnp.where(fwd8, pltpu.roll(tr, 1, 0), pltpu.roll(tr, SUBLANES - 1, 0)), 0.0)
                ci = jnp.where(keep, jnp.where(fwd8, pltpu.roll(ti, 1, 0), pltpu.roll(ti, SUBLANES - 1, 0)), 0.0)
            car_r.append(cr)
            car_i.append(ci)
        gr = car_r[0] if bsz == 1 else jnp.concatenate(car_r, axis=0)
        gi_ = car_i[0] if bsz == 1 else jnp.concatenate(car_i, axis=0)

        pr_all, pi_all = pow_ref[gi, 0], pow_ref[gi, 1]
        rows_r, rows_i = [], []
        for j in range(n_steps):
            jb = n_steps - 1 - j
            lr = jnp.where(is_fwd, ent_r[j], ent_r[jb])
            li = jnp.where(is_fwd, ent_i[j], ent_i[jb])
            pr, pi = pr_all[j:j + 1], pi_all[j:j + 1]
            rows_r.append(lr + pr * gr - pi * gi_)
            rows_i.append(li + pr * gi_ + pi * gr)
        state = jnp.concatenate([jnp.concatenate(rows_r, axis=0), jnp.concatenate(rows_i, axis=0)], axis=1)
        y = y + jnp.dot(state.astype(BF16), wout_ref[gi], preferred_element_type=F32)
        y = y + u.astype(F32) * dvec_ref[gi]
        o_ref[gi] = y.astype(o_ref.dtype)


def _s5_mix(u, tables, bsz, *, gb=4, name):
    kb_cat, kc_cat, w_in, w_out, decay, powers, dvec = tables
    T, width = u.shape
    S = T // bsz
    G = w_in.shape[0]
    N = width // G
    L = S5_CHUNK
    n_steps = S // (L * S5_SEGMENTS)
    rows = n_steps * bsz * S5_SEGMENTS
    gb = _pick_tile(G, gb, 1)
    ug = u.reshape(bsz, S5_SEGMENTS, n_steps, L, G, N)
    ug = jnp.transpose(ug, (4, 2, 0, 1, 3, 5)).reshape(G, rows, L * N)
    blk = lambda *shape: pl.BlockSpec((gb,) + shape, lambda g: (g,) + (0,) * len(shape))
    vmem = gb * (4 * rows * L * N * 2 + (L * N) ** 2 * 2 + 4 * kc_cat.shape[2] * L * N * 4
                 + 8 * L * N * w_in.shape[2] * 2) + (24 << 20)
    yg = pl.pallas_call(
        functools.partial(_s5_kernel, n_steps=n_steps, bsz=bsz, gb=gb),
        grid=(G // gb,),
        in_specs=[blk(rows, L * N), blk(*kb_cat.shape[1:]), blk(*kc_cat.shape[1:]),
                  blk(L * N, w_in.shape[2]), blk(w_out.shape[1], L * N),
                  blk(4, decay.shape[2]), blk(2, n_steps, powers.shape[3]), blk(1, L * N)],
        out_specs=blk(rows, L * N),
        out_shape=jax.ShapeDtypeStruct((G, rows, L * N), BF16),
        scratch_shapes=[pltpu.VMEM((gb, L * N, L * N), BF16)],
        compiler_params=_params(("parallel",), vmem),
        name=name,
    )(ug, kb_cat, kc_cat, w_in, w_out, decay, powers, dvec)
    yg = yg.reshape(G, n_steps, bsz, S5_SEGMENTS, L, N)
    return jnp.transpose(yg, (2, 3, 1, 4, 0, 5)).reshape(T, width)


def _s5_glu_kernel(y_ref, w_ref, b_ref, o_ref):
    g = jax.nn.gelu(y_ref[...].astype(F32), approximate=True)
    z = jnp.dot(g.astype(BF16), w_ref[...], preferred_element_type=F32) + b_ref[...]
    o_ref[...] = (g * jax.nn.sigmoid(z)).astype(o_ref.dtype)


def _s5_glu(y, w, b, *, tm=1024, name):
    M, W = y.shape
    w, layer = w
    tm = _pick_tile(M, tm, SUBLANES)
    vmem = 4 * tm * W * 2 + 2 * W * W * 2 + 4 * tm * W * 4 + (4 << 20)
    return pl.pallas_call(
        _s5_glu_kernel,
        grid=(M // tm,),
        in_specs=[pl.BlockSpec((tm, W), lambda i: (i, 0)),
                  pl.BlockSpec((None, W, W), lambda i: (layer, 0, 0)),
                  pl.BlockSpec((1, W), lambda i: (0, 0))],
        out_specs=pl.BlockSpec((tm, W), lambda i: (i, 0)),
        out_shape=jax.ShapeDtypeStruct((M, W), BF16),
        compiler_params=_params(("parallel",), vmem),
        name=name,
    )(y, w, b.reshape(1, W).astype(F32))


def _dilated_kernel(q_ref, k0, k1, k2, k3, v0, v1, v2, v3, o_ref, lse_ref, *, sub_len, half, nh):
    i = pl.program_id(1)
    tq = q_ref.shape[0]
    kb = k0.shape[0]
    dh = q_ref.shape[1] // nh
    kpos = (2 * i - 1) * kb + lax.broadcasted_iota(jnp.int32, (4 * kb, tq), 0)
    qpos = i * tq + lax.broadcasted_iota(jnp.int32, (4 * kb, tq), 1)
    valid = (jnp.abs(kpos - qpos) <= half) & (kpos >= 0) & (kpos < sub_len)
    head_row = lax.broadcasted_iota(jnp.int32, (nh, tq), 0)
    lse_all = jnp.zeros((nh, tq), F32)
    for h in range(nh):
        sl = slice(h * dh, (h + 1) * dh)
        kc = jnp.concatenate([k0[:, sl], k1[:, sl], k2[:, sl], k3[:, sl]], axis=0)
        vc = jnp.concatenate([v0[:, sl], v1[:, sl], v2[:, sl], v3[:, sl]], axis=0)
        st = lax.dot_general(kc, q_ref[:, sl], (((1,), (1,)), ((), ())), preferred_element_type=F32)
        st = jnp.where(valid, st, MASK_VALUE)
        m = jnp.max(st, axis=0, keepdims=True)
        e = jnp.exp2(st - m)
        den = jnp.sum(e, axis=0, keepdims=True)
        p = (e * (1.0 / den)).astype(BF16)
        o_ref[:, sl] = lax.dot_general(p, vc, (((0,), (0,)), ((), ())),
                                       preferred_element_type=F32).astype(o_ref.dtype)
        lse_all = jnp.where(head_row == h, m * LN_2 + jnp.log(den), lse_all)
    lse_ref[...] = lse_all


def _dilated_branch(q, k, v, cols, D, n_seq, *, half, nh, name):
    T = q.shape[0]
    sub_len = T // n_seq
    tq, kb = 2 * half, half
    nq, nkb = sub_len // tq, sub_len // kb
    cq, ck, cv = cols

    def kv_spec(off, c):
        return pl.BlockSpec((kb, D), lambda s, i: (s * nkb + jnp.clip(2 * i - 1 + off, 0, nkb - 1), c))

    vmem = 2 * (2 * tq * D * 2 + 8 * kb * D * 2) + (16 << 20)
    return pl.pallas_call(
        functools.partial(_dilated_kernel, sub_len=sub_len, half=half, nh=nh),
        grid=(n_seq, nq),
        in_specs=[pl.BlockSpec((tq, D), lambda s, i: (s * nq + i, cq))]
                 + [kv_spec(off, ck) for off in range(4)] + [kv_spec(off, cv) for off in range(4)],
        out_specs=[pl.BlockSpec((tq, D), lambda s, i: (s * nq + i, 0)),
                   pl.BlockSpec((nh, tq), lambda s, i: (0, s * nq + i))],
        out_shape=[jax.ShapeDtypeStruct((T, D), BF16), jax.ShapeDtypeStruct((nh, T), F32)],
        compiler_params=_params(("parallel", "parallel"), vmem),
        name=name,
    )(q, k, k, k, k, v, v, v, v)


def _residue_perm(dil):
    per = PERM_GROUP // dil
    rows = jnp.arange(PERM_GROUP)
    src = (rows % per) * dil + rows // per
    return (src[:, None] == rows[None, :]).astype(BF16)


def _to_residue_kernel(*refs, dils, n_in):
    n_d = len(dils)
    p_refs, x_refs, o_refs = refs[:n_d], refs[n_d:n_d + n_in], refs[n_d + n_in:]
    for xi, x_ref in enumerate(x_refs):
        for g in range(x_ref.shape[0] // PERM_GROUP):
            xg = x_ref[g * PERM_GROUP:(g + 1) * PERM_GROUP, :]
            for di, dil in enumerate(dils):
                o_ref = o_refs[di * n_in + xi]
                per = PERM_GROUP // dil
                y = jnp.dot(p_refs[di][...], xg, preferred_element_type=F32).astype(o_ref.dtype)
                for r in range(dil):
                    o_ref[r, g * per:(g + 1) * per, :] = y[r * per:(r + 1) * per]


def _to_residue(x, cols, width, dils, bsz, *, tb=512, name):
    T = x.shape[0]
    S = T // bsz
    tb = _pick_tile(S, tb, PERM_GROUP)
    nb = S // tb
    n_in = len(cols)
    perm_spec = pl.BlockSpec((PERM_GROUP, PERM_GROUP), lambda b, i: (0, 0))
    outs = pl.pallas_call(
        functools.partial(_to_residue_kernel, dils=tuple(dils), n_in=n_in),
        grid=(bsz, nb),
        in_specs=[perm_spec] * len(dils)
                 + [pl.BlockSpec((tb, width), lambda b, i, c=c: (b * nb + i, c)) for c in cols],
        out_specs=[pl.BlockSpec((None, dil, tb // dil, width), lambda b, i: (b, 0, i, 0))
                   for dil in dils for _ in cols],
        out_shape=[jax.ShapeDtypeStruct((bsz, dil, S // dil, width), x.dtype) for dil in dils for _ in cols],
        compiler_params=_params(("parallel", "parallel"),
                                (1 + len(dils)) * n_in * 2 * tb * width * 2 + (12 << 20)),
        name=name,
    )(*[_residue_perm(dil) for dil in dils], *([x] * n_in))
    outs = [o.reshape(T, width) for o in outs]
    return [outs[di * n_in:(di + 1) * n_in] for di in range(len(dils))]


def _from_residue_kernel(p_ref, x_ref, o_ref, *, dil):
    per = PERM_GROUP // dil
    for g in range(o_ref.shape[0] // PERM_GROUP):
        xg = jnp.concatenate([x_ref[r, g * per:(g + 1) * per, :] for r in range(dil)], axis=0)
        o_ref[g * PERM_GROUP:(g + 1) * PERM_GROUP, :] = jnp.dot(
            p_ref[...], xg, preferred_element_type=F32).astype(o_ref.dtype)


def _from_residue(x, dil, bsz, *, tb=1024, name):
    T, width = x.shape
    S = T // bsz
    tb = _pick_tile(S, tb, PERM_GROUP)
    nb = S // tb
    return pl.pallas_call(
        functools.partial(_from_residue_kernel, dil=dil),
        grid=(bsz, nb),
        in_specs=[pl.BlockSpec((PERM_GROUP, PERM_GROUP), lambda b, i: (0, 0)),
                  pl.BlockSpec((None, dil, tb // dil, width), lambda b, i: (b, 0, i, 0))],
        out_specs=pl.BlockSpec((tb, width), lambda b, i: (b * nb + i, 0)),
        out_shape=jax.ShapeDtypeStruct((T, width), x.dtype),
        compiler_params=_params(("parallel", "parallel"), 4 * tb * width * 2 + (8 << 20)),
        name=name,
    )(_residue_perm(dil).T, x.reshape(bsz, dil, S // dil, width))


def _merge_proj_kernel(o1, o2, o3, l1, l2, l3, w_ref, res_ref, out_ref, a_ref, *, nh):
    @pl.when(pl.program_id(1) == 0)
    def _():
        lse = [l1[...], l2[...], l3[...]]
        mx = jnp.maximum(jnp.maximum(lse[0], lse[1]), lse[2])
        e = [jnp.exp(x - mx) for x in lse]
        inv = 1.0 / (e[0] + e[1] + e[2])
        w1, w2 = e[0] * inv, e[1] * inv
        dh = a_ref.shape[1] // nh
        for h in range(nh):
            sl = slice(h * dh, (h + 1) * dh)
            b1, b2, b3 = o1[:, sl].astype(F32), o2[:, sl].astype(F32), o3[:, sl].astype(F32)
            a_ref[:, sl] = (b3 + w1[:, h:h + 1] * (b1 - b3) + w2[:, h:h + 1] * (b2 - b3)).astype(BF16)

    out_ref[...] = res_ref[...] + jnp.dot(a_ref[...], w_ref[...], preferred_element_type=F32)


def _merge_proj(outs, lses, w, res, *, nh, tm=512, tn=1024, name):
    M, D = outs[0].shape
    w, layer = w
    N = w.shape[2]
    tm = _pick_tile(M, tm, SUBLANES)
    tn = _pick_tile(N, tn, LANES)
    row = lambda width: pl.BlockSpec((tm, width), lambda i, j: (i, 0))
    vmem = 3 * 2 * tm * D * 2 + tm * D * 2 + 2 * D * tn * 2 + 5 * tm * tn * 4 + 4 * tm * D * 4 + (4 << 20)
    return pl.pallas_call(
        functools.partial(_merge_proj_kernel, nh=nh),
        grid=(M // tm, N // tn),
        in_specs=[row(D)] * 3 + [row(nh)] * 3
                 + [pl.BlockSpec((None, D, tn), lambda i, j: (layer, 0, j)),
                    pl.BlockSpec((tm, tn), lambda i, j: (i, j))],
        out_specs=pl.BlockSpec((tm, tn), lambda i, j: (i, j)),
        out_shape=jax.ShapeDtypeStruct((M, N), F32),
        scratch_shapes=[pltpu.VMEM((tm, D), BF16)],
        compiler_params=_params(("parallel", "arbitrary"), vmem),
        name=name,
    )(*outs, *lses, w, res)


def _final_norm_kernel(x_ref, g_ref, o_ref):
    o_ref[...] = _rms_normalize(x_ref[...], g_ref[...])


def _final_norm(x, g, *, tm=1024, name):
    M, D = x.shape
    tm = _pick_tile(M, tm, SUBLANES)
    return pl.pallas_call(
        _final_norm_kernel,
        grid=(M // tm,),
        in_specs=[pl.BlockSpec((tm, D), lambda i: (i, 0)), pl.BlockSpec((1, D), lambda i: (0, 0))],
        out_specs=pl.BlockSpec((tm, D), lambda i: (i, 0)),
        out_shape=jax.ShapeDtypeStruct((M, D), F32),
        compiler_params=_params(("parallel",), 6 * tm * D * 4 + (4 << 20)),
        name=name,
    )(x, g.reshape(1, D).astype(F32))


def _even_mixer(h, g, w_in, w_out, s5_tabs, glu_w, glu_b, lam_vec, subln_g, lambda_init, rope_tabs, bsz, tag):
    T, _ = h.shape
    S = T // bsz
    s5_width = glu_w[0].shape[1]
    width = (w_in[0].shape[2] - s5_width) // 3
    n_heads = width // (2 * HEAD_DIM)
    segs = ((s5_width, s5_width + width, True, HEAD_DIM ** -0.5 * LOG2_E),
            (s5_width + width, s5_width + 2 * width, True, 1.0))
    proj = _norm_matmul(h, g, w_in, segs=segs, rope_tabs=rope_tabs, seq=S, name=f"even_in_{tag}")
    y_s5 = _s5_mix(proj[:, :s5_width], s5_tabs, bsz, name=f"s5_{tag}")
    y_s5 = _s5_glu(y_s5, glu_w, glu_b, name=f"s5_glu_{tag}")
    y_diff = _diff_attn(proj, lam_vec, subln_g, bsz, s5_width=s5_width, n_heads=n_heads,
                        lambda_init=lambda_init, unroll=16, name=f"diff_attn_{tag}")
    return _matmul_res([y_s5, y_diff], w_out, h, tn=1024, name=f"even_out_{tag}")


def _odd_mixer(h, g, w_qkv, w_out, rope_tabs, bsz, tag):
    T, D = h.shape
    S = T // bsz
    nh = D // HEAD_DIM
    segs = ((0, D, True, HEAD_DIM ** -0.5 * LOG2_E), (D, 2 * D, True, 1.0))
    qkv = _norm_matmul(h, g, w_qkv, segs=segs, rope_tabs=rope_tabs, seq=S, name=f"odd_in_{tag}")
    outs, lses = [], []
    dils = [dil for _, dil in DILATED_BRANCHES if dil > 1]
    residue = dict(zip(dils, _to_residue(qkv, (0, 1, 2), D, dils, bsz, name=f"to_residue_{tag}")))
    for window, dil in DILATED_BRANCHES:
        half = window // (2 * dil)
        sub_len = S // dil

        if dil == 1:
            o, lse = _dilated_branch(qkv, qkv, qkv, (0, 1, 2), D, bsz, half=half, nh=nh,
                                     name=f"dilated_{dil}_{tag}")
        else:
            qr, kr, vr = residue[dil]
            o, lse = _dilated_branch(qr, kr, vr, (0, 0, 0), D, bsz * dil, half=half, nh=nh,
                                     name=f"dilated_{dil}_{tag}")
            o = _from_residue(o, dil, bsz, name=f"from_residue_{dil}_{tag}")
        lse = jnp.transpose(lse.reshape(nh, bsz, dil, sub_len), (1, 3, 2, 0)).reshape(T, nh)
        outs.append(o)
        lses.append(lse)
    return _merge_proj(outs, lses, w_out, h, nh=nh, name=f"odd_out_{tag}")


def kernel(x, mem, norm_mix_g, norm_xa_g, norm_mem_g, xa_wq, xa_wkv, xa_wo, norm_ffn_g, ffn_w13, ffn_w2, ab_w_in, ab_w_out, s5_lambda_re, s5_lambda_im, s5_log_step, s5_b_re, s5_b_im, s5_c_re, s5_c_im, s5_d, s5_glu_w, s5_glu_b, diff_lambda, diff_subln_g, c_w_qkv, c_w_out, final_norm_g):
    bsz, S, D = x.shape
    T = bsz * S
    depth = norm_mix_g.shape[0]
    mem_len = mem.shape[1]
    rope_tabs = _rope_tables(S)
    n_steps = S // (S5_CHUNK * S5_SEGMENTS)
    s5_width = s5_glu_w.shape[1]
    diff_width = (ab_w_in.shape[2] - s5_width) // 3
    ab_w_in = _rope_columns(ab_w_in, s5_width, s5_width + 2 * diff_width)
    c_w_qkv = _rope_columns(c_w_qkv, 0, 2 * D)
    ab_w_in, ab_w_out, s5_glu_w, c_w_qkv, c_w_out, xa_wq, xa_wkv, xa_wo, ffn_w13, ffn_w2 = (
        w.astype(BF16) for w in (ab_w_in, ab_w_out, s5_glu_w, c_w_qkv, c_w_out, xa_wq, xa_wkv, xa_wo,
                                 ffn_w13, ffn_w2))

    h = x.reshape(T, D)
    mem2 = mem.reshape(bsz * mem_len, D)
    for layer in range(depth):
        i = layer // 2
        if layer % 2 == 0:
            lambda_init = 0.8 - 0.6 * math.exp(-0.3 * layer)
            tabs = _s5_tables(s5_lambda_re[i], s5_lambda_im[i], s5_log_step[i], s5_b_re[i], s5_b_im[i],
                              s5_c_re[i], s5_c_im[i], s5_d[i], n_steps)
            h = _even_mixer(h, norm_mix_g[layer], (ab_w_in, i), (ab_w_out, i), tabs, (s5_glu_w, i),
                            s5_glu_b[i], diff_lambda[i], diff_subln_g[i], lambda_init, rope_tabs, bsz, layer)
        else:
            h = _odd_mixer(h, norm_mix_g[layer], (c_w_qkv, i), (c_w_out, i), rope_tabs, bsz, layer)

        q = _norm_matmul(h, norm_xa_g[layer], (xa_wq, layer),
                         segs=((0, D, False, (D // XA_HEADS) ** -0.5),), name=f"xa_q_{layer}")
        kv = _norm_matmul(mem2, norm_mem_g[layer], (xa_wkv, layer), name=f"xa_kv_{layer}")
        o = _cross_attn(q, kv, bsz, name=f"xa_core_{layer}")
        h = _matmul_res([o], (xa_wo, layer), h, tn=1024, name=f"xa_out_{layer}")

        hid = _swiglu_up(h, norm_ffn_g[layer], (ffn_w13, layer), name=f"ffn_up_{layer}")
        h = _matmul_res([hid], (ffn_w2, layer), h, tn=512, name=f"ffn_down_{layer}")
    return _final_norm(h, final_norm_g, name="final_norm").reshape(bsz, S, D)
```

```python
import functools
import math

import jax
import jax.numpy as jnp
from jax import lax
from jax.experimental import pallas as pl
from jax.experimental.pallas import tpu as pltpu

F32 = jnp.float32
BF16 = jnp.bfloat16

NORM_EPS = 1e-6
MASK_VALUE = -1e30
LOG2_E = math.log2(math.e)
LN_2 = math.log(2.0)
ROPE_THETA = 500000.0
ROPE_HALF = 16
HEAD_DIM = 128
XA_HEADS = 4
DILATED_BRANCHES = ((128, 1), (512, 4), (2048, 16))
S5_CHUNK = 32
S5_SEGMENTS = 8
PERM_GROUP = 256
LANES = 128
SUBLANES = 8
VMEM_CAP_BYTES = 56 * 1024 * 1024


def _pick_tile(n, pref, quantum):
    t = (min(n, pref) // quantum) * quantum
    while t >= quantum:
        if n % t == 0:
            return t
        t -= quantum
    return n


def _params(semantics, vmem_bytes):
    return pltpu.CompilerParams(
        dimension_semantics=semantics,
        vmem_limit_bytes=int(min(VMEM_CAP_BYTES, max(32 * 1024 * 1024, vmem_bytes))))


def _rms_normalize(x, g):
    ms = jnp.mean(x * x, axis=-1, keepdims=True)
    return (x * lax.rsqrt(ms + NORM_EPS)) * g


def _rope_lanes(y, cf, sf):
    outs = []
    for c in range(y.shape[1] // LANES):
        yc = y[:, c * LANES:(c + 1) * LANES]
        outs.append(yc * cf + pltpu.roll(yc, LANES // 2, 1) * sf)
    return outs[0] if len(outs) == 1 else jnp.concatenate(outs, axis=1)


def _rope_columns(w, lo, hi):
    mid = w[..., lo:hi]
    comp = mid.reshape(mid.shape[:-1] + ((hi - lo) // LANES, LANES))
    r, h = ROPE_HALF, LANES // 2
    comp = jnp.concatenate([comp[..., :r], comp[..., 2 * r:h + r], comp[..., r:2 * r], comp[..., h + r:]], axis=-1)
    return jnp.concatenate([w[..., :lo], comp.reshape(mid.shape), w[..., hi:]], axis=-1)


def _rope_tables(seq):
    pos = jnp.arange(seq, dtype=F32)
    inv = ROPE_THETA ** (-jnp.arange(0, 2 * ROPE_HALF, 2, dtype=F32) / (2 * ROPE_HALF))
    ang = pos[:, None] * inv[None, :]
    cos, sin = jnp.cos(ang), jnp.sin(ang)
    gap = LANES // 2 - ROPE_HALF
    cf = jnp.concatenate([cos, jnp.ones((seq, gap), F32), cos, jnp.ones((seq, gap), F32)], axis=1)
    sf = jnp.concatenate([-sin, jnp.zeros((seq, gap), F32), sin, jnp.zeros((seq, gap), F32)], axis=1)
    return cf, sf


def _norm_matmul_kernel(*refs, segs, tn, use_rope):
    if use_rope:
        x_ref, g_ref, w_ref, cf_ref, sf_ref, o_ref, xn_ref = refs
    else:
        x_ref, g_ref, w_ref, o_ref, xn_ref = refs
    j = pl.program_id(1)

    @pl.when(j == 0)
    def _():
        xn_ref[...] = _rms_normalize(x_ref[...], g_ref[...]).astype(BF16)

    y = jnp.dot(xn_ref[...], w_ref[...], preferred_element_type=F32)
    for lo, hi, rope, scale in segs:
        @pl.when(jnp.logical_and(j >= lo // tn, j < hi // tn))
        def _(rope=rope, scale=scale):
            z = y
            if rope:
                z = _rope_lanes(z, cf_ref[...], sf_ref[...])
            if scale != 1.0:
                z = z * scale
            o_ref[...] = z.astype(o_ref.dtype)


def _norm_matmul(x, g, w, *, segs=(), rope_tabs=None, seq=None, tm=1024, tn=1024, name):
    M, D = x.shape
    w, layer = w
    N = w.shape[2]
    tm = _pick_tile(M if seq is None else seq, tm, SUBLANES)
    tn = _pick_tile(math.gcd(N, *[b for s in segs for b in s[:2]]), tn, LANES)
    full, pos = [], 0
    for lo, hi, rope, scale in sorted(segs):
        if lo > pos:
            full.append((pos, lo, False, 1.0))
        full.append((lo, hi, rope, scale))
        pos = hi
    if pos < N:
        full.append((pos, N, False, 1.0))
    for lo, hi, _, _ in full:
        assert lo % tn == 0 and hi % tn == 0, (lo, hi, tn)
    use_rope = any(s[2] for s in full)
    in_specs = [pl.BlockSpec((tm, D), lambda i, j: (i, 0)),
                pl.BlockSpec((1, D), lambda i, j: (0, 0)),
                pl.BlockSpec((None, D, tn), lambda i, j: (layer, 0, j))]
    args = [x, g.reshape(1, D).astype(F32), w]
    if use_rope:
        nseq = seq // tm
        in_specs += [pl.BlockSpec((tm, LANES), lambda i, j: (i % nseq, 0))] * 2
        args += list(rope_tabs)
    vmem = 2 * tm * D * 4 + tm * D * 2 + 2 * D * tn * 2 + 2 * tm * tn * 2 + 3 * tm * tn * 4 + (4 << 20)
    return pl.pallas_call(
        functools.partial(_norm_matmul_kernel, segs=tuple(full), tn=tn, use_rope=use_rope),
        grid=(M // tm, N // tn),
        in_specs=in_specs,
        out_specs=pl.BlockSpec((tm, tn), lambda i, j: (i, j)),
        out_shape=jax.ShapeDtypeStruct((M, N), BF16),
        scratch_shapes=[pltpu.VMEM((tm, D), BF16)],
        compiler_params=_params(("parallel", "arbitrary"), vmem),
        name=name,
    )(*args)


def _matmul_res_kernel(*refs, n_a):
    a_refs, w_refs = refs[:n_a], refs[n_a:2 * n_a]
    res_ref, o_ref = refs[2 * n_a], refs[2 * n_a + 1]
    acc = res_ref[...]
    for a_ref, w_ref in zip(a_refs, w_refs):
        acc = acc + jnp.dot(a_ref[...], w_ref[...], preferred_element_type=F32)
    o_ref[...] = acc


def _matmul_res(a_list, w, res, *, tm=1024, tn=512, name):
    n_a = len(a_list)
    M, K = a_list[0].shape
    w, layer = w
    N = w.shape[2]
    tm = _pick_tile(M, tm, SUBLANES)
    tn = _pick_tile(N, tn, LANES)
    in_specs = [pl.BlockSpec((tm, K), lambda i, j: (i, 0)) for _ in range(n_a)]
    in_specs += [pl.BlockSpec((None, K, tn), lambda i, j, r=r: (layer, r, j)) for r in range(n_a)]
    in_specs += [pl.BlockSpec((tm, tn), lambda i, j: (i, j))]
    vmem = n_a * (2 * tm * K * 2 + 2 * K * tn * 2) + 5 * tm * tn * 4 + (4 << 20)
    return pl.pallas_call(
        functools.partial(_matmul_res_kernel, n_a=n_a),
        grid=(M // tm, N // tn),
        in_specs=in_specs,
        out_specs=pl.BlockSpec((tm, tn), lambda i, j: (i, j)),
        out_shape=jax.ShapeDtypeStruct((M, N), F32),
        compiler_params=_params(("parallel", "parallel"), vmem),
        name=name,
    )(*a_list, *([w] * n_a), res)


def _swiglu_up_kernel(x_ref, g_ref, w1_ref, w3_ref, o_ref, xn_ref):
    @pl.when(pl.program_id(1) == 0)
    def _():
        xn_ref[...] = _rms_normalize(x_ref[...], g_ref[...]).astype(BF16)

    xn = xn_ref[...]
    a = jnp.dot(xn, w1_ref[...], preferred_element_type=F32)
    b = jnp.dot(xn, w3_ref[...], preferred_element_type=F32)
    o_ref[...] = (a * jax.nn.sigmoid(a) * b).astype(o_ref.dtype)


def _swiglu_up(x, g, w13, *, tm=1024, tn=512, name):
    M, D = x.shape
    w13, layer = w13
    H = w13.shape[2] // 2
    tm = _pick_tile(M, tm, SUBLANES)
    tn = _pick_tile(H, tn, LANES)
    nj = H // tn
    vmem = 2 * tm * D * 4 + tm * D * 2 + 4 * D * tn * 2 + 2 * tm * tn * 2 + 4 * tm * tn * 4 + (4 << 20)
    return pl.pallas_call(
        _swiglu_up_kernel,
        grid=(M // tm, nj),
        in_specs=[pl.BlockSpec((tm, D), lambda i, j: (i, 0)),
                  pl.BlockSpec((1, D), lambda i, j: (0, 0)),
                  pl.BlockSpec((None, D, tn), lambda i, j: (layer, 0, j)),
                  pl.BlockSpec((None, D, tn), lambda i, j: (layer, 0, j + nj))],
        out_specs=pl.BlockSpec((tm, tn), lambda i, j: (i, j)),
        out_shape=jax.ShapeDtypeStruct((M, H), BF16),
        scratch_shapes=[pltpu.VMEM((tm, D), BF16)],
        compiler_params=_params(("parallel", "arbitrary"), vmem),
        name=name,
    )(x, g.reshape(1, D).astype(F32), w13, w13)


def _cross_attn_kernel(q_ref, k_ref, v_ref, o_ref, *, nh):
    dh = q_ref.shape[1] // nh
    for h in range(nh):
        sl = slice(h * dh, (h + 1) * dh)
        s = lax.dot_general(q_ref[:, sl], k_ref[:, sl], (((1,), (1,)), ((), ())), preferred_element_type=F32)
        m = jnp.max(s, axis=-1, keepdims=True)
        e = jnp.exp(s - m)
        den = jnp.sum(e, axis=-1, keepdims=True)
        o = jnp.dot(e.astype(BF16), v_ref[:, sl], preferred_element_type=F32)
        o_ref[:, sl] = (o / den).astype(o_ref.dtype)


def _cross_attn(q, kv, bsz, *, tq=1024, name):
    T, D = q.shape
    S = T // bsz
    mem_len = kv.shape[0] // bsz
    tq = _pick_tile(S, tq, SUBLANES)
    nq = S // tq
    kv3 = kv.reshape(bsz, mem_len, 2 * D)
    vmem = 4 * tq * D * 2 + 4 * mem_len * D * 2 + 4 * tq * mem_len * 4 + (8 << 20)
    return pl.pallas_call(
        functools.partial(_cross_attn_kernel, nh=XA_HEADS),
        grid=(bsz, nq),
        in_specs=[pl.BlockSpec((tq, D), lambda b, i: (b * nq + i, 0)),
                  pl.BlockSpec((None, mem_len, D), lambda b, i: (b, 0, 0)),
                  pl.BlockSpec((None, mem_len, D), lambda b, i: (b, 0, 1))],
        out_specs=pl.BlockSpec((tq, D), lambda b, i: (b * nq + i, 0)),
        out_shape=jax.ShapeDtypeStruct((T, D), BF16),
        compiler_params=_params(("parallel", "parallel"), vmem),
        name=name,
    )(q, kv3, kv3)


def _diff_attn_kernel(q_ref, k_ref, v_ref, lv_ref, sg_ref, o_ref, vt_ref, *, tk, unroll, lambda_init):
    tq = q_ref.shape[0]
    seq = k_ref.shape[0]
    dv = v_ref.shape[1]
    dh = dv // 2

    @pl.when(pl.program_id(2) == 0)
    def _():
        for c in range(seq // tk):
            vt_ref[:, c * tk:(c + 1) * tk] = v_ref[c * tk:(c + 1) * tk, :].astype(F32).T.astype(BF16)

    q = q_ref[...]
    nk = seq // tk

    def scores(kk):
        kc = k_ref[pl.ds(pl.multiple_of(kk * tk, tk), tk), :]
        return tuple(lax.dot_general(kc[:, c * dh:(c + 1) * dh], q[:, c * dh:(c + 1) * dh],
                                     (((1,), (1,)), ((), ())), preferred_element_type=F32)
                     for c in range(2))

    def body(kk, carry):
        st_next = scores(jnp.minimum(kk + 1, nk - 1))
        vt = vt_ref[:, pl.ds(pl.multiple_of(kk * tk, tk), tk)]
        new = []
        for c in range(2):
            st, m, l, acc = carry[c]
            m_new = jnp.maximum(m, jnp.max(st, axis=0, keepdims=True))
            alpha = jnp.exp2(m - m_new)
            pt = jnp.exp2(st - m_new)
            l = alpha * l + jnp.sum(pt, axis=0, keepdims=True)
            acc = alpha * acc + jnp.dot(vt, pt.astype(BF16), preferred_element_type=F32)
            new.append((st_next[c], m_new, l, acc))
        return tuple(new)

    st0 = scores(0)
    init = tuple((st0[c], jnp.full((1, tq), MASK_VALUE, F32), jnp.zeros((1, tq), F32),
                  jnp.zeros((dv, tq), F32)) for c in range(2))
    (_, _, l1, a1), (_, _, l2, a2) = lax.fori_loop(0, nk, body, init, unroll=unroll)
    lv = lv_ref[...]
    lam = (jnp.exp(jnp.sum(lv[0:1] * lv[1:2], axis=-1, keepdims=True))
           - jnp.exp(jnp.sum(lv[2:3] * lv[3:4], axis=-1, keepdims=True)) + lambda_init)
    ot = a1 / l1 - lam * (a2 / l2)
    ms = jnp.mean(ot * ot, axis=0, keepdims=True)
    ot = ot * lax.rsqrt(ms + NORM_EPS)
    o_ref[...] = (ot.T * sg_ref[...] * (1.0 - lambda_init)).astype(o_ref.dtype)


def _diff_attn(proj, lam_vec, subln_g, bsz, *, s5_width, n_heads, lambda_init, tq=512, tk=512, unroll=1, name):
    T, ncol = proj.shape
    S = T // bsz
    dv = 2 * HEAD_DIM
    width = n_heads * dv
    q0, k0, v0 = s5_width // dv, (s5_width + width) // dv, (s5_width + 2 * width) // dv
    tq = _pick_tile(S, tq, SUBLANES)
    tk = _pick_tile(S, tk, LANES)
    nq = S // tq
    proj3 = proj.reshape(bsz, S, ncol)
    vmem = 4 * tq * dv * 2 + 5 * S * dv * 2 + unroll * (3 * tq * tk * 4 + 2 * tq * dv * 4) + (8 << 20)
    return pl.pallas_call(
        functools.partial(_diff_attn_kernel, tk=tk, unroll=unroll, lambda_init=lambda_init),
        grid=(bsz, n_heads, nq),
        in_specs=[pl.BlockSpec((None, tq, dv), lambda b, h, i: (b, i, q0 + h)),
                  pl.BlockSpec((None, S, dv), lambda b, h, i: (b, 0, k0 + h)),
                  pl.BlockSpec((None, S, dv), lambda b, h, i: (b, 0, v0 + h)),
                  pl.BlockSpec((4, HEAD_DIM), lambda b, h, i: (0, 0)),
                  pl.BlockSpec((1, dv), lambda b, h, i: (0, 0))],
        out_specs=pl.BlockSpec((tq, dv), lambda b, h, i: (b * nq + i, h)),
        out_shape=jax.ShapeDtypeStruct((T, width), BF16),
        scratch_shapes=[pltpu.VMEM((dv, S), BF16)],
        compiler_params=_params(("parallel", "parallel", "arbitrary"), vmem),
        name=name,
    )(proj3, proj3, proj3, lam_vec.astype(F32), subln_g.reshape(1, dv).astype(F32))


def _s5_tables(lam_re, lam_im, log_step, b_re, b_im, c_re, c_im, d_skip, n_steps):
    L = S5_CHUNK
    _, G, P = lam_re.shape
    N = b_re.shape[-1]
    lr, li = lam_re.astype(F32), lam_im.astype(F32)
    step = jnp.exp(log_step.astype(F32))[..., None]
    ar, ai = lr * step, li * step

    def cexp(t, axes=()):
        a_r, a_i = jnp.expand_dims(ar, axes), jnp.expand_dims(ai, axes)
        mag = jnp.exp(t * a_r)
        return mag * jnp.cos(t * a_i), mag * jnp.sin(t * a_i)

    def cmul(xr, xi, yr, yi):
        return xr * yr - xi * yi, xr * yi + xi * yr

    er, ei = cexp(1.0)
    den = lr * lr + li * li
    qr, qi = ((er - 1.0) * lr + ei * li) / den, (ei * lr - (er - 1.0) * li) / den
    br, bi = cmul(qr[..., None], qi[..., None], b_re.astype(F32), b_im.astype(F32))
    cr, ci = c_re.astype(F32), c_im.astype(F32)
    tau = jnp.arange(L + 1, dtype=F32)[None, None, :, None]
    pr, pi = cexp(tau, axes=2)

    wr, wi = cmul(pr[:, :, :L, :, None], pi[:, :, :L, :, None], br[:, :, None], bi[:, :, None])
    kb_cat = jnp.transpose(jnp.concatenate([jnp.swapaxes(br, 2, 3), -jnp.swapaxes(bi, 2, 3)], axis=3),
                           (1, 0, 2, 3))
    pcr, pci = cmul(pr[:, :, :L, None, :], pi[:, :, :L, None, :], cr[:, :, None], ci[:, :, None])
    def _kc_rows(w):
        w = jnp.stack([w[0], w[1][:, ::-1]], axis=1)
        return jnp.transpose(w, (0, 1, 4, 2, 3)).reshape(G, 2, P, L * N)
    kc_cat = jnp.concatenate([_kc_rows(pcr), _kc_rows(pci)], axis=2)

    def _in_cols(w):
        return jnp.transpose(w, (0, 1, 3, 2)).reshape(G, L * N, P)
    w_in = jnp.concatenate([_in_cols(wr[0][:, ::-1]), _in_cols(wr[1]),
                            _in_cols(wi[0][:, ::-1]), _in_cols(wi[1])], axis=2)

    of_r, of_i = cmul(pr[0][:, 1:, None, :], pi[0][:, 1:, None, :], cr[0][:, None], ci[0][:, None])
    ob_r, ob_i = cmul(pr[1][:, ::-1][:, :L, None, :], pi[1][:, ::-1][:, :L, None, :],
                      cr[1][:, None], ci[1][:, None])
    def _out_rows(w):
        return jnp.transpose(w, (0, 3, 1, 2)).reshape(G, P, L * N)
    w_out = jnp.concatenate([_out_rows(of_r), _out_rows(ob_r), -_out_rows(of_i), -_out_rows(ob_i)], axis=1)

    def _lanes(z):
        return jnp.concatenate([z[0], z[1]], axis=-1)
    a_chunk, a_seg = cexp(float(L)), cexp(float(L * n_steps))
    decay = jnp.stack([_lanes(a_chunk[0]), _lanes(a_chunk[1]), _lanes(a_seg[0]), _lanes(a_seg[1])], axis=1)
    jj = float(L) * jnp.arange(n_steps, dtype=F32)
    jj = jnp.stack([jj, jj[::-1]], axis=0)[:, None, :, None]
    powers = jnp.stack([_lanes(p) for p in cexp(jj, axes=2)], axis=1)
    dvec = jnp.tile(d_skip.astype(F32).reshape(G, 1, N), (1, L, 1)).reshape(G, 1, L * N)
    return kb_cat, kc_cat, w_in.astype(BF16), w_out.astype(BF16), decay, powers, dvec


def _s5_kernel(u_ref, kb_ref, kc_ref, win_ref, wout_ref, decay_ref, pow_ref, dvec_ref, o_ref, toep_ref,
               *, n_steps, bsz, gb):
    rb = bsz * SUBLANES
    n_ch = kb_ref.shape[2]
    width = kc_ref.shape[3]
    chunk = width // n_ch
    lane = lax.broadcasted_iota(jnp.int32, (n_ch, width), 1)
    for gi in range(gb):
        kf = jnp.dot(kb_ref[gi, 0], kc_ref[gi, 0], preferred_element_type=F32, precision=lax.Precision.HIGHEST)
        kb = jnp.dot(kb_ref[gi, 1], kc_ref[gi, 1], preferred_element_type=F32, precision=lax.Precision.HIGHEST)
        for s in range(chunk):
            fwd = kf if s == 0 else jnp.where(lane >= s * n_ch, pltpu.roll(kf, s * n_ch, 1), 0.0)
            back = chunk - 1 - s
            bwd = kb if back == 0 else jnp.where(lane < (s + 1) * n_ch, pltpu.roll(kb, width - back * n_ch, 1), 0.0)
            toep_ref[gi, s * n_ch:(s + 1) * n_ch, :] = (fwd + bwd).astype(BF16)
        u = u_ref[gi]
        y = jnp.dot(u, toep_ref[gi], preferred_element_type=F32)
        z = jnp.dot(u, win_ref[gi], preferred_element_type=F32)
        half = z.shape[1] // 2
        zr, zi = z[:, :half], z[:, half:]
        dec = decay_ref[gi]
        ar, ai, sr, si = dec[0:1], dec[1:2], dec[2:3], dec[3:4]
        is_fwd = lax.broadcasted_iota(jnp.int32, (rb, half), 1) < half // 2

        xr = jnp.zeros((rb, half), F32)
        xi = jnp.zeros((rb, half), F32)
        ent_r, ent_i = [], []
        for j in range(n_steps):
            jb = n_steps - 1 - j
            ent_r.append(xr)
            ent_i.append(xi)
            zrj = jnp.where(is_fwd, zr[j * rb:(j + 1) * rb], zr[jb * rb:(jb + 1) * rb])
            zij = jnp.where(is_fwd, zi[j * rb:(j + 1) * rb], zi[jb * rb:(jb + 1) * rb])
            xr, xi = ar * xr - ai * xi + zrj, ar * xi + ai * xr + zij

        sub = lax.broadcasted_iota(jnp.int32, (SUBLANES, half), 0)
        fwd8 = lax.broadcasted_iota(jnp.int32, (SUBLANES, half), 1) < half // 2
        keep = (fwd8 & (sub > 0)) | (jnp.logical_not(fwd8) & (sub < SUBLANES - 1))
        car_r, car_i = [], []
        for b in range(bsz):
            er, ei = xr[b * SUBLANES:(b + 1) * SUBLANES], xi[b * SUBLANES:(b + 1) * SUBLANES]
            cr = jnp.zeros((SUBLANES, half), F32)
            ci = jnp.zeros((SUBLANES, half), F32)
            for _ in range(SUBLANES - 1):
                tr, ti = sr * cr - si * ci + er, sr * ci + si * cr + ei
                cr = jnp.where(keep, jnp.where(fwd8, pltpu.roll(tr, 1, 0), pltpu.roll(tr, SUBLANES - 1, 0)), 0.0)
                ci = jnp.where(keep, jnp.where(fwd8, pltpu.roll(ti, 1, 0), pltpu.roll(ti, SUBLANES - 1, 0)), 0.0)
            car_r.append(cr)
            car_i.append(ci)
        gr = car_r[0] if bsz == 1 else jnp.concatenate(car_r, axis=0)
        gi_ = car_i[0] if bsz == 1 else jnp.concatenate(car_i, axis=0)

        pr_all, pi_all = pow_ref[gi, 0], pow_ref[gi, 1]
        rows_r, rows_i = [], []
        for j in range(n_steps):
            jb = n_steps - 1 - j
            lr = jnp.where(is_fwd, ent_r[j], ent_r[jb])
            li = jnp.where(is_fwd, ent_i[j], ent_i[jb])
            pr, pi = pr_all[j:j + 1], pi_all[j:j + 1]
            rows_r.append(lr + pr * gr - pi * gi_)
            rows_i.append(li + pr * gi_ + pi * gr)
        state = jnp.concatenate([jnp.concatenate(rows_r, axis=0), jnp.concatenate(rows_i, axis=0)], axis=1)
        y = y + jnp.dot(state.astype(BF16), wout_ref[gi], preferred_element_type=F32)
        y = y + u.astype(F32) * dvec_ref[gi]
        o_ref[gi] = y.astype(o_ref.dtype)


def _s5_mix(u, tables, bsz, *, gb=4, name):
    kb_cat, kc_cat, w_in, w_out, decay, powers, dvec = tables
    T, width = u.shape
    S = T // bsz
    G = w_in.shape[0]
    N = width // G
    L = S5_CHUNK
    n_steps = S // (L * S5_SEGMENTS)
    rows = n_steps * bsz * S5_SEGMENTS
    gb = _pick_tile(G, gb, 1)
    ug = u.reshape(bsz, S5_SEGMENTS, n_steps, L, G, N)
    ug = jnp.transpose(ug, (4, 2, 0, 1, 3, 5)).reshape(G, rows, L * N)
    blk = lambda *shape: pl.BlockSpec((gb,) + shape, lambda g: (g,) + (0,) * len(shape))
    vmem = gb * (4 * rows * L * N * 2 + (L * N) ** 2 * 2 + 4 * kc_cat.shape[2] * L * N * 4
                 + 8 * L * N * w_in.shape[2] * 2) + (24 << 20)
    yg = pl.pallas_call(
        functools.partial(_s5_kernel, n_steps=n_steps, bsz=bsz, gb=gb),
        grid=(G // gb,),
        in_specs=[blk(rows, L * N), blk(*kb_cat.shape[1:]), blk(*kc_cat.shape[1:]),
                  blk(L * N, w_in.shape[2]), blk(w_out.shape[1], L * N),
                  blk(4, decay.shape[2]), blk(2, n_steps, powers.shape[3]), blk(1, L * N)],
        out_specs=blk(rows, L * N),
        out_shape=jax.ShapeDtypeStruct((G, rows, L * N), BF16),
        scratch_shapes=[pltpu.VMEM((gb, L * N, L * N), BF16)],
        compiler_params=_params(("parallel",), vmem),
        name=name,
    )(ug, kb_cat, kc_cat, w_in, w_out, decay, powers, dvec)
    yg = yg.reshape(G, n_steps, bsz, S5_SEGMENTS, L, N)
    return jnp.transpose(yg, (2, 3, 1, 4, 0, 5)).reshape(T, width)


def _s5_glu_kernel(y_ref, w_ref, b_ref, o_ref):
    g = jax.nn.gelu(y_ref[...].astype(F32), approximate=True)
    z = jnp.dot(g.astype(BF16), w_ref[...], preferred_element_type=F32) + b_ref[...]
    o_ref[...] = (g * jax.nn.sigmoid(z)).astype(o_ref.dtype)


def _s5_glu(y, w, b, *, tm=1024, name):
    M, W = y.shape
    w, layer = w
    tm = _pick_tile(M, tm, SUBLANES)
    vmem = 4 * tm * W * 2 + 2 * W * W * 2 + 4 * tm * W * 4 + (4 << 20)
    return pl.pallas_call(
        _s5_glu_kernel,
        grid=(M // tm,),
        in_specs=[pl.BlockSpec((tm, W), lambda i: (i, 0)),
                  pl.BlockSpec((None, W, W), lambda i: (layer, 0, 0)),
                  pl.BlockSpec((1, W), lambda i: (0, 0))],
        out_specs=pl.BlockSpec((tm, W), lambda i: (i, 0)),
        out_shape=jax.ShapeDtypeStruct((M, W), BF16),
        compiler_params=_params(("parallel",), vmem),
        name=name,
    )(y, w, b.reshape(1, W).astype(F32))


def _slab_start(i, tq, half, slab, sub_len):
    assert tq % half == 0 and (sub_len - slab) % half == 0
    return half * jnp.clip(i * (tq // half) - 1, 0, (sub_len - slab) // half)


def _dilated_kernel(q_ref, k_ref, v_ref, o_ref, lse_ref, *, sub_len, half, nh):
    i = pl.program_id(1)
    tq = q_ref.shape[0]
    slab = k_ref.shape[0]
    dh = q_ref.shape[1] // nh
    assert dh == LANES
    qpos = i * tq + lax.broadcasted_iota(jnp.int32, (tq, slab), 0)
    kpos = _slab_start(i, tq, half, slab, sub_len) + lax.broadcasted_iota(jnp.int32, (tq, slab), 1)
    valid = jnp.abs(kpos - qpos) <= half
    head_lane = lax.broadcasted_iota(jnp.int32, (tq, nh), 1)
    ones = jnp.ones((slab, LANES), BF16)
    lse_all = jnp.zeros((tq, nh), F32)
    for h in range(nh):
        sl = slice(h * dh, (h + 1) * dh)
        s = lax.dot_general(q_ref[:, sl], k_ref[:, sl], (((1,), (1,)), ((), ())), preferred_element_type=F32)
        s = jnp.where(valid, s, MASK_VALUE)
        m = jnp.max(s, axis=-1, keepdims=True)
        e = jnp.exp2(s - m).astype(BF16)
        ov = jnp.dot(e, jnp.concatenate([v_ref[:, sl], ones], axis=1),
                     preferred_element_type=F32)
        den = ov[:, dh:]
        o_ref[:, sl] = (ov[:, :dh] / den).astype(o_ref.dtype)
        lse_all = jnp.where(head_lane == h, m * LN_2 + jnp.log(den[:, :nh]), lse_all)
    lse_ref[...] = lse_all


def _dilated_branch(q, k, v, cols, D, n_seq, *, half, nh, name):
    T = q.shape[0]
    sub_len = T // n_seq
    tq = 2 * half
    slab = tq + 2 * half
    nq = sub_len // tq
    cq, ck, cv = cols

    assert sub_len >= slab

    def kv_spec(c):
        return pl.BlockSpec((pl.Element(slab), pl.Element(D)),
                            lambda s, i: (pl.multiple_of(s * sub_len + _slab_start(i, tq, half, slab, sub_len),
                                                         half), c * D))

    vmem = 2 * (2 * tq * D * 2 + 2 * slab * D * 2) + (16 << 20)
    return pl.pallas_call(
        functools.partial(_dilated_kernel, sub_len=sub_len, half=half, nh=nh),
        grid=(n_seq, nq),
        in_specs=[pl.BlockSpec((tq, D), lambda s, i: (s * nq + i, cq)), kv_spec(ck), kv_spec(cv)],
        out_specs=[pl.BlockSpec((tq, D), lambda s, i: (s * nq + i, 0)),
                   pl.BlockSpec((tq, nh), lambda s, i: (s * nq + i, 0))],
        out_shape=[jax.ShapeDtypeStruct((T, D), BF16), jax.ShapeDtypeStruct((T, nh), F32)],
        compiler_params=_params(("parallel", "parallel"), vmem),
        name=name,
    )(q, k, v)


def _residue_perm(dil):
    per = PERM_GROUP // dil
    rows = jnp.arange(PERM_GROUP)
    src = (rows % per) * dil + rows // per
    return (src[:, None] == rows[None, :]).astype(BF16)


def _to_residue_kernel(*refs, dils, n_in):
    n_d = len(dils)
    p_refs, x_refs, o_refs = refs[:n_d], refs[n_d:n_d + n_in], refs[n_d + n_in:]
    for xi, x_ref in enumerate(x_refs):
        for g in range(x_ref.shape[0] // PERM_GROUP):
            xg = x_ref[g * PERM_GROUP:(g + 1) * PERM_GROUP, :]
            for di, dil in enumerate(dils):
                o_ref = o_refs[di * n_in + xi]
                per = PERM_GROUP // dil
                y = jnp.dot(p_refs[di][...], xg, preferred_element_type=F32).astype(o_ref.dtype)
                for r in range(dil):
                    o_ref[r, g * per:(g + 1) * per, :] = y[r * per:(r + 1) * per]


def _to_residue(x, cols, width, dils, bsz, *, tb=512, name):
    T = x.shape[0]
    S = T // bsz
    tb = _pick_tile(S, tb, PERM_GROUP)
    nb = S // tb
    n_in = len(cols)
    perm_spec = pl.BlockSpec((PERM_GROUP, PERM_GROUP), lambda b, i: (0, 0))
    outs = pl.pallas_call(
        functools.partial(_to_residue_kernel, dils=tuple(dils), n_in=n_in),
        grid=(bsz, nb),
        in_specs=[perm_spec] * len(dils)
                 + [pl.BlockSpec((tb, width), lambda b, i, c=c: (b * nb + i, c)) for c in cols],
        out_specs=[pl.BlockSpec((None, dil, tb // dil, width), lambda b, i: (b, 0, i, 0))
                   for dil in dils for _ in cols],
        out_shape=[jax.ShapeDtypeStruct((bsz, dil, S // dil, width), x.dtype) for dil in dils for _ in cols],
        compiler_params=_params(("parallel", "parallel"),
                                (1 + len(dils)) * n_in * 2 * tb * width * 2 + (12 << 20)),
        name=name,
    )(*[_residue_perm(dil) for dil in dils], *([x] * n_in))
    outs = [o.reshape(T, width) for o in outs]
    return [outs[di * n_in:(di + 1) * n_in] for di in range(len(dils))]


def _from_residue_kernel(p_ref, x_ref, o_ref, *, dil):
    per = PERM_GROUP // dil
    for g in range(o_ref.shape[0] // PERM_GROUP):
        xg = jnp.concatenate([x_ref[r, g * per:(g + 1) * per, :] for r in range(dil)], axis=0)
        o_ref[g * PERM_GROUP:(g + 1) * PERM_GROUP, :] = jnp.dot(
            p_ref[...], xg, preferred_element_type=F32).astype(o_ref.dtype)


def _from_residue(x, dil, bsz, *, tb=1024, name):
    T, width = x.shape
    S = T // bsz
    tb = _pick_tile(S, tb, PERM_GROUP)
    nb = S // tb
    return pl.pallas_call(
        functools.partial(_from_residue_kernel, dil=dil),
        grid=(bsz, nb),
        in_specs=[pl.BlockSpec((PERM_GROUP, PERM_GROUP), lambda b, i: (0, 0)),
                  pl.BlockSpec((None, dil, tb // dil, width), lambda b, i: (b, 0, i, 0))],
        out_specs=pl.BlockSpec((tb, width), lambda b, i: (b * nb + i, 0)),
        out_shape=jax.ShapeDtypeStruct((T, width), x.dtype),
        compiler_params=_params(("parallel", "parallel"), 4 * tb * width * 2 + (8 << 20)),
        name=name,
    )(_residue_perm(dil).T, x.reshape(bsz, dil, S // dil, width))


def _merge_proj_kernel(o1, o2, o3, l1, l2, l3, w_ref, res_ref, out_ref, a_ref, *, nh):
    @pl.when(pl.program_id(1) == 0)
    def _():
        lse = [l1[...], l2[...], l3[...]]
        mx = jnp.maximum(jnp.maximum(lse[0], lse[1]), lse[2])
        e = [jnp.exp(x - mx) for x in lse]
        inv = 1.0 / (e[0] + e[1] + e[2])
        w1, w2 = e[0] * inv, e[1] * inv
        dh = a_ref.shape[1] // nh
        for h in range(nh):
            sl = slice(h * dh, (h + 1) * dh)
            b1, b2, b3 = o1[:, sl].astype(F32), o2[:, sl].astype(F32), o3[:, sl].astype(F32)
            a_ref[:, sl] = (b3 + w1[:, h:h + 1] * (b1 - b3) + w2[:, h:h + 1] * (b2 - b3)).astype(BF16)

    out_ref[...] = res_ref[...] + jnp.dot(a_ref[...], w_ref[...], preferred_element_type=F32)


def _merge_proj(outs, lses, w, res, *, nh, tm=512, tn=1024, name):
    M, D = outs[0].shape
    w, layer = w
    N = w.shape[2]
    tm = _pick_tile(M, tm, SUBLANES)
    tn = _pick_tile(N, tn, LANES)
    row = lambda width: pl.BlockSpec((tm, width), lambda i, j: (i, 0))
    vmem = 3 * 2 * tm * D * 2 + tm * D * 2 + 2 * D * tn * 2 + 5 * tm * tn * 4 + 4 * tm * D * 4 + (4 << 20)
    return pl.pallas_call(
        functools.partial(_merge_proj_kernel, nh=nh),
        grid=(M // tm, N // tn),
        in_specs=[row(D)] * 3 + [row(nh)] * 3
                 + [pl.BlockSpec((None, D, tn), lambda i, j: (layer, 0, j)),
                    pl.BlockSpec((tm, tn), lambda i, j: (i, j))],
        out_specs=pl.BlockSpec((tm, tn), lambda i, j: (i, j)),
        out_shape=jax.ShapeDtypeStruct((M, N), F32),
        scratch_shapes=[pltpu.VMEM((tm, D), BF16)],
        compiler_params=_params(("parallel", "arbitrary"), vmem),
        name=name,
    )(*outs, *lses, w, res)


def _final_norm_kernel(x_ref, g_ref, o_ref):
    o_ref[...] = _rms_normalize(x_ref[...], g_ref[...])


def _final_norm(x, g, *, tm=1024, name):
    M, D = x.shape
    tm = _pick_tile(M, tm, SUBLANES)
    return pl.pallas_call(
        _final_norm_kernel,
        grid=(M // tm,),
        in_specs=[pl.BlockSpec((tm, D), lambda i: (i, 0)), pl.BlockSpec((1, D), lambda i: (0, 0))],
        out_specs=pl.BlockSpec((tm, D), lambda i: (i, 0)),
        out_shape=jax.ShapeDtypeStruct((M, D), F32),
        compiler_params=_params(("parallel",), 6 * tm * D * 4 + (4 << 20)),
        name=name,
    )(x, g.reshape(1, D).astype(F32))


def _even_mixer(h, g, w_in, w_out, s5_tabs, glu_w, glu_b, lam_vec, subln_g, lambda_init, rope_tabs, bsz, tag):
    T, _ = h.shape
    S = T // bsz
    s5_width = glu_w[0].shape[1]
    width = (w_in[0].shape[2] - s5_width) // 3
    n_heads = width // (2 * HEAD_DIM)
    segs = ((s5_width, s5_width + width, True, HEAD_DIM ** -0.5 * LOG2_E),
            (s5_width + width, s5_width + 2 * width, True, 1.0))
    proj = _norm_matmul(h, g, w_in, segs=segs, rope_tabs=rope_tabs, seq=S, name=f"even_in_{tag}")
    y_s5 = _s5_mix(proj[:, :s5_width], s5_tabs, bsz, name=f"s5_{tag}")
    y_s5 = _s5_glu(y_s5, glu_w, glu_b, name=f"s5_glu_{tag}")
    y_diff = _diff_attn(proj, lam_vec, subln_g, bsz, s5_width=s5_width, n_heads=n_heads,
                        lambda_init=lambda_init, tk=1024, unroll=8, name=f"diff_attn_{tag}")
    return _matmul_res([y_s5, y_diff], w_out, h, tn=1024, name=f"even_out_{tag}")


def _odd_mixer(h, g, w_qkv, w_out, rope_tabs, bsz, tag):
    T, D = h.shape
    S = T // bsz
    nh = D // HEAD_DIM
    segs = ((0, D, True, HEAD_DIM ** -0.5 * LOG2_E), (D, 2 * D, True, 1.0))
    qkv = _norm_matmul(h, g, w_qkv, segs=segs, rope_tabs=rope_tabs, seq=S, name=f"odd_in_{tag}")
    outs, lses = [], []
    dils = [dil for _, dil in DILATED_BRANCHES if dil > 1]
    residue = dict(zip(dils, _to_residue(qkv, (0, 1, 2), D, dils, bsz, name=f"to_residue_{tag}")))
    for window, dil in DILATED_BRANCHES:
        half = window // (2 * dil)
        sub_len = S // dil

        if dil == 1:
            o, lse = _dilated_branch(qkv, qkv, qkv, (0, 1, 2), D, bsz, half=half, nh=nh,
                                     name=f"dilated_{dil}_{tag}")
        else:
            qr, kr, vr = residue[dil]
            o, lse = _dilated_branch(qr, kr, vr, (0, 0, 0), D, bsz * dil, half=half, nh=nh,
                                     name=f"dilated_{dil}_{tag}")
            o = _from_residue(o, dil, bsz, name=f"from_residue_{dil}_{tag}")
            lse = jnp.transpose(lse.reshape(bsz, dil, sub_len, nh), (0, 2, 1, 3)).reshape(T, nh)
        outs.append(o)
        lses.append(lse)
    return _merge_proj(outs, lses, w_out, h, nh=nh, name=f"odd_out_{tag}")


def kernel(x, mem, norm_mix_g, norm_xa_g, norm_mem_g, xa_wq, xa_wkv, xa_wo, norm_ffn_g, ffn_w13, ffn_w2, ab_w_in, ab_w_out, s5_lambda_re, s5_lambda_im, s5_log_step, s5_b_re, s5_b_im, s5_c_re, s5_c_im, s5_d, s5_glu_w, s5_glu_b, diff_lambda, diff_subln_g, c_w_qkv, c_w_out, final_norm_g):
    bsz, S, D = x.shape
    T = bsz * S
    depth = norm_mix_g.shape[0]
    mem_len = mem.shape[1]
    rope_tabs = _rope_tables(S)
    n_steps = S // (S5_CHUNK * S5_SEGMENTS)
    s5_width = s5_glu_w.shape[1]
    diff_width = (ab_w_in.shape[2] - s5_width) // 3
    ab_w_in = _rope_columns(ab_w_in, s5_width, s5_width + 2 * diff_width)
    c_w_qkv = _rope_columns(c_w_qkv, 0, 2 * D)
    ab_w_in, ab_w_out, s5_glu_w, c_w_qkv, c_w_out, xa_wq, xa_wkv, xa_wo, ffn_w13, ffn_w2 = (
        w.astype(BF16) for w in (ab_w_in, ab_w_out, s5_glu_w, c_w_qkv, c_w_out, xa_wq, xa_wkv, xa_wo,
                                 ffn_w13, ffn_w2))

    h = x.reshape(T, D)
    mem2 = mem.reshape(bsz * mem_len, D)
    for layer in range(depth):
        i = layer // 2
        if layer % 2 == 0:
            lambda_init = 0.8 - 0.6 * math.exp(-0.3 * layer)
            tabs = _s5_tables(s5_lambda_re[i], s5_lambda_im[i], s5_log_step[i], s5_b_re[i], s5_b_im[i],
                              s5_c_re[i], s5_c_im[i], s5_d[i], n_steps)
            h = _even_mixer(h, norm_mix_g[layer], (ab_w_in, i), (ab_w_out, i), tabs, (s5_glu_w, i),
                            s5_glu_b[i], diff_lambda[i], diff_subln_g[i], lambda_init, rope_tabs, bsz, layer)
        else:
            h = _odd_mixer(h, norm_mix_g[layer], (c_w_qkv, i), (c_w_out, i), rope_tabs, bsz, layer)

        q = _norm_matmul(h, norm_xa_g[layer], (xa_wq, layer),
                         segs=((0, D, False, (D // XA_HEADS) ** -0.5),), name=f"xa_q_{layer}")
        kv = _norm_matmul(mem2, norm_mem_g[layer], (xa_wkv, layer), name=f"xa_kv_{layer}")
        o = _cross_attn(q, kv, bsz, name=f"xa_core_{layer}")
        h = _matmul_res([o], (xa_wo, layer), h, tn=1024, name=f"xa_out_{layer}")

        hid = _swiglu_up(h, norm_ffn_g[layer], (ffn_w13, layer), name=f"ffn_up_{layer}")
        h = _matmul_res([hid], (ffn_w2, layer), h, tn=512, name=f"ffn_down_{layer}")
    return _final_norm(h, final_norm_g, name="final_norm").reshape(bsz, S, D)
```

```python
import functools
import math

import jax
import jax.numpy as jnp
from jax import lax
from jax.experimental import pallas as pl
from jax.experimental.pallas import tpu as pltpu

F32 = jnp.float32
BF16 = jnp.bfloat16

NORM_EPS = 1e-6
MASK_VALUE = -1e30
LOG2_E = math.log2(math.e)
LN_2 = math.log(2.0)
ROPE_THETA = 500000.0
ROPE_HALF = 16
HEAD_DIM = 128
XA_HEADS = 4
DILATED_BRANCHES = ((128, 1), (512, 4), (2048, 16))
S5_CHUNK = 32
S5_SEGMENTS = 8
PERM_GROUP = 256
LANES = 128
SUBLANES = 8
VMEM_CAP_BYTES = 56 * 1024 * 1024


def _pick_tile(n, pref, quantum):
    t = (min(n, pref) // quantum) * quantum
    while t >= quantum:
        if n % t == 0:
            return t
        t -= quantum
    return n


def _params(semantics, vmem_bytes):
    return pltpu.CompilerParams(
        dimension_semantics=semantics,
        vmem_limit_bytes=int(min(VMEM_CAP_BYTES, max(32 * 1024 * 1024, vmem_bytes))))


def _rms_normalize(x, g):
    ms = jnp.mean(x * x, axis=-1, keepdims=True)
    return (x * lax.rsqrt(ms + NORM_EPS)) * g


def _rope_lanes(y, cf, sf):
    outs = []
    for c in range(y.shape[1] // LANES):
        yc = y[:, c * LANES:(c + 1) * LANES]
        outs.append(yc * cf + pltpu.roll(yc, LANES // 2, 1) * sf)
    return outs[0] if len(outs) == 1 else jnp.concatenate(outs, axis=1)


def _cast_rope_columns_kernel(p_ref, w_ref, o_ref, *, lo_blk, hi_blk):
    j = pl.program_id(2)
    w = w_ref[...].astype(BF16)
    roped = jnp.logical_and(j >= lo_blk, j < hi_blk)

    @pl.when(roped)
    def _():
        o_ref[...] = jnp.dot(w, p_ref[...], preferred_element_type=F32).astype(BF16)

    @pl.when(jnp.logical_not(roped))
    def _():
        o_ref[...] = w


def _cast_rope_columns(w, lo, hi, *, tr=1024, tc=512, name):
    n_layers, K, N = w.shape
    tr = _pick_tile(K, tr, SUBLANES)
    tc = _pick_tile(math.gcd(N, lo, hi), tc, LANES)
    r, h = ROPE_HALF, LANES // 2
    src = jnp.concatenate([jnp.arange(0, r), jnp.arange(2 * r, h + r), jnp.arange(r, 2 * r),
                           jnp.arange(h + r, LANES)])
    perm = (jnp.arange(LANES)[:, None] == src[None, :]).astype(BF16)
    perm = jnp.kron(jnp.eye(tc // LANES, dtype=BF16), perm)
    return pl.pallas_call(
        functools.partial(_cast_rope_columns_kernel, lo_blk=lo // tc, hi_blk=hi // tc),
        grid=(n_layers, K // tr, N // tc),
        in_specs=[pl.BlockSpec((tc, tc), lambda l, i, j: (0, 0)),
                  pl.BlockSpec((None, tr, tc), lambda l, i, j: (l, i, j))],
        out_specs=pl.BlockSpec((None, tr, tc), lambda l, i, j: (l, i, j)),
        out_shape=jax.ShapeDtypeStruct(w.shape, BF16),
        compiler_params=_params(("parallel", "parallel", "parallel"), 8 * tr * tc * 4 + (4 << 20)),
        name=name,
    )(perm, w)


def _rope_tables(seq):
    pos = jnp.arange(seq, dtype=F32)
    inv = ROPE_THETA ** (-jnp.arange(0, 2 * ROPE_HALF, 2, dtype=F32) / (2 * ROPE_HALF))
    ang = pos[:, None] * inv[None, :]
    cos, sin = jnp.cos(ang), jnp.sin(ang)
    gap = LANES // 2 - ROPE_HALF
    cf = jnp.concatenate([cos, jnp.ones((seq, gap), F32), cos, jnp.ones((seq, gap), F32)], axis=1)
    sf = jnp.concatenate([-sin, jnp.zeros((seq, gap), F32), sin, jnp.zeros((seq, gap), F32)], axis=1)
    return cf, sf


def _norm_matmul_kernel(*refs, segs, tn, use_rope):
    if use_rope:
        x_ref, g_ref, w_ref, cf_ref, sf_ref, o_ref, xn_ref = refs
    else:
        x_ref, g_ref, w_ref, o_ref, xn_ref = refs
    j = pl.program_id(1)

    @pl.when(j == 0)
    def _():
        xn_ref[...] = _rms_normalize(x_ref[...], g_ref[...]).astype(BF16)

    y = jnp.dot(xn_ref[...], w_ref[...], preferred_element_type=F32)
    for lo, hi, rope, scale in segs:
        @pl.when(jnp.logical_and(j >= lo // tn, j < hi // tn))
        def _(rope=rope, scale=scale):
            z = y
            if rope:
                z = _rope_lanes(z, cf_ref[...], sf_ref[...])
            if scale != 1.0:
                z = z * scale
            o_ref[...] = z.astype(o_ref.dtype)


def _norm_matmul(x, g, w, *, segs=(), rope_tabs=None, seq=None, tm=1024, tn=1024, name):
    M, D = x.shape
    w, layer = w
    N = w.shape[2]
    tm = _pick_tile(M if seq is None else seq, tm, SUBLANES)
    tn = _pick_tile(math.gcd(N, *[b for s in segs for b in s[:2]]), tn, LANES)
    full, pos = [], 0
    for lo, hi, rope, scale in sorted(segs):
        if lo > pos:
            full.append((pos, lo, False, 1.0))
        full.append((lo, hi, rope, scale))
        pos = hi
    if pos < N:
        full.append((pos, N, False, 1.0))
    for lo, hi, _, _ in full:
        assert lo % tn == 0 and hi % tn == 0, (lo, hi, tn)
    use_rope = any(s[2] for s in full)
    in_specs = [pl.BlockSpec((tm, D), lambda i, j: (i, 0)),
                pl.BlockSpec((1, D), lambda i, j: (0, 0)),
                pl.BlockSpec((None, D, tn), lambda i, j: (layer, 0, j))]
    args = [x, g.reshape(1, D).astype(F32), w]
    if use_rope:
        nseq = seq // tm
        in_specs += [pl.BlockSpec((tm, LANES), lambda i, j: (i % nseq, 0))] * 2
        args += list(rope_tabs)
    vmem = 2 * tm * D * 4 + tm * D * 2 + 2 * D * tn * 2 + 2 * tm * tn * 2 + 3 * tm * tn * 4 + (4 << 20)
    return pl.pallas_call(
        functools.partial(_norm_matmul_kernel, segs=tuple(full), tn=tn, use_rope=use_rope),
        grid=(M // tm, N // tn),
        in_specs=in_specs,
        out_specs=pl.BlockSpec((tm, tn), lambda i, j: (i, j)),
        out_shape=jax.ShapeDtypeStruct((M, N), BF16),
        scratch_shapes=[pltpu.VMEM((tm, D), BF16)],
        compiler_params=_params(("parallel", "arbitrary"), vmem),
        name=name,
    )(*args)


def _matmul_res_kernel(*refs, n_a):
    a_refs, w_refs = refs[:n_a], refs[n_a:2 * n_a]
    res_ref, o_ref = refs[2 * n_a], refs[2 * n_a + 1]
    acc = res_ref[...]
    for a_ref, w_ref in zip(a_refs, w_refs):
        acc = acc + jnp.dot(a_ref[...], w_ref[...], preferred_element_type=F32)
    o_ref[...] = acc


def _matmul_res(a_list, w, res, *, tm=1024, tn=512, name):
    n_a = len(a_list)
    M, K = a_list[0].shape
    w, layer = w
    N = w.shape[2]
    tm = _pick_tile(M, tm, SUBLANES)
    tn = _pick_tile(N, tn, LANES)
    in_specs = [pl.BlockSpec((tm, K), lambda i, j: (i, 0)) for _ in range(n_a)]
    in_specs += [pl.BlockSpec((None, K, tn), lambda i, j, r=r: (layer, r, j)) for r in range(n_a)]
    in_specs += [pl.BlockSpec((tm, tn), lambda i, j: (i, j))]
    vmem = n_a * (2 * tm * K * 2 + 2 * K * tn * 2) + 5 * tm * tn * 4 + (4 << 20)
    return pl.pallas_call(
        functools.partial(_matmul_res_kernel, n_a=n_a),
        grid=(M // tm, N // tn),
        in_specs=in_specs,
        out_specs=pl.BlockSpec((tm, tn), lambda i, j: (i, j)),
        out_shape=jax.ShapeDtypeStruct((M, N), F32),
        compiler_params=_params(("parallel", "parallel"), vmem),
        name=name,
    )(*a_list, *([w] * n_a), res)


def _swiglu_up_kernel(x_ref, g_ref, w1_ref, w3_ref, o_ref, xn_ref):
    @pl.when(pl.program_id(1) == 0)
    def _():
        xn_ref[...] = _rms_normalize(x_ref[...], g_ref[...]).astype(BF16)

    xn = xn_ref[...]
    a = jnp.dot(xn, w1_ref[...], preferred_element_type=F32)
    b = jnp.dot(xn, w3_ref[...], preferred_element_type=F32)
    o_ref[...] = (a * jax.nn.sigmoid(a) * b).astype(o_ref.dtype)


def _swiglu_up(x, g, w13, *, tm=1024, tn=512, name):
    M, D = x.shape
    w13, layer = w13
    H = w13.shape[2] // 2
    tm = _pick_tile(M, tm, SUBLANES)
    tn = _pick_tile(H, tn, LANES)
    nj = H // tn
    vmem = 2 * tm * D * 4 + tm * D * 2 + 4 * D * tn * 2 + 2 * tm * tn * 2 + 4 * tm * tn * 4 + (4 << 20)
    return pl.pallas_call(
        _swiglu_up_kernel,
        grid=(M // tm, nj),
        in_specs=[pl.BlockSpec((tm, D), lambda i, j: (i, 0)),
                  pl.BlockSpec((1, D), lambda i, j: (0, 0)),
                  pl.BlockSpec((None, D, tn), lambda i, j: (layer, 0, j)),
                  pl.BlockSpec((None, D, tn), lambda i, j: (layer, 0, j + nj))],
        out_specs=pl.BlockSpec((tm, tn), lambda i, j: (i, j)),
        out_shape=jax.ShapeDtypeStruct((M, H), BF16),
        scratch_shapes=[pltpu.VMEM((tm, D), BF16)],
        compiler_params=_params(("parallel", "arbitrary"), vmem),
        name=name,
    )(x, g.reshape(1, D).astype(F32), w13, w13)


def _cross_attn_kernel(q_ref, k_ref, v_ref, o_ref, *, nh):
    dh = q_ref.shape[1] // nh
    for h in range(nh):
        sl = slice(h * dh, (h + 1) * dh)
        s = lax.dot_general(q_ref[:, sl], k_ref[:, sl], (((1,), (1,)), ((), ())), preferred_element_type=F32)
        m = jnp.max(s, axis=-1, keepdims=True)
        e = jnp.exp(s - m)
        den = jnp.sum(e, axis=-1, keepdims=True)
        o = jnp.dot(e.astype(BF16), v_ref[:, sl], preferred_element_type=F32)
        o_ref[:, sl] = (o / den).astype(o_ref.dtype)


def _cross_attn(q, kv, bsz, *, tq=1024, name):
    T, D = q.shape
    S = T // bsz
    mem_len = kv.shape[0] // bsz
    tq = _pick_tile(S, tq, SUBLANES)
    nq = S // tq
    kv3 = kv.reshape(bsz, mem_len, 2 * D)
    vmem = 4 * tq * D * 2 + 4 * mem_len * D * 2 + 4 * tq * mem_len * 4 + (8 << 20)
    return pl.pallas_call(
        functools.partial(_cross_attn_kernel, nh=XA_HEADS),
        grid=(bsz, nq),
        in_specs=[pl.BlockSpec((tq, D), lambda b, i: (b * nq + i, 0)),
                  pl.BlockSpec((None, mem_len, D), lambda b, i: (b, 0, 0)),
                  pl.BlockSpec((None, mem_len, D), lambda b, i: (b, 0, 1))],
        out_specs=pl.BlockSpec((tq, D), lambda b, i: (b * nq + i, 0)),
        out_shape=jax.ShapeDtypeStruct((T, D), BF16),
        compiler_params=_params(("parallel", "parallel"), vmem),
        name=name,
    )(q, kv3, kv3)


def _diff_attn_kernel(q_ref, k_ref, v_ref, lv_ref, sg_ref, o_ref, vt_ref, *, tk, unroll, lambda_init):
    tq = q_ref.shape[0]
    seq = k_ref.shape[0]
    dv = v_ref.shape[1]
    dh = dv // 2

    @pl.when(pl.program_id(2) == 0)
    def _():
        for c in range(seq // tk):
            vt_ref[:, c * tk:(c + 1) * tk] = v_ref[c * tk:(c + 1) * tk, :].astype(F32).T.astype(BF16)

    q = q_ref[...]
    nk = seq // tk

    def scores(kk):
        kc = k_ref[pl.ds(pl.multiple_of(kk * tk, tk), tk), :]
        return tuple(lax.dot_general(kc[:, c * dh:(c + 1) * dh], q[:, c * dh:(c + 1) * dh],
                                     (((1,), (1,)), ((), ())), preferred_element_type=F32)
                     for c in range(2))

    def body(kk, carry):
        st_next = scores(jnp.minimum(kk + 1, nk - 1))
        vt = vt_ref[:, pl.ds(pl.multiple_of(kk * tk, tk), tk)]
        new = []
        for c in range(2):
            st, m, l, acc = carry[c]
            m_new = jnp.maximum(m, jnp.max(st, axis=0, keepdims=True))
            alpha = jnp.exp2(m - m_new)
            pt = jnp.exp2(st - m_new)
            l = alpha * l + jnp.sum(pt, axis=0, keepdims=True)
            acc = alpha * acc + jnp.dot(vt, pt.astype(BF16), preferred_element_type=F32)
            new.append((st_next[c], m_new, l, acc))
        return tuple(new)

    st0 = scores(0)
    init = tuple((st0[c], jnp.full((1, tq), MASK_VALUE, F32), jnp.zeros((1, tq), F32),
                  jnp.zeros((dv, tq), F32)) for c in range(2))
    (_, _, l1, a1), (_, _, l2, a2) = lax.fori_loop(0, nk, body, init, unroll=unroll)
    lv = lv_ref[...]
    lam = (jnp.exp(jnp.sum(lv[0:1] * lv[1:2], axis=-1, keepdims=True))
           - jnp.exp(jnp.sum(lv[2:3] * lv[3:4], axis=-1, keepdims=True)) + lambda_init)
    ot = a1 / l1 - lam * (a2 / l2)
    ms = jnp.mean(ot * ot, axis=0, keepdims=True)
    ot = ot * lax.rsqrt(ms + NORM_EPS)
    o_ref[...] = (ot.T * sg_ref[...] * (1.0 - lambda_init)).astype(o_ref.dtype)


def _diff_attn(proj, lam_vec, subln_g, bsz, *, s5_width, n_heads, lambda_init, tq=512, tk=512, unroll=1, name):
    T, ncol = proj.shape
    S = T // bsz
    dv = 2 * HEAD_DIM
    width = n_heads * dv
    q0, k0, v0 = s5_width // dv, (s5_width + width) // dv, (s5_width + 2 * width) // dv
    tq = _pick_tile(S, tq, SUBLANES)
    tk = _pick_tile(S, tk, LANES)
    nq = S // tq
    proj3 = proj.reshape(bsz, S, ncol)
    vmem = 4 * tq * dv * 2 + 5 * S * dv * 2 + unroll * (3 * tq * tk * 4 + 2 * tq * dv * 4) + (8 << 20)
    return pl.pallas_call(
        functools.partial(_diff_attn_kernel, tk=tk, unroll=unroll, lambda_init=lambda_init),
        grid=(bsz, n_heads, nq),
        in_specs=[pl.BlockSpec((None, tq, dv), lambda b, h, i: (b, i, q0 + h)),
                  pl.BlockSpec((None, S, dv), lambda b, h, i: (b, 0, k0 + h)),
                  pl.BlockSpec((None, S, dv), lambda b, h, i: (b, 0, v0 + h)),
                  pl.BlockSpec((4, HEAD_DIM), lambda b, h, i: (0, 0)),
                  pl.BlockSpec((1, dv), lambda b, h, i: (0, 0))],
        out_specs=pl.BlockSpec((tq, dv), lambda b, h, i: (b * nq + i, h)),
        out_shape=jax.ShapeDtypeStruct((T, width), BF16),
        scratch_shapes=[pltpu.VMEM((dv, S), BF16)],
        compiler_params=_params(("parallel", "parallel", "arbitrary"), vmem),
        name=name,
    )(proj3, proj3, proj3, lam_vec.astype(F32), subln_g.reshape(1, dv).astype(F32))


def _s5_tables(lam_re, lam_im, log_step, b_re, b_im, c_re, c_im, d_skip, n_steps):
    L = S5_CHUNK
    _, G, P = lam_re.shape
    N = b_re.shape[-1]
    lr, li = lam_re.astype(F32), lam_im.astype(F32)
    step = jnp.exp(log_step.astype(F32))[..., None]
    ar, ai = lr * step, li * step

    def cexp(t, axes=()):
        a_r, a_i = jnp.expand_dims(ar, axes), jnp.expand_dims(ai, axes)
        mag = jnp.exp(t * a_r)
        return mag * jnp.cos(t * a_i), mag * jnp.sin(t * a_i)

    def cmul(xr, xi, yr, yi):
        return xr * yr - xi * yi, xr * yi + xi * yr

    er, ei = cexp(1.0)
    den = lr * lr + li * li
    qr, qi = ((er - 1.0) * lr + ei * li) / den, (ei * lr - (er - 1.0) * li) / den
    br, bi = cmul(qr[..., None], qi[..., None], b_re.astype(F32), b_im.astype(F32))
    cr, ci = c_re.astype(F32), c_im.astype(F32)
    tau = jnp.arange(L + 1, dtype=F32)[None, None, :, None]
    pr, pi = cexp(tau, axes=2)

    wr, wi = cmul(pr[:, :, :L, :, None], pi[:, :, :L, :, None], br[:, :, None], bi[:, :, None])
    kb_cat = jnp.transpose(jnp.concatenate([jnp.swapaxes(br, 2, 3), -jnp.swapaxes(bi, 2, 3)], axis=3),
                           (1, 0, 2, 3))
    pcr, pci = cmul(pr[:, :, :L, None, :], pi[:, :, :L, None, :], cr[:, :, None], ci[:, :, None])
    def _kc_rows(w):
        w = jnp.stack([w[0], w[1][:, ::-1]], axis=1)
        return jnp.transpose(w, (0, 1, 4, 2, 3)).reshape(G, 2, P, L * N)
    kc_cat = jnp.concatenate([_kc_rows(pcr), _kc_rows(pci)], axis=2)

    def _in_cols(w):
        return jnp.transpose(w, (0, 1, 3, 2)).reshape(G, L * N, P)
    w_in = jnp.concatenate([_in_cols(wr[0][:, ::-1]), _in_cols(wr[1]),
                            _in_cols(wi[0][:, ::-1]), _in_cols(wi[1])], axis=2)

    of_r, of_i = cmul(pr[0][:, 1:, None, :], pi[0][:, 1:, None, :], cr[0][:, None], ci[0][:, None])
    ob_r, ob_i = cmul(pr[1][:, ::-1][:, :L, None, :], pi[1][:, ::-1][:, :L, None, :],
                      cr[1][:, None], ci[1][:, None])
    def _out_rows(w):
        return jnp.transpose(w, (0, 3, 1, 2)).reshape(G, P, L * N)
    w_out = jnp.concatenate([_out_rows(of_r), _out_rows(ob_r), -_out_rows(of_i), -_out_rows(ob_i)], axis=1)

    def _lanes(z):
        return jnp.concatenate([z[0], z[1]], axis=-1)
    a_chunk, a_seg = cexp(float(L)), cexp(float(L * n_steps))
    decay = jnp.stack([_lanes(a_chunk[0]), _lanes(a_chunk[1]), _lanes(a_seg[0]), _lanes(a_seg[1])], axis=1)
    jj = float(L) * jnp.arange(n_steps, dtype=F32)
    jj = jnp.stack([jj, jj[::-1]], axis=0)[:, None, :, None]
    powers = jnp.stack([_lanes(p) for p in cexp(jj, axes=2)], axis=1)
    dvec = jnp.tile(d_skip.astype(F32).reshape(G, 1, N), (1, L, 1)).reshape(G, 1, L * N)
    return kb_cat, kc_cat, w_in.astype(BF16), w_out.astype(BF16), decay, powers, dvec


def _s5_kernel(u_ref, kb_ref, kc_ref, win_ref, wout_ref, decay_ref, pow_ref, dvec_ref, o_ref, toep_ref,
               *, n_steps, bsz, gb):
    rb = bsz * SUBLANES
    n_ch = kb_ref.shape[2]
    width = kc_ref.shape[3]
    chunk = width // n_ch
    lane = lax.broadcasted_iota(jnp.int32, (n_ch, width), 1)
    for gi in range(gb):
        kf = jnp.dot(kb_ref[gi, 0], kc_ref[gi, 0], preferred_element_type=F32, precision=lax.Precision.HIGHEST)
        kb = jnp.dot(kb_ref[gi, 1], kc_ref[gi, 1], preferred_element_type=F32, precision=lax.Precision.HIGHEST)
        for s in range(chunk):
            fwd = kf if s == 0 else jnp.where(lane >= s * n_ch, pltpu.roll(kf, s * n_ch, 1), 0.0)
            back = chunk - 1 - s
            bwd = kb if back == 0 else jnp.where(lane < (s + 1) * n_ch, pltpu.roll(kb, width - back * n_ch, 1), 0.0)
            toep_ref[gi, s * n_ch:(s + 1) * n_ch, :] = (fwd + bwd).astype(BF16)
        u = u_ref[gi]
        y = jnp.dot(u, toep_ref[gi], preferred_element_type=F32)
        z = jnp.dot(u, win_ref[gi], preferred_element_type=F32)
        half = z.shape[1] // 2
        zr, zi = z[:, :half], z[:, half:]
        dec = decay_ref[gi]
        ar, ai, sr, si = dec[0:1], dec[1:2], dec[2:3], dec[3:4]
        is_fwd = lax.broadcasted_iota(jnp.int32, (rb, half), 1) < half // 2

        xr = jnp.zeros((rb, half), F32)
        xi = jnp.zeros((rb, half), F32)
        ent_r, ent_i = [], []
        for j in range(n_steps):
            jb = n_steps - 1 - j
            ent_r.append(xr)
            ent_i.append(xi)
            zrj = jnp.where(is_fwd, zr[j * rb:(j + 1) * rb], zr[jb * rb:(jb + 1) * rb])
            zij = jnp.where(is_fwd, zi[j * rb:(j + 1) * rb], zi[jb * rb:(jb + 1) * rb])
            xr, xi = ar * xr - ai * xi + zrj, ar * xi + ai * xr + zij

        sub = lax.broadcasted_iota(jnp.int32, (SUBLANES, half), 0)
        fwd8 = lax.broadcasted_iota(jnp.int32, (SUBLANES, half), 1) < half // 2
        keep = (fwd8 & (sub > 0)) | (jnp.logical_not(fwd8) & (sub < SUBLANES - 1))
        car_r, car_i = [], []
        for b in range(bsz):
            er, ei = xr[b * SUBLANES:(b + 1) * SUBLANES], xi[b * SUBLANES:(b + 1) * SUBLANES]
            cr = jnp.zeros((SUBLANES, half), F32)
            ci = jnp.zeros((SUBLANES, half), F32)
            for _ in range(SUBLANES - 1):
                tr, ti = sr * cr - si * ci + er, sr * ci + si * cr + ei
                cr = jnp.where(keep, jnp.where(fwd8, pltpu.roll(tr, 1, 0), pltpu.roll(tr, SUBLANES - 1, 0)), 0.0)
                ci = jnp.where(keep, jnp.where(fwd8, pltpu.roll(ti, 1, 0), pltpu.roll(ti, SUBLANES - 1, 0)), 0.0)
            car_r.append(cr)
            car_i.append(ci)
        gr = car_r[0] if bsz == 1 else jnp.concatenate(car_r, axis=0)
        gi_ = car_i[0] if bsz == 1 else jnp.concatenate(car_i, axis=0)

        pr_all, pi_all = pow_ref[gi, 0], pow_ref[gi, 1]
        rows_r, rows_i = [], []
        for j in range(n_steps):
            jb = n_steps - 1 - j
            lr = jnp.where(is_fwd, ent_r[j], ent_r[jb])
            li = jnp.where(is_fwd, ent_i[j], ent_i[jb])
            pr, pi = pr_all[j:j + 1], pi_all[j:j + 1]
            rows_r.append(lr + pr * gr - pi * gi_)
            rows_i.append(li + pr * gi_ + pi * gr)
        state = jnp.concatenate([jnp.concatenate(rows_r, axis=0), jnp.concatenate(rows_i, axis=0)], axis=1)
        y = y + jnp.dot(state.astype(BF16), wout_ref[gi], preferred_element_type=F32)
        y = y + u.astype(F32) * dvec_ref[gi]
        o_ref[gi] = y.astype(o_ref.dtype)


def _s5_mix(u, tables, bsz, *, gb=4, name):
    kb_cat, kc_cat, w_in, w_out, decay, powers, dvec = tables
    T, width = u.shape
    S = T // bsz
    G = w_in.shape[0]
    N = width // G
    L = S5_CHUNK
    n_steps = S // (L * S5_SEGMENTS)
    rows = n_steps * bsz * S5_SEGMENTS
    gb = _pick_tile(G, gb, 1)
    ug = u.reshape(bsz, S5_SEGMENTS, n_steps, L, G, N)
    ug = jnp.transpose(ug, (4, 2, 0, 1, 3, 5)).reshape(G, rows, L * N)
    blk = lambda *shape: pl.BlockSpec((gb,) + shape, lambda g: (g,) + (0,) * len(shape))
    vmem = gb * (4 * rows * L * N * 2 + (L * N) ** 2 * 2 + 4 * kc_cat.shape[2] * L * N * 4
                 + 8 * L * N * w_in.shape[2] * 2) + (24 << 20)
    yg = pl.pallas_call(
        functools.partial(_s5_kernel, n_steps=n_steps, bsz=bsz, gb=gb),
        grid=(G // gb,),
        in_specs=[blk(rows, L * N), blk(*kb_cat.shape[1:]), blk(*kc_cat.shape[1:]),
                  blk(L * N, w_in.shape[2]), blk(w_out.shape[1], L * N),
                  blk(4, decay.shape[2]), blk(2, n_steps, powers.shape[3]), blk(1, L * N)],
        out_specs=blk(rows, L * N),
        out_shape=jax.ShapeDtypeStruct((G, rows, L * N), BF16),
        scratch_shapes=[pltpu.VMEM((gb, L * N, L * N), BF16)],
        compiler_params=_params(("parallel",), vmem),
        name=name,
    )(ug, kb_cat, kc_cat, w_in, w_out, decay, powers, dvec)
    yg = yg.reshape(G, n_steps, bsz, S5_SEGMENTS, L, N)
    return jnp.transpose(yg, (2, 3, 1, 4, 0, 5)).reshape(T, width)


def _s5_glu_kernel(y_ref, w_ref, b_ref, o_ref):
    g = jax.nn.gelu(y_ref[...].astype(F32), approximate=True)
    z = jnp.dot(g.astype(BF16), w_ref[...], preferred_element_type=F32) + b_ref[...]
    o_ref[...] = (g * jax.nn.sigmoid(z)).astype(o_ref.dtype)


def _s5_glu(y, w, b, *, tm=1024, name):
    M, W = y.shape
    w, layer = w
    tm = _pick_tile(M, tm, SUBLANES)
    vmem = 4 * tm * W * 2 + 2 * W * W * 2 + 4 * tm * W * 4 + (4 << 20)
    return pl.pallas_call(
        _s5_glu_kernel,
        grid=(M // tm,),
        in_specs=[pl.BlockSpec((tm, W), lambda i: (i, 0)),
                  pl.BlockSpec((None, W, W), lambda i: (layer, 0, 0)),
                  pl.BlockSpec((1, W), lambda i: (0, 0))],
        out_specs=pl.BlockSpec((tm, W), lambda i: (i, 0)),
        out_shape=jax.ShapeDtypeStruct((M, W), BF16),
        compiler_params=_params(("parallel",), vmem),
        name=name,
    )(y, w, b.reshape(1, W).astype(F32))


def _slab_start(i, tq, half, slab, sub_len):
    assert tq % half == 0 and (sub_len - slab) % half == 0
    return half * jnp.clip(i * (tq // half) - 1, 0, (sub_len - slab) // half)


def _dilated_kernel(q_ref, k_ref, v_ref, o_ref, lse_ref, *, sub_len, half, nh):
    i = pl.program_id(1)
    tq = q_ref.shape[0]
    slab = k_ref.shape[0]
    dh = q_ref.shape[1] // nh
    assert dh == LANES
    sq = 2 * half
    win = sq + 2 * half
    start = _slab_start(i, tq, half, slab, sub_len)
    head_lane = lax.broadcasted_iota(jnp.int32, (sq, nh), 1)
    ones = jnp.ones((win, LANES), BF16)
    for j in range(tq // sq):
        off = pl.multiple_of(jnp.clip(i * tq + j * sq - half - start, 0, slab - win), half)
        qpos = i * tq + j * sq + lax.broadcasted_iota(jnp.int32, (sq, win), 0)
        kpos = start + off + lax.broadcasted_iota(jnp.int32, (sq, win), 1)
        valid = jnp.abs(kpos - qpos) <= half
        rows = slice(j * sq, (j + 1) * sq)
        lse_all = jnp.zeros((sq, nh), F32)
        for h in range(nh):
            sl = slice(h * dh, (h + 1) * dh)
            s = lax.dot_general(q_ref[rows, sl], k_ref[pl.ds(off, win), sl], (((1,), (1,)), ((), ())),
                                preferred_element_type=F32)
            s = jnp.where(valid, s, MASK_VALUE)
            m = jnp.max(s, axis=-1, keepdims=True)
            e = jnp.exp2(s - m).astype(BF16)
            ov = jnp.dot(e, jnp.concatenate([v_ref[pl.ds(off, win), sl], ones], axis=1),
                         preferred_element_type=F32)
            den = ov[:, dh:]
            o_ref[rows, sl] = (ov[:, :dh] / den).astype(o_ref.dtype)
            lse_all = jnp.where(head_lane == h, m * LN_2 + jnp.log(den[:, :nh]), lse_all)
        lse_ref[rows, :] = lse_all


def _dilated_branch(q, k, v, cols, D, n_seq, *, half, nh, name):
    T = q.shape[0]
    sub_len = T // n_seq
    sq = 2 * half
    n_sub = max(n for n in (4, 2, 1) if sub_len % (n * sq) == 0 and n * sq + 2 * half <= sub_len)
    tq = n_sub * sq
    slab = tq + 2 * half
    nq = sub_len // tq
    cq, ck, cv = cols

    def kv_spec(c):
        return pl.BlockSpec((pl.Element(slab), pl.Element(D)),
                            lambda s, i: (pl.multiple_of(s * sub_len + _slab_start(i, tq, half, slab, sub_len),
                                                         half), c * D))

    vmem = 2 * (2 * tq * D * 2 + 2 * slab * D * 2) + (16 << 20)
    return pl.pallas_call(
        functools.partial(_dilated_kernel, sub_len=sub_len, half=half, nh=nh),
        grid=(n_seq, nq),
        in_specs=[pl.BlockSpec((tq, D), lambda s, i: (s * nq + i, cq)), kv_spec(ck), kv_spec(cv)],
        out_specs=[pl.BlockSpec((tq, D), lambda s, i: (s * nq + i, 0)),
                   pl.BlockSpec((tq, nh), lambda s, i: (s * nq + i, 0))],
        out_shape=[jax.ShapeDtypeStruct((T, D), BF16), jax.ShapeDtypeStruct((T, nh), F32)],
        compiler_params=_params(("parallel", "parallel"), vmem),
        name=name,
    )(q, k, v)


def _residue_perm(dil):
    per = PERM_GROUP // dil
    rows = jnp.arange(PERM_GROUP)
    src = (rows % per) * dil + rows // per
    return (src[:, None] == rows[None, :]).astype(BF16)


def _to_residue_kernel(*refs, dils, n_in):
    n_d = len(dils)
    p_refs, x_refs, o_refs = refs[:n_d], refs[n_d:n_d + n_in], refs[n_d + n_in:]
    for xi, x_ref in enumerate(x_refs):
        for g in range(x_ref.shape[0] // PERM_GROUP):
            xg = x_ref[g * PERM_GROUP:(g + 1) * PERM_GROUP, :]
            for di, dil in enumerate(dils):
                o_ref = o_refs[di * n_in + xi]
                per = PERM_GROUP // dil
                y = jnp.dot(p_refs[di][...], xg, preferred_element_type=F32).astype(o_ref.dtype)
                for r in range(dil):
                    o_ref[r, g * per:(g + 1) * per, :] = y[r * per:(r + 1) * per]


def _to_residue(x, cols, width, dils, bsz, *, tb=512, name):
    T = x.shape[0]
    S = T // bsz
    tb = _pick_tile(S, tb, PERM_GROUP)
    nb = S // tb
    n_in = len(cols)
    perm_spec = pl.BlockSpec((PERM_GROUP, PERM_GROUP), lambda b, i: (0, 0))
    outs = pl.pallas_call(
        functools.partial(_to_residue_kernel, dils=tuple(dils), n_in=n_in),
        grid=(bsz, nb),
        in_specs=[perm_spec] * len(dils)
                 + [pl.BlockSpec((tb, width), lambda b, i, c=c: (b * nb + i, c)) for c in cols],
        out_specs=[pl.BlockSpec((None, dil, tb // dil, width), lambda b, i: (b, 0, i, 0))
                   for dil in dils for _ in cols],
        out_shape=[jax.ShapeDtypeStruct((bsz, dil, S // dil, width), x.dtype) for dil in dils for _ in cols],
        compiler_params=_params(("parallel", "parallel"),
                                (1 + len(dils)) * n_in * 2 * tb * width * 2 + (12 << 20)),
        name=name,
    )(*[_residue_perm(dil) for dil in dils], *([x] * n_in))
    outs = [o.reshape(T, width) for o in outs]
    return [outs[di * n_in:(di + 1) * n_in] for di in range(len(dils))]


def _from_residue_kernel(p_ref, x_ref, o_ref, *, dil):
    per = PERM_GROUP // dil
    for g in range(o_ref.shape[0] // PERM_GROUP):
        xg = jnp.concatenate([x_ref[r, g * per:(g + 1) * per, :] for r in range(dil)], axis=0)
        o_ref[g * PERM_GROUP:(g + 1) * PERM_GROUP, :] = jnp.dot(
            p_ref[...], xg, preferred_element_type=F32).astype(o_ref.dtype)


def _from_residue(x, dil, bsz, *, tb=1024, name):
    T, width = x.shape
    S = T // bsz
    tb = _pick_tile(S, tb, PERM_GROUP)
    nb = S // tb
    return pl.pallas_call(
        functools.partial(_from_residue_kernel, dil=dil),
        grid=(bsz, nb),
        in_specs=[pl.BlockSpec((PERM_GROUP, PERM_GROUP), lambda b, i: (0, 0)),
                  pl.BlockSpec((None, dil, tb // dil, width), lambda b, i: (b, 0, i, 0))],
        out_specs=pl.BlockSpec((tb, width), lambda b, i: (b * nb + i, 0)),
        out_shape=jax.ShapeDtypeStruct((T, width), x.dtype),
        compiler_params=_params(("parallel", "parallel"), 4 * tb * width * 2 + (8 << 20)),
        name=name,
    )(_residue_perm(dil).T, x.reshape(bsz, dil, S // dil, width))


def _merge_proj_kernel(o1, o2, o3, l1, l2, l3, w_ref, res_ref, out_ref, a_ref, *, nh):
    @pl.when(pl.program_id(1) == 0)
    def _():
        lse = [l1[...], l2[...], l3[...]]
        mx = jnp.maximum(jnp.maximum(lse[0], lse[1]), lse[2])
        e = [jnp.exp(x - mx) for x in lse]
        inv = 1.0 / (e[0] + e[1] + e[2])
        w1, w2 = e[0] * inv, e[1] * inv
        dh = a_ref.shape[1] // nh
        for h in range(nh):
            sl = slice(h * dh, (h + 1) * dh)
            b1, b2, b3 = o1[:, sl].astype(F32), o2[:, sl].astype(F32), o3[:, sl].astype(F32)
            a_ref[:, sl] = (b3 + w1[:, h:h + 1] * (b1 - b3) + w2[:, h:h + 1] * (b2 - b3)).astype(BF16)

    out_ref[...] = res_ref[...] + jnp.dot(a_ref[...], w_ref[...], preferred_element_type=F32)


def _merge_proj(outs, lses, w, res, *, nh, tm=512, tn=1024, name):
    M, D = outs[0].shape
    w, layer = w
    N = w.shape[2]
    tm = _pick_tile(M, tm, SUBLANES)
    tn = _pick_tile(N, tn, LANES)
    row = lambda width: pl.BlockSpec((tm, width), lambda i, j: (i, 0))
    vmem = 3 * 2 * tm * D * 2 + tm * D * 2 + 2 * D * tn * 2 + 5 * tm * tn * 4 + 4 * tm * D * 4 + (4 << 20)
    return pl.pallas_call(
        functools.partial(_merge_proj_kernel, nh=nh),
        grid=(M // tm, N // tn),
        in_specs=[row(D)] * 3 + [row(nh)] * 3
                 + [pl.BlockSpec((None, D, tn), lambda i, j: (layer, 0, j)),
                    pl.BlockSpec((tm, tn), lambda i, j: (i, j))],
        out_specs=pl.BlockSpec((tm, tn), lambda i, j: (i, j)),
        out_shape=jax.ShapeDtypeStruct((M, N), F32),
        scratch_shapes=[pltpu.VMEM((tm, D), BF16)],
        compiler_params=_params(("parallel", "arbitrary"), vmem),
        name=name,
    )(*outs, *lses, w, res)


def _final_norm_kernel(x_ref, g_ref, o_ref):
    o_ref[...] = _rms_normalize(x_ref[...], g_ref[...])


def _final_norm(x, g, *, tm=1024, name):
    M, D = x.shape
    tm = _pick_tile(M, tm, SUBLANES)
    return pl.pallas_call(
        _final_norm_kernel,
        grid=(M // tm,),
        in_specs=[pl.BlockSpec((tm, D), lambda i: (i, 0)), pl.BlockSpec((1, D), lambda i: (0, 0))],
        out_specs=pl.BlockSpec((tm, D), lambda i: (i, 0)),
        out_shape=jax.ShapeDtypeStruct((M, D), F32),
        compiler_params=_params(("parallel",), 6 * tm * D * 4 + (4 << 20)),
        name=name,
    )(x, g.reshape(1, D).astype(F32))


def _even_mixer(h, g, w_in, w_out, s5_tabs, glu_w, glu_b, lam_vec, subln_g, lambda_init, rope_tabs, bsz, tag):
    T, _ = h.shape
    S = T // bsz
    s5_width = glu_w[0].shape[1]
    width = (w_in[0].shape[2] - s5_width) // 3
    n_heads = width // (2 * HEAD_DIM)
    segs = ((s5_width, s5_width + width, True, HEAD_DIM ** -0.5 * LOG2_E),
            (s5_width + width, s5_width + 2 * width, True, 1.0))
    proj = _norm_matmul(h, g, w_in, segs=segs, rope_tabs=rope_tabs, seq=S, name=f"even_in_{tag}")
    y_s5 = _s5_mix(proj[:, :s5_width], s5_tabs, bsz, name=f"s5_{tag}")
    y_s5 = _s5_glu(y_s5, glu_w, glu_b, name=f"s5_glu_{tag}")
    y_diff = _diff_attn(proj, lam_vec, subln_g, bsz, s5_width=s5_width, n_heads=n_heads,
                        lambda_init=lambda_init, tk=1024, unroll=8, name=f"diff_attn_{tag}")
    return _matmul_res([y_s5, y_diff], w_out, h, tn=1024, name=f"even_out_{tag}")


def _odd_mixer(h, g, w_qkv, w_out, rope_tabs, bsz, tag):
    T, D = h.shape
    S = T // bsz
    nh = D // HEAD_DIM
    segs = ((0, D, True, HEAD_DIM ** -0.5 * LOG2_E), (D, 2 * D, True, 1.0))
    qkv = _norm_matmul(h, g, w_qkv, segs=segs, rope_tabs=rope_tabs, seq=S, name=f"odd_in_{tag}")
    outs, lses = [], []
    dils = [dil for _, dil in DILATED_BRANCHES if dil > 1]
    residue = dict(zip(dils, _to_residue(qkv, (0, 1, 2), D, dils, bsz, name=f"to_residue_{tag}")))
    for window, dil in DILATED_BRANCHES:
        half = window // (2 * dil)
        sub_len = S // dil

        if dil == 1:
            o, lse = _dilated_branch(qkv, qkv, qkv, (0, 1, 2), D, bsz, half=half, nh=nh,
                                     name=f"dilated_{dil}_{tag}")
        else:
            qr, kr, vr = residue[dil]
            o, lse = _dilated_branch(qr, kr, vr, (0, 0, 0), D, bsz * dil, half=half, nh=nh,
                                     name=f"dilated_{dil}_{tag}")
            o = _from_residue(o, dil, bsz, name=f"from_residue_{dil}_{tag}")
            lse = jnp.transpose(lse.reshape(bsz, dil, sub_len, nh), (0, 2, 1, 3)).reshape(T, nh)
        outs.append(o)
        lses.append(lse)
    return _merge_proj(outs, lses, w_out, h, nh=nh, name=f"odd_out_{tag}")


def kernel(x, mem, norm_mix_g, norm_xa_g, norm_mem_g, xa_wq, xa_wkv, xa_wo, norm_ffn_g, ffn_w13, ffn_w2, ab_w_in, ab_w_out, s5_lambda_re, s5_lambda_im, s5_log_step, s5_b_re, s5_b_im, s5_c_re, s5_c_im, s5_d, s5_glu_w, s5_glu_b, diff_lambda, diff_subln_g, c_w_qkv, c_w_out, final_norm_g):
    bsz, S, D = x.shape
    T = bsz * S
    depth = norm_mix_g.shape[0]
    mem_len = mem.shape[1]
    rope_tabs = _rope_tables(S)
    n_steps = S // (S5_CHUNK * S5_SEGMENTS)
    s5_width = s5_glu_w.shape[1]
    diff_width = (ab_w_in.shape[2] - s5_width) // 3
    ab_w_in = _cast_rope_columns(ab_w_in, s5_width, s5_width + 2 * diff_width, name="cast_even_in")
    c_w_qkv = _cast_rope_columns(c_w_qkv, 0, 2 * D, name="cast_odd_in")
    ab_w_out, s5_glu_w, c_w_out, xa_wq, xa_wkv, xa_wo, ffn_w13, ffn_w2 = (
        w.astype(BF16) for w in (ab_w_out, s5_glu_w, c_w_out, xa_wq, xa_wkv, xa_wo, ffn_w13, ffn_w2))

    h = x.reshape(T, D)
    mem2 = mem.reshape(bsz * mem_len, D)
    for layer in range(depth):
        i = layer // 2
        if layer % 2 == 0:
            lambda_init = 0.8 - 0.6 * math.exp(-0.3 * layer)
            tabs = _s5_tables(s5_lambda_re[i], s5_lambda_im[i], s5_log_step[i], s5_b_re[i], s5_b_im[i],
                              s5_c_re[i], s5_c_im[i], s5_d[i], n_steps)
            h = _even_mixer(h, norm_mix_g[layer], (ab_w_in, i), (ab_w_out, i), tabs, (s5_glu_w, i),
                            s5_glu_b[i], diff_lambda[i], diff_subln_g[i], lambda_init, rope_tabs, bsz, layer)
        else:
            h = _odd_mixer(h, norm_mix_g[layer], (c_w_qkv, i), (c_w_out, i), rope_tabs, bsz, layer)

        q = _norm_matmul(h, norm_xa_g[layer], (xa_wq, layer),
                         segs=((0, D, False, (D // XA_HEADS) ** -0.5),), name=f"xa_q_{layer}")
        kv = _norm_matmul(mem2, norm_mem_g[layer], (xa_wkv, layer), name=f"xa_kv_{layer}")
        o = _cross_attn(q, kv, bsz, name=f"xa_core_{layer}")
        h = _matmul_res([o], (xa_wo, layer), h, tn=1024, name=f"xa_out_{layer}")

        hid = _swiglu_up(h, norm_ffn_g[layer], (ffn_w13, layer), name=f"ffn_up_{layer}")
        h = _matmul_res([hid], (ffn_w2, layer), h, tn=512, name=f"ffn_down_{layer}")
    return _final_norm(h, final_norm_g, name="final_norm").reshape(bsz, S, D)
```

```python
import functools
import math

import jax
import jax.numpy as jnp
from jax import lax
from jax.experimental import pallas as pl
from jax.experimental.pallas import tpu as pltpu

F32 = jnp.float32
BF16 = jnp.bfloat16

NORM_EPS = 1e-6
MASK_VALUE = -1e30
LOG2_E = math.log2(math.e)
LN_2 = math.log(2.0)
ROPE_THETA = 500000.0
ROPE_HALF = 16
HEAD_DIM = 128
XA_HEADS = 4
DILATED_BRANCHES = ((128, 1), (512, 4), (2048, 16))
S5_CHUNK = 32
S5_SEGMENTS = 8
PERM_GROUP = 256
LANES = 128
SUBLANES = 8
VMEM_CAP_BYTES = 56 * 1024 * 1024


def _pick_tile(n, pref, quantum):
    t = (min(n, pref) // quantum) * quantum
    while t >= quantum:
        if n % t == 0:
            return t
        t -= quantum
    return n


def _params(semantics, vmem_bytes):
    return pltpu.CompilerParams(
        dimension_semantics=semantics,
        vmem_limit_bytes=int(min(VMEM_CAP_BYTES, max(32 * 1024 * 1024, vmem_bytes))))


def _rms_normalize(x, g):
    ms = jnp.mean(x * x, axis=-1, keepdims=True)
    return (x * lax.rsqrt(ms + NORM_EPS)) * g


def _rope_lanes(y, cf, sf):
    outs = []
    for c in range(y.shape[1] // LANES):
        yc = y[:, c * LANES:(c + 1) * LANES]
        outs.append(yc * cf + pltpu.roll(yc, LANES // 2, 1) * sf)
    return outs[0] if len(outs) == 1 else jnp.concatenate(outs, axis=1)


def _cast_rope_columns_kernel(p_ref, w_ref, o_ref, *, lo_blk, hi_blk):
    j = pl.program_id(2)
    w = w_ref[...].astype(BF16)
    roped = jnp.logical_and(j >= lo_blk, j < hi_blk)

    @pl.when(roped)
    def _():
        o_ref[...] = jnp.dot(w, p_ref[...], preferred_element_type=F32).astype(BF16)

    @pl.when(jnp.logical_not(roped))
    def _():
        o_ref[...] = w


def _cast_rope_columns(w, lo, hi, *, tr=1024, tc=512, name):
    n_layers, K, N = w.shape
    tr = _pick_tile(K, tr, SUBLANES)
    tc = _pick_tile(math.gcd(N, lo, hi), tc, LANES)
    r, h = ROPE_HALF, LANES // 2
    src = jnp.concatenate([jnp.arange(0, r), jnp.arange(2 * r, h + r), jnp.arange(r, 2 * r),
                           jnp.arange(h + r, LANES)])
    perm = (jnp.arange(LANES)[:, None] == src[None, :]).astype(BF16)
    perm = jnp.kron(jnp.eye(tc // LANES, dtype=BF16), perm)
    return pl.pallas_call(
        functools.partial(_cast_rope_columns_kernel, lo_blk=lo // tc, hi_blk=hi // tc),
        grid=(n_layers, K // tr, N // tc),
        in_specs=[pl.BlockSpec((tc, tc), lambda l, i, j: (0, 0)),
                  pl.BlockSpec((None, tr, tc), lambda l, i, j: (l, i, j))],
        out_specs=pl.BlockSpec((None, tr, tc), lambda l, i, j: (l, i, j)),
        out_shape=jax.ShapeDtypeStruct(w.shape, BF16),
        compiler_params=_params(("parallel", "parallel", "parallel"), 8 * tr * tc * 4 + (4 << 20)),
        name=name,
    )(perm, w)


def _rope_tables(seq):
    pos = jnp.arange(seq, dtype=F32)
    inv = ROPE_THETA ** (-jnp.arange(0, 2 * ROPE_HALF, 2, dtype=F32) / (2 * ROPE_HALF))
    ang = pos[:, None] * inv[None, :]
    cos, sin = jnp.cos(ang), jnp.sin(ang)
    gap = LANES // 2 - ROPE_HALF
    cf = jnp.concatenate([cos, jnp.ones((seq, gap), F32), cos, jnp.ones((seq, gap), F32)], axis=1)
    sf = jnp.concatenate([-sin, jnp.zeros((seq, gap), F32), sin, jnp.zeros((seq, gap), F32)], axis=1)
    return cf, sf


def _norm_matmul_kernel(*refs, segs, tn, use_rope):
    if use_rope:
        x_ref, g_ref, w_ref, cf_ref, sf_ref, o_ref, xn_ref = refs
    else:
        x_ref, g_ref, w_ref, o_ref, xn_ref = refs
    j = pl.program_id(1)

    @pl.when(j == 0)
    def _():
        xn_ref[...] = _rms_normalize(x_ref[...], g_ref[...]).astype(BF16)

    y = jnp.dot(xn_ref[...], w_ref[...], preferred_element_type=F32)
    for lo, hi, rope, scale in segs:
        @pl.when(jnp.logical_and(j >= lo // tn, j < hi // tn))
        def _(rope=rope, scale=scale):
            z = y
            if rope:
                cf, sf = cf_ref[...], sf_ref[...]
                if scale != 1.0:
                    cf, sf = cf * scale, sf * scale
                z = _rope_lanes(z, cf, sf)
            elif scale != 1.0:
                z = z * scale
            o_ref[...] = z.astype(o_ref.dtype)


def _norm_matmul(x, g, w, *, segs=(), rope_tabs=None, seq=None, tm=1024, tn=1024, name):
    M, D = x.shape
    w, layer = w
    N = w.shape[2]
    tm = _pick_tile(M if seq is None else seq, tm, SUBLANES)
    tn = _pick_tile(math.gcd(N, *[b for s in segs for b in s[:2]]), tn, LANES)
    full, pos = [], 0
    for lo, hi, rope, scale in sorted(segs):
        if lo > pos:
            full.append((pos, lo, False, 1.0))
        full.append((lo, hi, rope, scale))
        pos = hi
    if pos < N:
        full.append((pos, N, False, 1.0))
    for lo, hi, _, _ in full:
        assert lo % tn == 0 and hi % tn == 0, (lo, hi, tn)
    use_rope = any(s[2] for s in full)
    in_specs = [pl.BlockSpec((tm, D), lambda i, j: (i, 0)),
                pl.BlockSpec((1, D), lambda i, j: (0, 0)),
                pl.BlockSpec((None, D, tn), lambda i, j: (layer, 0, j))]
    args = [x, g.reshape(1, D).astype(F32), w]
    if use_rope:
        nseq = seq // tm
        in_specs += [pl.BlockSpec((tm, LANES), lambda i, j: (i % nseq, 0))] * 2
        args += list(rope_tabs)
    vmem = 2 * tm * D * 4 + tm * D * 2 + 2 * D * tn * 2 + 2 * tm * tn * 2 + 3 * tm * tn * 4 + (4 << 20)
    return pl.pallas_call(
        functools.partial(_norm_matmul_kernel, segs=tuple(full), tn=tn, use_rope=use_rope),
        grid=(M // tm, N // tn),
        in_specs=in_specs,
        out_specs=pl.BlockSpec((tm, tn), lambda i, j: (i, j)),
        out_shape=jax.ShapeDtypeStruct((M, N), BF16),
        scratch_shapes=[pltpu.VMEM((tm, D), BF16)],
        compiler_params=_params(("parallel", "arbitrary"), vmem),
        name=name,
    )(*args)


def _matmul_res_kernel(*refs, n_a):
    a_refs, w_refs = refs[:n_a], refs[n_a:2 * n_a]
    res_ref, o_ref = refs[2 * n_a], refs[2 * n_a + 1]
    acc = res_ref[...]
    for a_ref, w_ref in zip(a_refs, w_refs):
        acc = acc + jnp.dot(a_ref[...], w_ref[...], preferred_element_type=F32)
    o_ref[...] = acc


def _matmul_res(a_list, w, res, *, tm=1024, tn=512, name):
    n_a = len(a_list)
    M, K = a_list[0].shape
    w, layer = w
    N = w.shape[2]
    tm = _pick_tile(M, tm, SUBLANES)
    tn = _pick_tile(N, tn, LANES)
    in_specs = [pl.BlockSpec((tm, K), lambda i, j: (i, 0)) for _ in range(n_a)]
    in_specs += [pl.BlockSpec((None, K, tn), lambda i, j, r=r: (layer, r, j)) for r in range(n_a)]
    in_specs += [pl.BlockSpec((tm, tn), lambda i, j: (i, j))]
    vmem = n_a * (2 * tm * K * 2 + 2 * K * tn * 2) + 5 * tm * tn * 4 + (4 << 20)
    return pl.pallas_call(
        functools.partial(_matmul_res_kernel, n_a=n_a),
        grid=(M // tm, N // tn),
        in_specs=in_specs,
        out_specs=pl.BlockSpec((tm, tn), lambda i, j: (i, j)),
        out_shape=jax.ShapeDtypeStruct((M, N), F32),
        compiler_params=_params(("parallel", "parallel"), vmem),
        name=name,
    )(*a_list, *([w] * n_a), res)


def _swiglu_up_kernel(x_ref, g_ref, w1_ref, w3_ref, o_ref, xn_ref):
    @pl.when(pl.program_id(1) == 0)
    def _():
        xn_ref[...] = _rms_normalize(x_ref[...], g_ref[...]).astype(BF16)

    xn = xn_ref[...]
    a = jnp.dot(xn, w1_ref[...], preferred_element_type=F32)
    b = jnp.dot(xn, w3_ref[...], preferred_element_type=F32)
    o_ref[...] = (a * jax.nn.sigmoid(a) * b).astype(o_ref.dtype)


def _swiglu_up(x, g, w13, *, tm=1024, tn=512, name):
    M, D = x.shape
    w13, layer = w13
    H = w13.shape[2] // 2
    tm = _pick_tile(M, tm, SUBLANES)
    tn = _pick_tile(H, tn, LANES)
    nj = H // tn
    vmem = 2 * tm * D * 4 + tm * D * 2 + 4 * D * tn * 2 + 2 * tm * tn * 2 + 4 * tm * tn * 4 + (4 << 20)
    return pl.pallas_call(
        _swiglu_up_kernel,
        grid=(M // tm, nj),
        in_specs=[pl.BlockSpec((tm, D), lambda i, j: (i, 0)),
                  pl.BlockSpec((1, D), lambda i, j: (0, 0)),
                  pl.BlockSpec((None, D, tn), lambda i, j: (layer, 0, j)),
                  pl.BlockSpec((None, D, tn), lambda i, j: (layer, 0, j + nj))],
        out_specs=pl.BlockSpec((tm, tn), lambda i, j: (i, j)),
        out_shape=jax.ShapeDtypeStruct((M, H), BF16),
        scratch_shapes=[pltpu.VMEM((tm, D), BF16)],
        compiler_params=_params(("parallel", "arbitrary"), vmem),
        name=name,
    )(x, g.reshape(1, D).astype(F32), w13, w13)


def _cross_attn_kernel(q_ref, k_ref, v_ref, o_ref, *, nh):
    dh = q_ref.shape[1] // nh
    for h in range(nh):
        sl = slice(h * dh, (h + 1) * dh)
        s = lax.dot_general(q_ref[:, sl], k_ref[:, sl], (((1,), (1,)), ((), ())), preferred_element_type=F32)
        m = jnp.max(s, axis=-1, keepdims=True)
        e = jnp.exp(s - m)
        den = jnp.sum(e, axis=-1, keepdims=True)
        o = jnp.dot(e.astype(BF16), v_ref[:, sl], preferred_element_type=F32)
        o_ref[:, sl] = (o / den).astype(o_ref.dtype)


def _cross_attn(q, kv, bsz, *, tq=1024, name):
    T, D = q.shape
    S = T // bsz
    mem_len = kv.shape[0] // bsz
    tq = _pick_tile(S, tq, SUBLANES)
    nq = S // tq
    kv3 = kv.reshape(bsz, mem_len, 2 * D)
    vmem = 4 * tq * D * 2 + 4 * mem_len * D * 2 + 4 * tq * mem_len * 4 + (8 << 20)
    return pl.pallas_call(
        functools.partial(_cross_attn_kernel, nh=XA_HEADS),
        grid=(bsz, nq),
        in_specs=[pl.BlockSpec((tq, D), lambda b, i: (b * nq + i, 0)),
                  pl.BlockSpec((None, mem_len, D), lambda b, i: (b, 0, 0)),
                  pl.BlockSpec((None, mem_len, D), lambda b, i: (b, 0, 1))],
        out_specs=pl.BlockSpec((tq, D), lambda b, i: (b * nq + i, 0)),
        out_shape=jax.ShapeDtypeStruct((T, D), BF16),
        compiler_params=_params(("parallel", "parallel"), vmem),
        name=name,
    )(q, kv3, kv3)


def _diff_attn_kernel(q_ref, k_ref, v_ref, lv_ref, sg_ref, o_ref, vt_ref, *, tk, unroll, lambda_init):
    tq = q_ref.shape[0]
    seq = k_ref.shape[0]
    dv = v_ref.shape[1]
    dh = dv // 2

    @pl.when(pl.program_id(2) == 0)
    def _():
        for c in range(seq // tk):
            vt_ref[:, c * tk:(c + 1) * tk] = v_ref[c * tk:(c + 1) * tk, :].astype(F32).T.astype(BF16)

    q = q_ref[...]
    nk = seq // tk

    def scores(kk):
        kc = k_ref[pl.ds(pl.multiple_of(kk * tk, tk), tk), :]
        return tuple(lax.dot_general(kc[:, c * dh:(c + 1) * dh], q[:, c * dh:(c + 1) * dh],
                                     (((1,), (1,)), ((), ())), preferred_element_type=F32)
                     for c in range(2))

    def body(kk, carry):
        st_next = scores(jnp.minimum(kk + 1, nk - 1))
        vt = vt_ref[:, pl.ds(pl.multiple_of(kk * tk, tk), tk)]
        new = []
        for c in range(2):
            st, m, l, acc = carry[c]
            m_new = jnp.maximum(m, jnp.max(st, axis=0, keepdims=True))
            alpha = jnp.exp2(m - m_new)
            pt = jnp.exp2(st - m_new)
            l = alpha * l + jnp.sum(pt, axis=0, keepdims=True)
            acc = alpha * acc + jnp.dot(vt, pt.astype(BF16), preferred_element_type=F32)
            new.append((st_next[c], m_new, l, acc))
        return tuple(new)

    st0 = scores(0)
    init = tuple((st0[c], jnp.full((1, tq), MASK_VALUE, F32), jnp.zeros((1, tq), F32),
                  jnp.zeros((dv, tq), F32)) for c in range(2))
    (_, _, l1, a1), (_, _, l2, a2) = lax.fori_loop(0, nk, body, init, unroll=unroll)
    lv = lv_ref[...]
    lam = (jnp.exp(jnp.sum(lv[0:1] * lv[1:2], axis=-1, keepdims=True))
           - jnp.exp(jnp.sum(lv[2:3] * lv[3:4], axis=-1, keepdims=True)) + lambda_init)
    ot = a1 / l1 - lam * (a2 / l2)
    ms = jnp.mean(ot * ot, axis=0, keepdims=True)
    ot = ot * lax.rsqrt(ms + NORM_EPS)
    o_ref[...] = (ot.T * sg_ref[...] * (1.0 - lambda_init)).astype(o_ref.dtype)


def _diff_attn(proj, lam_vec, subln_g, bsz, *, s5_width, n_heads, lambda_init, tq=512, tk=512, unroll=1, name):
    T, ncol = proj.shape
    S = T // bsz
    dv = 2 * HEAD_DIM
    width = n_heads * dv
    q0, k0, v0 = s5_width // dv, (s5_width + width) // dv, (s5_width + 2 * width) // dv
    tq = _pick_tile(S, tq, SUBLANES)
    tk = _pick_tile(S, tk, LANES)
    nq = S // tq
    proj3 = proj.reshape(bsz, S, ncol)
    vmem = 4 * tq * dv * 2 + 5 * S * dv * 2 + unroll * (3 * tq * tk * 4 + 2 * tq * dv * 4) + (8 << 20)
    return pl.pallas_call(
        functools.partial(_diff_attn_kernel, tk=tk, unroll=unroll, lambda_init=lambda_init),
        grid=(bsz, n_heads, nq),
        in_specs=[pl.BlockSpec((None, tq, dv), lambda b, h, i: (b, i, q0 + h)),
                  pl.BlockSpec((None, S, dv), lambda b, h, i: (b, 0, k0 + h)),
                  pl.BlockSpec((None, S, dv), lambda b, h, i: (b, 0, v0 + h)),
                  pl.BlockSpec((4, HEAD_DIM), lambda b, h, i: (0, 0)),
                  pl.BlockSpec((1, dv), lambda b, h, i: (0, 0))],
        out_specs=pl.BlockSpec((tq, dv), lambda b, h, i: (b * nq + i, h)),
        out_shape=jax.ShapeDtypeStruct((T, width), BF16),
        scratch_shapes=[pltpu.VMEM((dv, S), BF16)],
        compiler_params=_params(("parallel", "parallel", "arbitrary"), vmem),
        name=name,
    )(proj3, proj3, proj3, lam_vec.astype(F32), subln_g.reshape(1, dv).astype(F32))


def _s5_tables(lam_re, lam_im, log_step, b_re, b_im, c_re, c_im, d_skip, n_steps):
    L = S5_CHUNK
    _, G, P = lam_re.shape
    N = b_re.shape[-1]
    lr, li = lam_re.astype(F32), lam_im.astype(F32)
    step = jnp.exp(log_step.astype(F32))[..., None]
    ar, ai = lr * step, li * step

    def cexp(t, axes=()):
        a_r, a_i = jnp.expand_dims(ar, axes), jnp.expand_dims(ai, axes)
        mag = jnp.exp(t * a_r)
        return mag * jnp.cos(t * a_i), mag * jnp.sin(t * a_i)

    def cmul(xr, xi, yr, yi):
        return xr * yr - xi * yi, xr * yi + xi * yr

    er, ei = cexp(1.0)
    den = lr * lr + li * li
    qr, qi = ((er - 1.0) * lr + ei * li) / den, (ei * lr - (er - 1.0) * li) / den
    br, bi = cmul(qr[..., None], qi[..., None], b_re.astype(F32), b_im.astype(F32))
    cr, ci = c_re.astype(F32), c_im.astype(F32)
    tau = jnp.arange(L + 1, dtype=F32)[None, None, :, None]
    pr, pi = cexp(tau, axes=2)

    wr, wi = cmul(pr[:, :, :L, :, None], pi[:, :, :L, :, None], br[:, :, None], bi[:, :, None])
    kb_cat = jnp.transpose(jnp.concatenate([jnp.swapaxes(br, 2, 3), -jnp.swapaxes(bi, 2, 3)], axis=3),
                           (1, 0, 2, 3))
    pcr, pci = cmul(pr[:, :, :L, None, :], pi[:, :, :L, None, :], cr[:, :, None], ci[:, :, None])
    def _kc_rows(w):
        w = jnp.stack([w[0], w[1][:, ::-1]], axis=1)
        return jnp.transpose(w, (0, 1, 4, 2, 3)).reshape(G, 2, P, L * N)
    kc_cat = jnp.concatenate([_kc_rows(pcr), _kc_rows(pci)], axis=2)

    def _in_cols(w):
        return jnp.transpose(w, (0, 1, 3, 2)).reshape(G, L * N, P)
    w_in = jnp.concatenate([_in_cols(wr[0][:, ::-1]), _in_cols(wr[1]),
                            _in_cols(wi[0][:, ::-1]), _in_cols(wi[1])], axis=2)

    of_r, of_i = cmul(pr[0][:, 1:, None, :], pi[0][:, 1:, None, :], cr[0][:, None], ci[0][:, None])
    ob_r, ob_i = cmul(pr[1][:, ::-1][:, :L, None, :], pi[1][:, ::-1][:, :L, None, :],
                      cr[1][:, None], ci[1][:, None])
    def _out_rows(w):
        return jnp.transpose(w, (0, 3, 1, 2)).reshape(G, P, L * N)
    w_out = jnp.concatenate([_out_rows(of_r), _out_rows(ob_r), -_out_rows(of_i), -_out_rows(ob_i)], axis=1)

    def _lanes(z):
        return jnp.concatenate([z[0], z[1]], axis=-1)
    a_chunk, a_seg = cexp(float(L)), cexp(float(L * n_steps))
    decay = jnp.stack([_lanes(a_chunk[0]), _lanes(a_chunk[1]), _lanes(a_seg[0]), _lanes(a_seg[1])], axis=1)
    jj = float(L) * jnp.arange(n_steps, dtype=F32)
    jj = jnp.stack([jj, jj[::-1]], axis=0)[:, None, :, None]
    powers = jnp.stack([_lanes(p) for p in cexp(jj, axes=2)], axis=1)
    dvec = jnp.tile(d_skip.astype(F32).reshape(G, 1, N), (1, L, 1)).reshape(G, 1, L * N)
    return kb_cat, kc_cat, w_in.astype(BF16), w_out.astype(BF16), decay, powers, dvec


def _s5_kernel(u_ref, kb_ref, kc_ref, win_ref, wout_ref, decay_ref, pow_ref, dvec_ref, o_ref, toep_ref,
               *, n_steps, bsz, gb):
    rb = bsz * SUBLANES
    n_ch = kb_ref.shape[2]
    width = kc_ref.shape[3]
    chunk = width // n_ch
    lane = lax.broadcasted_iota(jnp.int32, (n_ch, width), 1)
    for gi in range(gb):
        kf = jnp.dot(kb_ref[gi, 0], kc_ref[gi, 0], preferred_element_type=F32, precision=lax.Precision.HIGHEST)
        kb = jnp.dot(kb_ref[gi, 1], kc_ref[gi, 1], preferred_element_type=F32, precision=lax.Precision.HIGHEST)
        for s in range(chunk):
            fwd = kf if s == 0 else jnp.where(lane >= s * n_ch, pltpu.roll(kf, s * n_ch, 1), 0.0)
            back = chunk - 1 - s
            bwd = kb if back == 0 else jnp.where(lane < (s + 1) * n_ch, pltpu.roll(kb, width - back * n_ch, 1), 0.0)
            toep_ref[gi, s * n_ch:(s + 1) * n_ch, :] = (fwd + bwd).astype(BF16)
        u = u_ref[gi]
        y = jnp.dot(u, toep_ref[gi], preferred_element_type=F32)
        z = jnp.dot(u, win_ref[gi], preferred_element_type=F32)
        half = z.shape[1] // 2
        zr, zi = z[:, :half], z[:, half:]
        dec = decay_ref[gi]
        ar, ai, sr, si = dec[0:1], dec[1:2], dec[2:3], dec[3:4]
        is_fwd = lax.broadcasted_iota(jnp.int32, (rb, half), 1) < half // 2

        xr = jnp.zeros((rb, half), F32)
        xi = jnp.zeros((rb, half), F32)
        ent_r, ent_i = [], []
        for j in range(n_steps):
            jb = n_steps - 1 - j
            ent_r.append(xr)
            ent_i.append(xi)
            zrj = jnp.where(is_fwd, zr[j * rb:(j + 1) * rb], zr[jb * rb:(jb + 1) * rb])
            zij = jnp.where(is_fwd, zi[j * rb:(j + 1) * rb], zi[jb * rb:(jb + 1) * rb])
            xr, xi = ar * xr - ai * xi + zrj, ar * xi + ai * xr + zij

        sub = lax.broadcasted_iota(jnp.int32, (SUBLANES, half), 0)
        fwd8 = lax.broadcasted_iota(jnp.int32, (SUBLANES, half), 1) < half // 2
        keep = (fwd8 & (sub > 0)) | (jnp.logical_not(fwd8) & (sub < SUBLANES - 1))
        car_r, car_i = [], []
        for b in range(bsz):
            er, ei = xr[b * SUBLANES:(b + 1) * SUBLANES], xi[b * SUBLANES:(b + 1) * SUBLANES]
            cr = jnp.zeros((SUBLANES, half), F32)
            ci = jnp.zeros((SUBLANES, half), F32)
            for _ in range(SUBLANES - 1):
                tr, ti = sr * cr - si * ci + er, sr * ci + si * cr + ei
                cr = jnp.where(keep, jnp.where(fwd8, pltpu.roll(tr, 1, 0), pltpu.roll(tr, SUBLANES - 1, 0)), 0.0)
                ci = jnp.where(keep, jnp.where(fwd8, pltpu.roll(ti, 1, 0), pltpu.roll(ti, SUBLANES - 1, 0)), 0.0)
            car_r.append(cr)
            car_i.append(ci)
        gr = car_r[0] if bsz == 1 else jnp.concatenate(car_r, axis=0)
        gi_ = car_i[0] if bsz == 1 else jnp.concatenate(car_i, axis=0)

        pr_all, pi_all = pow_ref[gi, 0], pow_ref[gi, 1]
        rows_r, rows_i = [], []
        for j in range(n_steps):
            jb = n_steps - 1 - j
            lr = jnp.where(is_fwd, ent_r[j], ent_r[jb])
            li = jnp.where(is_fwd, ent_i[j], ent_i[jb])
            pr, pi = pr_all[j:j + 1], pi_all[j:j + 1]
            rows_r.append(lr + pr * gr - pi * gi_)
            rows_i.append(li + pr * gi_ + pi * gr)
        state = jnp.concatenate([jnp.concatenate(rows_r, axis=0), jnp.concatenate(rows_i, axis=0)], axis=1)
        y = y + jnp.dot(state.astype(BF16), wout_ref[gi], preferred_element_type=F32)
        y = y + u.astype(F32) * dvec_ref[gi]
        o_ref[gi] = y.astype(o_ref.dtype)


def _s5_mix(u, tables, bsz, *, gb=4, name):
    kb_cat, kc_cat, w_in, w_out, decay, powers, dvec = tables
    T, width = u.shape
    S = T // bsz
    G = w_in.shape[0]
    N = width // G
    L = S5_CHUNK
    n_steps = S // (L * S5_SEGMENTS)
    rows = n_steps * bsz * S5_SEGMENTS
    gb = _pick_tile(G, gb, 1)
    ug = u.reshape(bsz, S5_SEGMENTS, n_steps, L, G, N)
    ug = jnp.transpose(ug, (4, 2, 0, 1, 3, 5)).reshape(G, rows, L * N)
    blk = lambda *shape: pl.BlockSpec((gb,) + shape, lambda g: (g,) + (0,) * len(shape))
    vmem = gb * (4 * rows * L * N * 2 + (L * N) ** 2 * 2 + 4 * kc_cat.shape[2] * L * N * 4
                 + 8 * L * N * w_in.shape[2] * 2) + (24 << 20)
    yg = pl.pallas_call(
        functools.partial(_s5_kernel, n_steps=n_steps, bsz=bsz, gb=gb),
        grid=(G // gb,),
        in_specs=[blk(rows, L * N), blk(*kb_cat.shape[1:]), blk(*kc_cat.shape[1:]),
                  blk(L * N, w_in.shape[2]), blk(w_out.shape[1], L * N),
                  blk(4, decay.shape[2]), blk(2, n_steps, powers.shape[3]), blk(1, L * N)],
        out_specs=blk(rows, L * N),
        out_shape=jax.ShapeDtypeStruct((G, rows, L * N), BF16),
        scratch_shapes=[pltpu.VMEM((gb, L * N, L * N), BF16)],
        compiler_params=_params(("parallel",), vmem),
        name=name,
    )(ug, kb_cat, kc_cat, w_in, w_out, decay, powers, dvec)
    yg = yg.reshape(G, n_steps, bsz, S5_SEGMENTS, L, N)
    return jnp.transpose(yg, (2, 3, 1, 4, 0, 5)).reshape(T, width)


def _s5_glu_kernel(y_ref, w_ref, b_ref, o_ref):
    g = jax.nn.gelu(y_ref[...].astype(F32), approximate=True)
    z = jnp.dot(g.astype(BF16), w_ref[...], preferred_element_type=F32) + b_ref[...]
    o_ref[...] = (g * jax.nn.sigmoid(z)).astype(o_ref.dtype)


def _s5_glu(y, w, b, *, tm=1024, name):
    M, W = y.shape
    w, layer = w
    tm = _pick_tile(M, tm, SUBLANES)
    vmem = 4 * tm * W * 2 + 2 * W * W * 2 + 4 * tm * W * 4 + (4 << 20)
    return pl.pallas_call(
        _s5_glu_kernel,
        grid=(M // tm,),
        in_specs=[pl.BlockSpec((tm, W), lambda i: (i, 0)),
                  pl.BlockSpec((None, W, W), lambda i: (layer, 0, 0)),
                  pl.BlockSpec((1, W), lambda i: (0, 0))],
        out_specs=pl.BlockSpec((tm, W), lambda i: (i, 0)),
        out_shape=jax.ShapeDtypeStruct((M, W), BF16),
        compiler_params=_params(("parallel",), vmem),
        name=name,
    )(y, w, b.reshape(1, W).astype(F32))


def _slab_start(i, tq, half, slab, sub_len):
    assert tq % half == 0 and (sub_len - slab) % half == 0
    return half * jnp.clip(i * (tq // half) - 1, 0, (sub_len - slab) // half)


def _dilated_kernel(q_ref, k_ref, v_ref, o_ref, lse_ref, *, sub_len, half, nh):
    i = pl.program_id(1)
    tq = q_ref.shape[0]
    slab = k_ref.shape[0]
    dh = q_ref.shape[1] // nh
    assert dh == LANES
    sq = 2 * half
    win = sq + 2 * half
    start = _slab_start(i, tq, half, slab, sub_len)
    head_lane = lax.broadcasted_iota(jnp.int32, (sq, nh), 1)
    ones = jnp.ones((win, LANES), BF16)
    for j in range(tq // sq):
        off = pl.multiple_of(jnp.clip(i * tq + j * sq - half - start, 0, slab - win), half)
        qpos = i * tq + j * sq + lax.broadcasted_iota(jnp.int32, (sq, win), 0)
        kpos = start + off + lax.broadcasted_iota(jnp.int32, (sq, win), 1)
        valid = jnp.abs(kpos - qpos) <= half
        rows = slice(j * sq, (j + 1) * sq)
        lse_all = jnp.zeros((sq, nh), F32)
        for h in range(nh):
            sl = slice(h * dh, (h + 1) * dh)
            s = lax.dot_general(q_ref[rows, sl], k_ref[pl.ds(off, win), sl], (((1,), (1,)), ((), ())),
                                preferred_element_type=F32)
            s = jnp.where(valid, s, MASK_VALUE)
            m = jnp.max(s, axis=-1, keepdims=True)
            e = jnp.exp2(s - m).astype(BF16)
            ov = jnp.dot(e, jnp.concatenate([v_ref[pl.ds(off, win), sl], ones], axis=1),
                         preferred_element_type=F32)
            den = ov[:, dh:]
            o_ref[rows, sl] = (ov[:, :dh] / den).astype(o_ref.dtype)
            lse_all = jnp.where(head_lane == h, m * LN_2 + jnp.log(den[:, :nh]), lse_all)
        lse_ref[rows, :] = lse_all


def _dilated_branch(q, k, v, cols, D, n_seq, *, half, nh, name):
    T = q.shape[0]
    sub_len = T // n_seq
    sq = 2 * half
    n_sub = max(n for n in (4, 2, 1) if sub_len % (n * sq) == 0 and n * sq + 2 * half <= sub_len)
    tq = n_sub * sq
    slab = tq + 2 * half
    nq = sub_len // tq
    cq, ck, cv = cols

    def kv_spec(c):
        return pl.BlockSpec((pl.Element(slab), pl.Element(D)),
                            lambda s, i: (pl.multiple_of(s * sub_len + _slab_start(i, tq, half, slab, sub_len),
                                                         half), c * D))

    vmem = 2 * (2 * tq * D * 2 + 2 * slab * D * 2) + (16 << 20)
    return pl.pallas_call(
        functools.partial(_dilated_kernel, sub_len=sub_len, half=half, nh=nh),
        grid=(n_seq, nq),
        in_specs=[pl.BlockSpec((tq, D), lambda s, i: (s * nq + i, cq)), kv_spec(ck), kv_spec(cv)],
        out_specs=[pl.BlockSpec((tq, D), lambda s, i: (s * nq + i, 0)),
                   pl.BlockSpec((tq, nh), lambda s, i: (s * nq + i, 0))],
        out_shape=[jax.ShapeDtypeStruct((T, D), BF16), jax.ShapeDtypeStruct((T, nh), F32)],
        compiler_params=_params(("parallel", "parallel"), vmem),
        name=name,
    )(q, k, v)


def _residue_perm(dil):
    per = PERM_GROUP // dil
    rows = jnp.arange(PERM_GROUP)
    src = (rows % per) * dil + rows // per
    return (src[:, None] == rows[None, :]).astype(BF16)


def _to_residue_kernel(*refs, dils, n_in):
    n_d = len(dils)
    p_refs, x_refs, o_refs = refs[:n_d], refs[n_d:n_d + n_in], refs[n_d + n_in:]
    for xi, x_ref in enumerate(x_refs):
        for g in range(x_ref.shape[0] // PERM_GROUP):
            xg = x_ref[g * PERM_GROUP:(g + 1) * PERM_GROUP, :]
            for di, dil in enumerate(dils):
                o_ref = o_refs[di * n_in + xi]
                per = PERM_GROUP // dil
                y = jnp.dot(p_refs[di][...], xg, preferred_element_type=F32).astype(o_ref.dtype)
                for r in range(dil):
                    o_ref[r, g * per:(g + 1) * per, :] = y[r * per:(r + 1) * per]


def _to_residue(x, cols, width, dils, bsz, *, tb=512, name):
    T = x.shape[0]
    S = T // bsz
    tb = _pick_tile(S, tb, PERM_GROUP)
    nb = S // tb
    n_in = len(cols)
    perm_spec = pl.BlockSpec((PERM_GROUP, PERM_GROUP), lambda b, i: (0, 0))
    outs = pl.pallas_call(
        functools.partial(_to_residue_kernel, dils=tuple(dils), n_in=n_in),
        grid=(bsz, nb),
        in_specs=[perm_spec] * len(dils)
                 + [pl.BlockSpec((tb, width), lambda b, i, c=c: (b * nb + i, c)) for c in cols],
        out_specs=[pl.BlockSpec((None, dil, tb // dil, width), lambda b, i: (b, 0, i, 0))
                   for dil in dils for _ in cols],
        out_shape=[jax.ShapeDtypeStruct((bsz, dil, S // dil, width), x.dtype) for dil in dils for _ in cols],
        compiler_params=_params(("parallel", "parallel"),
                                (1 + len(dils)) * n_in * 2 * tb * width * 2 + (12 << 20)),
        name=name,
    )(*[_residue_perm(dil) for dil in dils], *([x] * n_in))
    outs = [o.reshape(T, width) for o in outs]
    return [outs[di * n_in:(di + 1) * n_in] for di in range(len(dils))]


def _from_residue_merge_kernel(p_ref, x_ref, lx_ref, prev_ref, lp_ref, o_ref, lo_ref, *, dil, nh):
    per = PERM_GROUP // dil
    dh = o_ref.shape[1] // nh
    lp, lx = lp_ref[...], lx_ref[...]
    mx = jnp.maximum(lp, lx)
    ep, ex = jnp.exp(lp - mx), jnp.exp(lx - mx)
    wx = ex / (ep + ex)
    lo_ref[...] = mx + jnp.log(ep + ex)
    for g in range(o_ref.shape[0] // PERM_GROUP):
        rows = slice(g * PERM_GROUP, (g + 1) * PERM_GROUP)
        xg = jnp.concatenate([x_ref[r, g * per:(g + 1) * per, :] for r in range(dil)], axis=0)
        y = jnp.dot(p_ref[...], xg, preferred_element_type=F32)
        for h in range(nh):
            sl = slice(h * dh, (h + 1) * dh)
            prev = prev_ref[rows, sl].astype(F32)
            o_ref[rows, sl] = (prev + wx[rows, h:h + 1] * (y[:, sl] - prev)).astype(o_ref.dtype)


def _from_residue_merge(x, lse_x, prev, lse_prev, dil, bsz, *, nh, tb=512, name):
    T, width = x.shape
    S = T // bsz
    tb = _pick_tile(S, tb, PERM_GROUP)
    nb = S // tb
    row = lambda w: pl.BlockSpec((tb, w), lambda b, i: (b * nb + i, 0))
    return pl.pallas_call(
        functools.partial(_from_residue_merge_kernel, dil=dil, nh=nh),
        grid=(bsz, nb),
        in_specs=[pl.BlockSpec((PERM_GROUP, PERM_GROUP), lambda b, i: (0, 0)),
                  pl.BlockSpec((None, dil, tb // dil, width), lambda b, i: (b, 0, i, 0)),
                  row(nh), row(width), row(nh)],
        out_specs=[row(width), row(nh)],
        out_shape=[jax.ShapeDtypeStruct((T, width), x.dtype), jax.ShapeDtypeStruct((T, nh), F32)],
        compiler_params=_params(("parallel", "parallel"), 6 * tb * width * 2 + 4 * tb * width * 4 + (8 << 20)),
        name=name,
    )(_residue_perm(dil).T, x.reshape(bsz, dil, S // dil, width), lse_x, prev, lse_prev)


def _final_norm_kernel(x_ref, g_ref, o_ref):
    o_ref[...] = _rms_normalize(x_ref[...], g_ref[...])


def _final_norm(x, g, *, tm=1024, name):
    M, D = x.shape
    tm = _pick_tile(M, tm, SUBLANES)
    return pl.pallas_call(
        _final_norm_kernel,
        grid=(M // tm,),
        in_specs=[pl.BlockSpec((tm, D), lambda i: (i, 0)), pl.BlockSpec((1, D), lambda i: (0, 0))],
        out_specs=pl.BlockSpec((tm, D), lambda i: (i, 0)),
        out_shape=jax.ShapeDtypeStruct((M, D), F32),
        compiler_params=_params(("parallel",), 6 * tm * D * 4 + (4 << 20)),
        name=name,
    )(x, g.reshape(1, D).astype(F32))


def _even_mixer(h, g, w_in, w_out, s5_tabs, glu_w, glu_b, lam_vec, subln_g, lambda_init, rope_tabs, bsz, tag):
    T, _ = h.shape
    S = T // bsz
    s5_width = glu_w[0].shape[1]
    width = (w_in[0].shape[2] - s5_width) // 3
    n_heads = width // (2 * HEAD_DIM)
    segs = ((s5_width, s5_width + width, True, HEAD_DIM ** -0.5 * LOG2_E),
            (s5_width + width, s5_width + 2 * width, True, 1.0))
    proj = _norm_matmul(h, g, w_in, segs=segs, rope_tabs=rope_tabs, seq=S, name=f"even_in_{tag}")
    y_s5 = _s5_mix(proj[:, :s5_width], s5_tabs, bsz, name=f"s5_{tag}")
    y_s5 = _s5_glu(y_s5, glu_w, glu_b, name=f"s5_glu_{tag}")
    y_diff = _diff_attn(proj, lam_vec, subln_g, bsz, s5_width=s5_width, n_heads=n_heads,
                        lambda_init=lambda_init, tk=1024, unroll=8, name=f"diff_attn_{tag}")
    return _matmul_res([y_s5, y_diff], w_out, h, tn=1024, name=f"even_out_{tag}")


def _odd_mixer(h, g, w_qkv, w_out, rope_tabs, bsz, tag):
    T, D = h.shape
    S = T // bsz
    nh = D // HEAD_DIM
    segs = ((0, D, True, HEAD_DIM ** -0.5 * LOG2_E), (D, 2 * D, True, 1.0))
    qkv = _norm_matmul(h, g, w_qkv, segs=segs, rope_tabs=rope_tabs, seq=S, name=f"odd_in_{tag}")
    dils = [dil for _, dil in DILATED_BRANCHES if dil > 1]
    residue = dict(zip(dils, _to_residue(qkv, (0, 1, 2), D, dils, bsz, name=f"to_residue_{tag}")))
    merged = lse_merged = None
    for window, dil in sorted(DILATED_BRANCHES, key=lambda wd: wd[1]):
        half = window // (2 * dil)
        sub_len = S // dil
        if dil == 1:
            merged, lse_merged = _dilated_branch(qkv, qkv, qkv, (0, 1, 2), D, bsz, half=half, nh=nh,
                                                 name=f"dilated_{dil}_{tag}")
        else:
            qr, kr, vr = residue[dil]
            o, lse = _dilated_branch(qr, kr, vr, (0, 0, 0), D, bsz * dil, half=half, nh=nh,
                                     name=f"dilated_{dil}_{tag}")
            lse = jnp.transpose(lse.reshape(bsz, dil, sub_len, nh), (0, 2, 1, 3)).reshape(T, nh)
            merged, lse_merged = _from_residue_merge(o, lse, merged, lse_merged, dil, bsz, nh=nh,
                                                     name=f"from_residue_{dil}_{tag}")
    return _matmul_res([merged], w_out, h, tn=1024, name=f"odd_out_{tag}")


def kernel(x, mem, norm_mix_g, norm_xa_g, norm_mem_g, xa_wq, xa_wkv, xa_wo, norm_ffn_g, ffn_w13, ffn_w2, ab_w_in, ab_w_out, s5_lambda_re, s5_lambda_im, s5_log_step, s5_b_re, s5_b_im, s5_c_re, s5_c_im, s5_d, s5_glu_w, s5_glu_b, diff_lambda, diff_subln_g, c_w_qkv, c_w_out, final_norm_g):
    bsz, S, D = x.shape
    T = bsz * S
    depth = norm_mix_g.shape[0]
    mem_len = mem.shape[1]
    rope_tabs = _rope_tables(S)
    n_steps = S // (S5_CHUNK * S5_SEGMENTS)
    s5_width = s5_glu_w.shape[1]
    diff_width = (ab_w_in.shape[2] - s5_width) // 3
    ab_w_in = _cast_rope_columns(ab_w_in, s5_width, s5_width + 2 * diff_width, name="cast_even_in")
    c_w_qkv = _cast_rope_columns(c_w_qkv, 0, 2 * D, name="cast_odd_in")
    ab_w_out, s5_glu_w, c_w_out, xa_wq, xa_wkv, xa_wo, ffn_w13, ffn_w2 = (
        w.astype(BF16) for w in (ab_w_out, s5_glu_w, c_w_out, xa_wq, xa_wkv, xa_wo, ffn_w13, ffn_w2))

    h = x.reshape(T, D)
    mem2 = mem.reshape(bsz * mem_len, D)
    for layer in range(depth):
        i = layer // 2
        if layer % 2 == 0:
            lambda_init = 0.8 - 0.6 * math.exp(-0.3 * layer)
            tabs = _s5_tables(s5_lambda_re[i], s5_lambda_im[i], s5_log_step[i], s5_b_re[i], s5_b_im[i],
                              s5_c_re[i], s5_c_im[i], s5_d[i], n_steps)
            h = _even_mixer(h, norm_mix_g[layer], (ab_w_in, i), (ab_w_out, i), tabs, (s5_glu_w, i),
                            s5_glu_b[i], diff_lambda[i], diff_subln_g[i], lambda_init, rope_tabs, bsz, layer)
        else:
            h = _odd_mixer(h, norm_mix_g[layer], (c_w_qkv, i), (c_w_out, i), rope_tabs, bsz, layer)

        q = _norm_matmul(h, norm_xa_g[layer], (xa_wq, layer), name=f"xa_q_{layer}")
        kv = _norm_matmul(mem2, norm_mem_g[layer], (xa_wkv, layer),
                          segs=((0, D, False, (D // XA_HEADS) ** -0.5),), name=f"xa_kv_{layer}")
        o = _cross_attn(q, kv, bsz, name=f"xa_core_{layer}")
        h = _matmul_res([o], (xa_wo, layer), h, tn=1024, name=f"xa_out_{layer}")

        hid = _swiglu_up(h, norm_ffn_g[layer], (ffn_w13, layer), name=f"ffn_up_{layer}")
        h = _matmul_res([hid], (ffn_w2, layer), h, tn=512, name=f"ffn_down_{layer}")
    return _final_norm(h, final_norm_g, name="final_norm").reshape(bsz, S, D)
```

```python
import functools
import math

import jax
import jax.numpy as jnp
from jax import lax
from jax.experimental import pallas as pl
from jax.experimental.pallas import tpu as pltpu

F32 = jnp.float32
BF16 = jnp.bfloat16

NORM_EPS = 1e-6
MASK_VALUE = -1e30
LOG2_E = math.log2(math.e)
LN_2 = math.log(2.0)
ROPE_THETA = 500000.0
ROPE_HALF = 16
HEAD_DIM = 128
XA_HEADS = 4
DILATED_BRANCHES = ((128, 1), (512, 4), (2048, 16))
S5_CHUNK = 32
S5_SEGMENTS = 8
PERM_GROUP = 256
LANES = 128
SUBLANES = 8
VMEM_CAP_BYTES = 56 * 1024 * 1024


def _pick_tile(n, pref, quantum):
    t = (min(n, pref) // quantum) * quantum
    while t >= quantum:
        if n % t == 0:
            return t
        t -= quantum
    return n


def _params(semantics, vmem_bytes):
    return pltpu.CompilerParams(
        dimension_semantics=semantics,
        vmem_limit_bytes=int(min(VMEM_CAP_BYTES, max(32 * 1024 * 1024, vmem_bytes))))


def _rms_normalize(x, g):
    ms = jnp.mean(x * x, axis=-1, keepdims=True)
    return (x * lax.rsqrt(ms + NORM_EPS)) * g


def _rope_lanes(y, cf, sf):
    outs = []
    for c in range(y.shape[1] // LANES):
        yc = y[:, c * LANES:(c + 1) * LANES]
        outs.append(yc * cf + pltpu.roll(yc, LANES // 2, 1) * sf)
    return outs[0] if len(outs) == 1 else jnp.concatenate(outs, axis=1)


def _cast_rope_columns_kernel(p_ref, w_ref, o_ref, *, lo_blk, hi_blk):
    j = pl.program_id(2)
    w = w_ref[...].astype(BF16)
    roped = jnp.logical_and(j >= lo_blk, j < hi_blk)

    @pl.when(roped)
    def _():
        o_ref[...] = jnp.dot(w, p_ref[...], preferred_element_type=F32).astype(BF16)

    @pl.when(jnp.logical_not(roped))
    def _():
        o_ref[...] = w


def _cast_rope_columns(w, lo, hi, *, tr=1024, tc=512, name):
    n_layers, K, N = w.shape
    tr = _pick_tile(K, tr, SUBLANES)
    tc = _pick_tile(math.gcd(N, lo, hi), tc, LANES)
    r, h = ROPE_HALF, LANES // 2
    src = jnp.concatenate([jnp.arange(0, r), jnp.arange(2 * r, h + r), jnp.arange(r, 2 * r),
                           jnp.arange(h + r, LANES)])
    perm = (jnp.arange(LANES)[:, None] == src[None, :]).astype(BF16)
    perm = jnp.kron(jnp.eye(tc // LANES, dtype=BF16), perm)
    return pl.pallas_call(
        functools.partial(_cast_rope_columns_kernel, lo_blk=lo // tc, hi_blk=hi // tc),
        grid=(n_layers, K // tr, N // tc),
        in_specs=[pl.BlockSpec((tc, tc), lambda l, i, j: (0, 0)),
                  pl.BlockSpec((None, tr, tc), lambda l, i, j: (l, i, j))],
        out_specs=pl.BlockSpec((None, tr, tc), lambda l, i, j: (l, i, j)),
        out_shape=jax.ShapeDtypeStruct(w.shape, BF16),
        compiler_params=_params(("parallel", "parallel", "parallel"), 8 * tr * tc * 4 + (4 << 20)),
        name=name,
    )(perm, w)


def _rope_tables(seq):
    pos = jnp.arange(seq, dtype=F32)
    inv = ROPE_THETA ** (-jnp.arange(0, 2 * ROPE_HALF, 2, dtype=F32) / (2 * ROPE_HALF))
    ang = pos[:, None] * inv[None, :]
    cos, sin = jnp.cos(ang), jnp.sin(ang)
    gap = LANES // 2 - ROPE_HALF
    cf = jnp.concatenate([cos, jnp.ones((seq, gap), F32), cos, jnp.ones((seq, gap), F32)], axis=1)
    sf = jnp.concatenate([-sin, jnp.zeros((seq, gap), F32), sin, jnp.zeros((seq, gap), F32)], axis=1)
    return cf, sf


def _norm_matmul_kernel(*refs, segs, tn, use_rope):
    if use_rope:
        x_ref, g_ref, w_ref, cf_ref, sf_ref, o_ref, xn_ref = refs
    else:
        x_ref, g_ref, w_ref, o_ref, xn_ref = refs
    j = pl.program_id(1)

    @pl.when(j == 0)
    def _():
        xn_ref[...] = _rms_normalize(x_ref[...], g_ref[...]).astype(BF16)

    y = jnp.dot(xn_ref[...], w_ref[...], preferred_element_type=F32)
    for lo, hi, rope, scale in segs:
        @pl.when(jnp.logical_and(j >= lo // tn, j < hi // tn))
        def _(rope=rope, scale=scale):
            z = y
            if rope:
                cf, sf = cf_ref[...], sf_ref[...]
                if scale != 1.0:
                    cf, sf = cf * scale, sf * scale
                z = _rope_lanes(z, cf, sf)
            elif scale != 1.0:
                z = z * scale
            o_ref[...] = z.astype(o_ref.dtype)


def _norm_matmul(x, g, w, *, segs=(), rope_tabs=None, seq=None, tm=1024, tn=1024, name):
    M, D = x.shape
    w, layer = w
    N = w.shape[2]
    tm = _pick_tile(M if seq is None else seq, tm, SUBLANES)
    tn = _pick_tile(math.gcd(N, *[b for s in segs for b in s[:2]]), tn, LANES)
    full, pos = [], 0
    for lo, hi, rope, scale in sorted(segs):
        if lo > pos:
            full.append((pos, lo, False, 1.0))
        full.append((lo, hi, rope, scale))
        pos = hi
    if pos < N:
        full.append((pos, N, False, 1.0))
    for lo, hi, _, _ in full:
        assert lo % tn == 0 and hi % tn == 0, (lo, hi, tn)
    use_rope = any(s[2] for s in full)
    in_specs = [pl.BlockSpec((tm, D), lambda i, j: (i, 0)),
                pl.BlockSpec((1, D), lambda i, j: (0, 0)),
                pl.BlockSpec((None, D, tn), lambda i, j: (layer, 0, j))]
    args = [x, g.reshape(1, D).astype(F32), w]
    if use_rope:
        nseq = seq // tm
        in_specs += [pl.BlockSpec((tm, LANES), lambda i, j: (i % nseq, 0))] * 2
        args += list(rope_tabs)
    vmem = 2 * tm * D * 4 + tm * D * 2 + 2 * D * tn * 2 + 2 * tm * tn * 2 + 3 * tm * tn * 4 + (4 << 20)
    return pl.pallas_call(
        functools.partial(_norm_matmul_kernel, segs=tuple(full), tn=tn, use_rope=use_rope),
        grid=(M // tm, N // tn),
        in_specs=in_specs,
        out_specs=pl.BlockSpec((tm, tn), lambda i, j: (i, j)),
        out_shape=jax.ShapeDtypeStruct((M, N), BF16),
        scratch_shapes=[pltpu.VMEM((tm, D), BF16)],
        compiler_params=_params(("parallel", "arbitrary"), vmem),
        name=name,
    )(*args)


def _matmul_res_kernel(*refs, n_a):
    a_refs, w_refs = refs[:n_a], refs[n_a:2 * n_a]
    res_ref, o_ref = refs[2 * n_a], refs[2 * n_a + 1]
    acc = res_ref[...]
    for a_ref, w_ref in zip(a_refs, w_refs):
        acc = acc + jnp.dot(a_ref[...], w_ref[...], preferred_element_type=F32)
    o_ref[...] = acc


def _matmul_res(a_list, w, res, *, tm=1024, tn=512, name):
    n_a = len(a_list)
    M, K = a_list[0].shape
    w, layer = w
    N = w.shape[2]
    tm = _pick_tile(M, tm, SUBLANES)
    tn = _pick_tile(N, tn, LANES)
    in_specs = [pl.BlockSpec((tm, K), lambda i, j: (i, 0)) for _ in range(n_a)]
    in_specs += [pl.BlockSpec((None, K, tn), lambda i, j, r=r: (layer, r, j)) for r in range(n_a)]
    in_specs += [pl.BlockSpec((tm, tn), lambda i, j: (i, j))]
    vmem = n_a * (2 * tm * K * 2 + 2 * K * tn * 2) + 5 * tm * tn * 4 + (4 << 20)
    return pl.pallas_call(
        functools.partial(_matmul_res_kernel, n_a=n_a),
        grid=(M // tm, N // tn),
        in_specs=in_specs,
        out_specs=pl.BlockSpec((tm, tn), lambda i, j: (i, j)),
        out_shape=jax.ShapeDtypeStruct((M, N), F32),
        compiler_params=_params(("parallel", "parallel"), vmem),
        name=name,
    )(*a_list, *([w] * n_a), res)


def _swiglu_up_kernel(x_ref, g_ref, w1_ref, w3_ref, o_ref, xn_ref):
    @pl.when(pl.program_id(1) == 0)
    def _():
        xn_ref[...] = _rms_normalize(x_ref[...], g_ref[...]).astype(BF16)

    xn = xn_ref[...]
    a = jnp.dot(xn, w1_ref[...], preferred_element_type=F32)
    b = jnp.dot(xn, w3_ref[...], preferred_element_type=F32)
    o_ref[...] = (a * jax.nn.sigmoid(a) * b).astype(o_ref.dtype)


def _swiglu_up(x, g, w13, *, tm=1024, tn=512, name):
    M, D = x.shape
    w13, layer = w13
    H = w13.shape[2] // 2
    tm = _pick_tile(M, tm, SUBLANES)
    tn = _pick_tile(H, tn, LANES)
    nj = H // tn
    vmem = 2 * tm * D * 4 + tm * D * 2 + 4 * D * tn * 2 + 2 * tm * tn * 2 + 4 * tm * tn * 4 + (4 << 20)
    return pl.pallas_call(
        _swiglu_up_kernel,
        grid=(M // tm, nj),
        in_specs=[pl.BlockSpec((tm, D), lambda i, j: (i, 0)),
                  pl.BlockSpec((1, D), lambda i, j: (0, 0)),
                  pl.BlockSpec((None, D, tn), lambda i, j: (layer, 0, j)),
                  pl.BlockSpec((None, D, tn), lambda i, j: (layer, 0, j + nj))],
        out_specs=pl.BlockSpec((tm, tn), lambda i, j: (i, j)),
        out_shape=jax.ShapeDtypeStruct((M, H), BF16),
        scratch_shapes=[pltpu.VMEM((tm, D), BF16)],
        compiler_params=_params(("parallel", "arbitrary"), vmem),
        name=name,
    )(x, g.reshape(1, D).astype(F32), w13, w13)


def _cross_attn_kernel(q_ref, k_ref, v_ref, o_ref, *, nh):
    dh = q_ref.shape[1] // nh
    for h in range(nh):
        sl = slice(h * dh, (h + 1) * dh)
        s = lax.dot_general(q_ref[:, sl], k_ref[:, sl], (((1,), (1,)), ((), ())), preferred_element_type=F32)
        m = jnp.max(s, axis=-1, keepdims=True)
        e = jnp.exp(s - m)
        den = jnp.sum(e, axis=-1, keepdims=True)
        o = jnp.dot(e.astype(BF16), v_ref[:, sl], preferred_element_type=F32)
        o_ref[:, sl] = (o / den).astype(o_ref.dtype)


def _cross_attn(q, kv, bsz, *, tq=1024, name):
    T, D = q.shape
    S = T // bsz
    mem_len = kv.shape[0] // bsz
    tq = _pick_tile(S, tq, SUBLANES)
    nq = S // tq
    kv3 = kv.reshape(bsz, mem_len, 2 * D)
    vmem = 4 * tq * D * 2 + 4 * mem_len * D * 2 + 4 * tq * mem_len * 4 + (8 << 20)
    return pl.pallas_call(
        functools.partial(_cross_attn_kernel, nh=XA_HEADS),
        grid=(bsz, nq),
        in_specs=[pl.BlockSpec((tq, D), lambda b, i: (b * nq + i, 0)),
                  pl.BlockSpec((None, mem_len, D), lambda b, i: (b, 0, 0)),
                  pl.BlockSpec((None, mem_len, D), lambda b, i: (b, 0, 1))],
        out_specs=pl.BlockSpec((tq, D), lambda b, i: (b * nq + i, 0)),
        out_shape=jax.ShapeDtypeStruct((T, D), BF16),
        compiler_params=_params(("parallel", "parallel"), vmem),
        name=name,
    )(q, kv3, kv3)


def _diff_attn_kernel(q_ref, k_ref, v_ref, lv_ref, sg_ref, o_ref, vt_ref, *, tk, unroll, lambda_init):
    tq = q_ref.shape[0]
    seq = k_ref.shape[0]
    dv = v_ref.shape[1]
    dh = dv // 2

    @pl.when(pl.program_id(2) == 0)
    def _():
        for c in range(seq // tk):
            vt_ref[:, c * tk:(c + 1) * tk] = v_ref[c * tk:(c + 1) * tk, :].astype(F32).T.astype(BF16)

    q = q_ref[...]
    nk = seq // tk

    def scores(kk):
        kc = k_ref[pl.ds(pl.multiple_of(kk * tk, tk), tk), :]
        return tuple(lax.dot_general(kc[:, c * dh:(c + 1) * dh], q[:, c * dh:(c + 1) * dh],
                                     (((1,), (1,)), ((), ())), preferred_element_type=F32)
                     for c in range(2))

    def body(kk, carry):
        st_next = scores(jnp.minimum(kk + 1, nk - 1))
        vt = vt_ref[:, pl.ds(pl.multiple_of(kk * tk, tk), tk)]
        new = []
        for c in range(2):
            st, m, l, acc = carry[c]
            m_new = jnp.maximum(m, jnp.max(st, axis=0, keepdims=True))
            alpha = jnp.exp2(m - m_new)
            pt = jnp.exp2(st - m_new)
            l = alpha * l + jnp.sum(pt, axis=0, keepdims=True)
            acc = alpha * acc + jnp.dot(vt, pt.astype(BF16), preferred_element_type=F32)
            new.append((st_next[c], m_new, l, acc))
        return tuple(new)

    st0 = scores(0)
    init = tuple((st0[c], jnp.full((1, tq), MASK_VALUE, F32), jnp.zeros((1, tq), F32),
                  jnp.zeros((dv, tq), F32)) for c in range(2))
    (_, _, l1, a1), (_, _, l2, a2) = lax.fori_loop(0, nk, body, init, unroll=unroll)
    lv = lv_ref[...]
    lam = (jnp.exp(jnp.sum(lv[0:1] * lv[1:2], axis=-1, keepdims=True))
           - jnp.exp(jnp.sum(lv[2:3] * lv[3:4], axis=-1, keepdims=True)) + lambda_init)
    ot = a1 / l1 - lam * (a2 / l2)
    ms = jnp.mean(ot * ot, axis=0, keepdims=True)
    ot = ot * lax.rsqrt(ms + NORM_EPS)
    o_ref[...] = (ot.T * sg_ref[...] * (1.0 - lambda_init)).astype(o_ref.dtype)


def _diff_attn(proj, lam_vec, subln_g, bsz, *, s5_width, n_heads, lambda_init, tq=512, tk=512, unroll=1, name):
    T, ncol = proj.shape
    S = T // bsz
    dv = 2 * HEAD_DIM
    width = n_heads * dv
    q0, k0, v0 = s5_width // dv, (s5_width + width) // dv, (s5_width + 2 * width) // dv
    tq = _pick_tile(S, tq, SUBLANES)
    tk = _pick_tile(S, tk, LANES)
    nq = S // tq
    proj3 = proj.reshape(bsz, S, ncol)
    vmem = 4 * tq * dv * 2 + 5 * S * dv * 2 + unroll * (3 * tq * tk * 4 + 2 * tq * dv * 4) + (8 << 20)
    return pl.pallas_call(
        functools.partial(_diff_attn_kernel, tk=tk, unroll=unroll, lambda_init=lambda_init),
        grid=(bsz, n_heads, nq),
        in_specs=[pl.BlockSpec((None, tq, dv), lambda b, h, i: (b, i, q0 + h)),
                  pl.BlockSpec((None, S, dv), lambda b, h, i: (b, 0, k0 + h)),
                  pl.BlockSpec((None, S, dv), lambda b, h, i: (b, 0, v0 + h)),
                  pl.BlockSpec((4, HEAD_DIM), lambda b, h, i: (0, 0)),
                  pl.BlockSpec((1, dv), lambda b, h, i: (0, 0))],
        out_specs=pl.BlockSpec((tq, dv), lambda b, h, i: (b * nq + i, h)),
        out_shape=jax.ShapeDtypeStruct((T, width), BF16),
        scratch_shapes=[pltpu.VMEM((dv, S), BF16)],
        compiler_params=_params(("parallel", "parallel", "arbitrary"), vmem),
        name=name,
    )(proj3, proj3, proj3, lam_vec.astype(F32), subln_g.reshape(1, dv).astype(F32))


def _s5_tables(lam_re, lam_im, log_step, b_re, b_im, c_re, c_im, d_skip, n_steps):
    L = S5_CHUNK
    _, G, P = lam_re.shape
    N = b_re.shape[-1]
    lr, li = lam_re.astype(F32), lam_im.astype(F32)
    step = jnp.exp(log_step.astype(F32))[..., None]
    ar, ai = lr * step, li * step

    def cexp(t, axes=()):
        a_r, a_i = jnp.expand_dims(ar, axes), jnp.expand_dims(ai, axes)
        mag = jnp.exp(t * a_r)
        return mag * jnp.cos(t * a_i), mag * jnp.sin(t * a_i)

    def cmul(xr, xi, yr, yi):
        return xr * yr - xi * yi, xr * yi + xi * yr

    er, ei = cexp(1.0)
    den = lr * lr + li * li
    qr, qi = ((er - 1.0) * lr + ei * li) / den, (ei * lr - (er - 1.0) * li) / den
    br, bi = cmul(qr[..., None], qi[..., None], b_re.astype(F32), b_im.astype(F32))
    cr, ci = c_re.astype(F32), c_im.astype(F32)
    tau = jnp.arange(L + 1, dtype=F32)
    pr, pi = cexp(tau[None, None, :, None], axes=2)
    prt, pit = cexp(tau[None, None, None, :], axes=3)
    brt, bit = jnp.swapaxes(br, 2, 3), jnp.swapaxes(bi, 2, 3)
    crt, cit = jnp.swapaxes(cr, 2, 3), jnp.swapaxes(ci, 2, 3)

    def pow_c(d, taus):
        return cmul(prt[d][:, :, taus, None], pit[d][:, :, taus, None], crt[d][:, :, None, :], cit[d][:, :, None, :])

    kb_cat = jnp.stack([jnp.concatenate([brt[d], -bit[d]], axis=2) for d in range(2)], axis=1)
    kc_cat = jnp.stack([jnp.concatenate(pow_c(d, taus), axis=1).reshape(G, 2 * P, L * N)
                        for d, taus in ((0, slice(0, L)), (1, slice(L - 1, None, -1)))], axis=1)

    def pow_b(d, taus):
        return cmul(pr[d][:, taus, None, :], pi[d][:, taus, None, :], brt[d][:, None], bit[d][:, None])
    (fr, fi), (gr, gi) = pow_b(0, slice(L - 1, None, -1)), pow_b(1, slice(0, L))
    w_in = jnp.concatenate([fr, gr, fi, gi], axis=3).reshape(G, L * N, 4 * P)

    (of_r, of_i), (ob_r, ob_i) = pow_c(0, slice(1, L + 1)), pow_c(1, slice(L, 0, -1))
    w_out = jnp.concatenate([of_r, ob_r, -of_i, -ob_i], axis=1).reshape(G, 4 * P, L * N)

    def _lanes(z):
        return jnp.concatenate([z[0], z[1]], axis=-1)
    a_chunk, a_seg = cexp(float(L)), cexp(float(L * n_steps))
    decay = jnp.stack([_lanes(a_chunk[0]), _lanes(a_chunk[1]), _lanes(a_seg[0]), _lanes(a_seg[1])], axis=1)
    jj = float(L) * jnp.arange(n_steps, dtype=F32)
    jj = jnp.stack([jj, jj[::-1]], axis=0)[:, None, :, None]
    powers = jnp.stack([_lanes(p) for p in cexp(jj, axes=2)], axis=1)
    dvec = jnp.tile(d_skip.astype(F32).reshape(G, 1, N), (1, L, 1)).reshape(G, 1, L * N)
    return kb_cat, kc_cat, w_in.astype(BF16), w_out.astype(BF16), decay, powers, dvec


def _s5_kernel(u_ref, kb_ref, kc_ref, win_ref, wout_ref, decay_ref, pow_ref, dvec_ref, o_ref, toep_ref,
               *, n_steps, bsz, gb):
    rb = bsz * SUBLANES
    n_ch = kb_ref.shape[2]
    width = kc_ref.shape[3]
    chunk = width // n_ch
    lane = lax.broadcasted_iota(jnp.int32, (n_ch, width), 1)
    for gi in range(gb):
        kf = jnp.dot(kb_ref[gi, 0], kc_ref[gi, 0], preferred_element_type=F32, precision=lax.Precision.HIGHEST)
        kb = jnp.dot(kb_ref[gi, 1], kc_ref[gi, 1], preferred_element_type=F32, precision=lax.Precision.HIGHEST)
        for s in range(chunk):
            fwd = kf if s == 0 else jnp.where(lane >= s * n_ch, pltpu.roll(kf, s * n_ch, 1), 0.0)
            back = chunk - 1 - s
            bwd = kb if back == 0 else jnp.where(lane < (s + 1) * n_ch, pltpu.roll(kb, width - back * n_ch, 1), 0.0)
            toep_ref[gi, s * n_ch:(s + 1) * n_ch, :] = (fwd + bwd).astype(BF16)
        u = u_ref[gi]
        y = jnp.dot(u, toep_ref[gi], preferred_element_type=F32)
        z = jnp.dot(u, win_ref[gi], preferred_element_type=F32)
        half = z.shape[1] // 2
        zr, zi = z[:, :half], z[:, half:]
        dec = decay_ref[gi]
        ar, ai, sr, si = dec[0:1], dec[1:2], dec[2:3], dec[3:4]
        is_fwd = lax.broadcasted_iota(jnp.int32, (rb, half), 1) < half // 2

        xr = jnp.zeros((rb, half), F32)
        xi = jnp.zeros((rb, half), F32)
        ent_r, ent_i = [], []
        for j in range(n_steps):
            jb = n_steps - 1 - j
            ent_r.append(xr)
            ent_i.append(xi)
            zrj = jnp.where(is_fwd, zr[j * rb:(j + 1) * rb], zr[jb * rb:(jb + 1) * rb])
            zij = jnp.where(is_fwd, zi[j * rb:(j + 1) * rb], zi[jb * rb:(jb + 1) * rb])
            xr, xi = ar * xr - ai * xi + zrj, ar * xi + ai * xr + zij

        sub = lax.broadcasted_iota(jnp.int32, (SUBLANES, half), 0)
        fwd8 = lax.broadcasted_iota(jnp.int32, (SUBLANES, half), 1) < half // 2
        keep = (fwd8 & (sub > 0)) | (jnp.logical_not(fwd8) & (sub < SUBLANES - 1))
        car_r, car_i = [], []
        for b in range(bsz):
            er, ei = xr[b * SUBLANES:(b + 1) * SUBLANES], xi[b * SUBLANES:(b + 1) * SUBLANES]
            cr = jnp.zeros((SUBLANES, half), F32)
            ci = jnp.zeros((SUBLANES, half), F32)
            for _ in range(SUBLANES - 1):
                tr, ti = sr * cr - si * ci + er, sr * ci + si * cr + ei
                cr = jnp.where(keep, jnp.where(fwd8, pltpu.roll(tr, 1, 0), pltpu.roll(tr, SUBLANES - 1, 0)), 0.0)
                ci = jnp.where(keep, jnp.where(fwd8, pltpu.roll(ti, 1, 0), pltpu.roll(ti, SUBLANES - 1, 0)), 0.0)
            car_r.append(cr)
            car_i.append(ci)
        gr = car_r[0] if bsz == 1 else jnp.concatenate(car_r, axis=0)
        gi_ = car_i[0] if bsz == 1 else jnp.concatenate(car_i, axis=0)

        pr_all, pi_all = pow_ref[gi, 0], pow_ref[gi, 1]
        rows_r, rows_i = [], []
        for j in range(n_steps):
            jb = n_steps - 1 - j
            lr = jnp.where(is_fwd, ent_r[j], ent_r[jb])
            li = jnp.where(is_fwd, ent_i[j], ent_i[jb])
            pr, pi = pr_all[j:j + 1], pi_all[j:j + 1]
            rows_r.append(lr + pr * gr - pi * gi_)
            rows_i.append(li + pr * gi_ + pi * gr)
        state = jnp.concatenate([jnp.concatenate(rows_r, axis=0), jnp.concatenate(rows_i, axis=0)], axis=1)
        y = y + jnp.dot(state.astype(BF16), wout_ref[gi], preferred_element_type=F32)
        y = y + u.astype(F32) * dvec_ref[gi]
        o_ref[gi] = y.astype(o_ref.dtype)


def _s5_mix(u, tables, bsz, *, gb=4, name):
    kb_cat, kc_cat, w_in, w_out, decay, powers, dvec = tables
    T, width = u.shape
    S = T // bsz
    G = w_in.shape[0]
    N = width // G
    L = S5_CHUNK
    n_steps = S // (L * S5_SEGMENTS)
    rows = n_steps * bsz * S5_SEGMENTS
    gb = _pick_tile(G, gb, 1)
    ug = u.reshape(bsz, S5_SEGMENTS, n_steps, L, G, N)
    ug = jnp.transpose(ug, (4, 2, 0, 1, 3, 5)).reshape(G, rows, L * N)
    blk = lambda *shape: pl.BlockSpec((gb,) + shape, lambda g: (g,) + (0,) * len(shape))
    vmem = gb * (4 * rows * L * N * 2 + (L * N) ** 2 * 2 + 4 * kc_cat.shape[2] * L * N * 4
                 + 8 * L * N * w_in.shape[2] * 2) + (24 << 20)
    yg = pl.pallas_call(
        functools.partial(_s5_kernel, n_steps=n_steps, bsz=bsz, gb=gb),
        grid=(G // gb,),
        in_specs=[blk(rows, L * N), blk(*kb_cat.shape[1:]), blk(*kc_cat.shape[1:]),
                  blk(L * N, w_in.shape[2]), blk(w_out.shape[1], L * N),
                  blk(4, decay.shape[2]), blk(2, n_steps, powers.shape[3]), blk(1, L * N)],
        out_specs=blk(rows, L * N),
        out_shape=jax.ShapeDtypeStruct((G, rows, L * N), BF16),
        scratch_shapes=[pltpu.VMEM((gb, L * N, L * N), BF16)],
        compiler_params=_params(("parallel",), vmem),
        name=name,
    )(ug, kb_cat, kc_cat, w_in, w_out, decay, powers, dvec)
    yg = yg.reshape(G, n_steps, bsz, S5_SEGMENTS, L, N)
    return jnp.transpose(yg, (2, 3, 1, 4, 0, 5)).reshape(T, width)


def _s5_glu_kernel(y_ref, w_ref, b_ref, o_ref):
    g = jax.nn.gelu(y_ref[...].astype(F32), approximate=True)
    z = jnp.dot(g.astype(BF16), w_ref[...], preferred_element_type=F32) + b_ref[...]
    o_ref[...] = (g * jax.nn.sigmoid(z)).astype(o_ref.dtype)


def _s5_glu(y, w, b, *, tm=1024, name):
    M, W = y.shape
    w, layer = w
    tm = _pick_tile(M, tm, SUBLANES)
    vmem = 4 * tm * W * 2 + 2 * W * W * 2 + 4 * tm * W * 4 + (4 << 20)
    return pl.pallas_call(
        _s5_glu_kernel,
        grid=(M // tm,),
        in_specs=[pl.BlockSpec((tm, W), lambda i: (i, 0)),
                  pl.BlockSpec((None, W, W), lambda i: (layer, 0, 0)),
                  pl.BlockSpec((1, W), lambda i: (0, 0))],
        out_specs=pl.BlockSpec((tm, W), lambda i: (i, 0)),
        out_shape=jax.ShapeDtypeStruct((M, W), BF16),
        compiler_params=_params(("parallel",), vmem),
        name=name,
    )(y, w, b.reshape(1, W).astype(F32))


def _slab_start(i, tq, half, slab, sub_len):
    assert tq % half == 0 and (sub_len - slab) % half == 0
    return half * jnp.clip(i * (tq // half) - 1, 0, (sub_len - slab) // half)


def _dilated_kernel(q_ref, k_ref, v_ref, o_ref, lse_ref, *, sub_len, half, nh):
    i = pl.program_id(1)
    tq = q_ref.shape[0]
    slab = k_ref.shape[0]
    dh = q_ref.shape[1] // nh
    assert dh == LANES
    sq = 2 * half
    win = sq + 2 * half
    start = _slab_start(i, tq, half, slab, sub_len)
    head_lane = lax.broadcasted_iota(jnp.int32, (sq, nh), 1)
    ones = jnp.ones((win, LANES), BF16)
    for j in range(tq // sq):
        off = pl.multiple_of(jnp.clip(i * tq + j * sq - half - start, 0, slab - win), half)
        qpos = i * tq + j * sq + lax.broadcasted_iota(jnp.int32, (sq, win), 0)
        kpos = start + off + lax.broadcasted_iota(jnp.int32, (sq, win), 1)
        valid = jnp.abs(kpos - qpos) <= half
        rows = slice(j * sq, (j + 1) * sq)
        lse_all = jnp.zeros((sq, nh), F32)
        for h in range(nh):
            sl = slice(h * dh, (h + 1) * dh)
            s = lax.dot_general(q_ref[rows, sl], k_ref[pl.ds(off, win), sl], (((1,), (1,)), ((), ())),
                                preferred_element_type=F32)
            s = jnp.where(valid, s, MASK_VALUE)
            m = jnp.max(s, axis=-1, keepdims=True)
            e = jnp.exp2(s - m).astype(BF16)
            ov = jnp.dot(e, jnp.concatenate([v_ref[pl.ds(off, win), sl], ones], axis=1),
                         preferred_element_type=F32)
            den = ov[:, dh:]
            o_ref[rows, sl] = (ov[:, :dh] / den).astype(o_ref.dtype)
            lse_all = jnp.where(head_lane == h, m * LN_2 + jnp.log(den[:, :nh]), lse_all)
        lse_ref[rows, :] = lse_all


def _dilated_branch(q, k, v, cols, D, n_seq, *, half, nh, name):
    T = q.shape[0]
    sub_len = T // n_seq
    sq = 2 * half
    n_sub = max(n for n in (4, 2, 1) if sub_len % (n * sq) == 0 and n * sq + 2 * half <= sub_len)
    tq = n_sub * sq
    slab = tq + 2 * half
    nq = sub_len // tq
    cq, ck, cv = cols

    def kv_spec(c):
        return pl.BlockSpec((pl.Element(slab), pl.Element(D)),
                            lambda s, i: (pl.multiple_of(s * sub_len + _slab_start(i, tq, half, slab, sub_len),
                                                         half), c * D))

    vmem = 2 * (2 * tq * D * 2 + 2 * slab * D * 2) + (16 << 20)
    return pl.pallas_call(
        functools.partial(_dilated_kernel, sub_len=sub_len, half=half, nh=nh),
        grid=(n_seq, nq),
        in_specs=[pl.BlockSpec((tq, D), lambda s, i: (s * nq + i, cq)), kv_spec(ck), kv_spec(cv)],
        out_specs=[pl.BlockSpec((tq, D), lambda s, i: (s * nq + i, 0)),
                   pl.BlockSpec((tq, nh), lambda s, i: (s * nq + i, 0))],
        out_shape=[jax.ShapeDtypeStruct((T, D), BF16), jax.ShapeDtypeStruct((T, nh), F32)],
        compiler_params=_params(("parallel", "parallel"), vmem),
        name=name,
    )(q, k, v)


def _residue_perm(dil):
    per = PERM_GROUP // dil
    rows = jnp.arange(PERM_GROUP)
    src = (rows % per) * dil + rows // per
    return (src[:, None] == rows[None, :]).astype(BF16)


def _to_residue_kernel(*refs, dils, n_in):
    n_d = len(dils)
    p_refs, x_refs, o_refs = refs[:n_d], refs[n_d:n_d + n_in], refs[n_d + n_in:]
    for xi, x_ref in enumerate(x_refs):
        for g in range(x_ref.shape[0] // PERM_GROUP):
            xg = x_ref[g * PERM_GROUP:(g + 1) * PERM_GROUP, :]
            for di, dil in enumerate(dils):
                o_ref = o_refs[di * n_in + xi]
                per = PERM_GROUP // dil
                y = jnp.dot(p_refs[di][...], xg, preferred_element_type=F32).astype(o_ref.dtype)
                for r in range(dil):
                    o_ref[r, g * per:(g + 1) * per, :] = y[r * per:(r + 1) * per]


def _to_residue(x, cols, width, dils, bsz, *, tb=512, name):
    T = x.shape[0]
    S = T // bsz
    tb = _pick_tile(S, tb, PERM_GROUP)
    nb = S // tb
    n_in = len(cols)
    perm_spec = pl.BlockSpec((PERM_GROUP, PERM_GROUP), lambda b, i: (0, 0))
    outs = pl.pallas_call(
        functools.partial(_to_residue_kernel, dils=tuple(dils), n_in=n_in),
        grid=(bsz, nb),
        in_specs=[perm_spec] * len(dils)
                 + [pl.BlockSpec((tb, width), lambda b, i, c=c: (b * nb + i, c)) for c in cols],
        out_specs=[pl.BlockSpec((None, dil, tb // dil, width), lambda b, i: (b, 0, i, 0))
                   for dil in dils for _ in cols],
        out_shape=[jax.ShapeDtypeStruct((bsz, dil, S // dil, width), x.dtype) for dil in dils for _ in cols],
        compiler_params=_params(("parallel", "parallel"),
                                (1 + len(dils)) * n_in * 2 * tb * width * 2 + (12 << 20)),
        name=name,
    )(*[_residue_perm(dil) for dil in dils], *([x] * n_in))
    outs = [o.reshape(T, width) for o in outs]
    return [outs[di * n_in:(di + 1) * n_in] for di in range(len(dils))]


def _from_residue_merge_kernel(p_ref, x_ref, lx_ref, prev_ref, lp_ref, o_ref, lo_ref, *, dil, nh):
    per = PERM_GROUP // dil
    dh = o_ref.shape[1] // nh
    lp, lx = lp_ref[...], lx_ref[...]
    mx = jnp.maximum(lp, lx)
    ep, ex = jnp.exp(lp - mx), jnp.exp(lx - mx)
    wx = ex / (ep + ex)
    lo_ref[...] = mx + jnp.log(ep + ex)
    for g in range(o_ref.shape[0] // PERM_GROUP):
        rows = slice(g * PERM_GROUP, (g + 1) * PERM_GROUP)
        xg = jnp.concatenate([x_ref[r, g * per:(g + 1) * per, :] for r in range(dil)], axis=0)
        y = jnp.dot(p_ref[...], xg, preferred_element_type=F32)
        for h in range(nh):
            sl = slice(h * dh, (h + 1) * dh)
            prev = prev_ref[rows, sl].astype(F32)
            o_ref[rows, sl] = (prev + wx[rows, h:h + 1] * (y[:, sl] - prev)).astype(o_ref.dtype)


def _from_residue_merge(x, lse_x, prev, lse_prev, dil, bsz, *, nh, tb=512, name):
    T, width = x.shape
    S = T // bsz
    tb = _pick_tile(S, tb, PERM_GROUP)
    nb = S // tb
    row = lambda w: pl.BlockSpec((tb, w), lambda b, i: (b * nb + i, 0))
    return pl.pallas_call(
        functools.partial(_from_residue_merge_kernel, dil=dil, nh=nh),
        grid=(bsz, nb),
        in_specs=[pl.BlockSpec((PERM_GROUP, PERM_GROUP), lambda b, i: (0, 0)),
                  pl.BlockSpec((None, dil, tb // dil, width), lambda b, i: (b, 0, i, 0)),
                  row(nh), row(width), row(nh)],
        out_specs=[row(width), row(nh)],
        out_shape=[jax.ShapeDtypeStruct((T, width), x.dtype), jax.ShapeDtypeStruct((T, nh), F32)],
        compiler_params=_params(("parallel", "parallel"), 6 * tb * width * 2 + 4 * tb * width * 4 + (8 << 20)),
        name=name,
    )(_residue_perm(dil).T, x.reshape(bsz, dil, S // dil, width), lse_x, prev, lse_prev)


def _final_norm_kernel(x_ref, g_ref, o_ref):
    o_ref[...] = _rms_normalize(x_ref[...], g_ref[...])


def _final_norm(x, g, *, tm=1024, name):
    M, D = x.shape
    tm = _pick_tile(M, tm, SUBLANES)
    return pl.pallas_call(
        _final_norm_kernel,
        grid=(M // tm,),
        in_specs=[pl.BlockSpec((tm, D), lambda i: (i, 0)), pl.BlockSpec((1, D), lambda i: (0, 0))],
        out_specs=pl.BlockSpec((tm, D), lambda i: (i, 0)),
        out_shape=jax.ShapeDtypeStruct((M, D), F32),
        compiler_params=_params(("parallel",), 6 * tm * D * 4 + (4 << 20)),
        name=name,
    )(x, g.reshape(1, D).astype(F32))


def _even_mixer(h, g, w_in, w_out, s5_tabs, glu_w, glu_b, lam_vec, subln_g, lambda_init, rope_tabs, bsz, tag):
    T, _ = h.shape
    S = T // bsz
    s5_width = glu_w[0].shape[1]
    width = (w_in[0].shape[2] - s5_width) // 3
    n_heads = width // (2 * HEAD_DIM)
    segs = ((s5_width, s5_width + width, True, HEAD_DIM ** -0.5 * LOG2_E),
            (s5_width + width, s5_width + 2 * width, True, 1.0))
    proj = _norm_matmul(h, g, w_in, segs=segs, rope_tabs=rope_tabs, seq=S, name=f"even_in_{tag}")
    y_s5 = _s5_mix(proj[:, :s5_width], s5_tabs, bsz, name=f"s5_{tag}")
    y_s5 = _s5_glu(y_s5, glu_w, glu_b, name=f"s5_glu_{tag}")
    y_diff = _diff_attn(proj, lam_vec, subln_g, bsz, s5_width=s5_width, n_heads=n_heads,
                        lambda_init=lambda_init, tk=1024, unroll=8, name=f"diff_attn_{tag}")
    return _matmul_res([y_s5, y_diff], w_out, h, tn=1024, name=f"even_out_{tag}")


def _odd_mixer(h, g, w_qkv, w_out, rope_tabs, bsz, tag):
    T, D = h.shape
    S = T // bsz
    nh = D // HEAD_DIM
    segs = ((0, D, True, HEAD_DIM ** -0.5 * LOG2_E), (D, 2 * D, True, 1.0))
    qkv = _norm_matmul(h, g, w_qkv, segs=segs, rope_tabs=rope_tabs, seq=S, name=f"odd_in_{tag}")
    dils = [dil for _, dil in DILATED_BRANCHES if dil > 1]
    residue = dict(zip(dils, _to_residue(qkv, (0, 1, 2), D, dils, bsz, name=f"to_residue_{tag}")))
    merged = lse_merged = None
    for window, dil in sorted(DILATED_BRANCHES, key=lambda wd: wd[1]):
        half = window // (2 * dil)
        sub_len = S // dil
        if dil == 1:
            merged, lse_merged = _dilated_branch(qkv, qkv, qkv, (0, 1, 2), D, bsz, half=half, nh=nh,
                                                 name=f"dilated_{dil}_{tag}")
        else:
            qr, kr, vr = residue[dil]
            o, lse = _dilated_branch(qr, kr, vr, (0, 0, 0), D, bsz * dil, half=half, nh=nh,
                                     name=f"dilated_{dil}_{tag}")
            lse = jnp.transpose(lse.reshape(bsz, dil, sub_len, nh), (0, 2, 1, 3)).reshape(T, nh)
            merged, lse_merged = _from_residue_merge(o, lse, merged, lse_merged, dil, bsz, nh=nh,
                                                     name=f"from_residue_{dil}_{tag}")
    return _matmul_res([merged], w_out, h, tn=1024, name=f"odd_out_{tag}")


def kernel(x, mem, norm_mix_g, norm_xa_g, norm_mem_g, xa_wq, xa_wkv, xa_wo, norm_ffn_g, ffn_w13, ffn_w2, ab_w_in, ab_w_out, s5_lambda_re, s5_lambda_im, s5_log_step, s5_b_re, s5_b_im, s5_c_re, s5_c_im, s5_d, s5_glu_w, s5_glu_b, diff_lambda, diff_subln_g, c_w_qkv, c_w_out, final_norm_g):
    bsz, S, D = x.shape
    T = bsz * S
    depth = norm_mix_g.shape[0]
    mem_len = mem.shape[1]
    rope_tabs = _rope_tables(S)
    n_steps = S // (S5_CHUNK * S5_SEGMENTS)
    s5_width = s5_glu_w.shape[1]
    diff_width = (ab_w_in.shape[2] - s5_width) // 3
    ab_w_in = _cast_rope_columns(ab_w_in, s5_width, s5_width + 2 * diff_width, name="cast_even_in")
    c_w_qkv = _cast_rope_columns(c_w_qkv, 0, 2 * D, name="cast_odd_in")
    ab_w_out, s5_glu_w, c_w_out, xa_wq, xa_wkv, xa_wo, ffn_w13, ffn_w2 = (
        w.astype(BF16) for w in (ab_w_out, s5_glu_w, c_w_out, xa_wq, xa_wkv, xa_wo, ffn_w13, ffn_w2))

    h = x.reshape(T, D)
    mem2 = mem.reshape(bsz * mem_len, D)
    for layer in range(depth):
        i = layer // 2
        if layer % 2 == 0:
            lambda_init = 0.8 - 0.6 * math.exp(-0.3 * layer)
            tabs = _s5_tables(s5_lambda_re[i], s5_lambda_im[i], s5_log_step[i], s5_b_re[i], s5_b_im[i],
                              s5_c_re[i], s5_c_im[i], s5_d[i], n_steps)
            h = _even_mixer(h, norm_mix_g[layer], (ab_w_in, i), (ab_w_out, i), tabs, (s5_glu_w, i),
                            s5_glu_b[i], diff_lambda[i], diff_subln_g[i], lambda_init, rope_tabs, bsz, layer)
        else:
            h = _odd_mixer(h, norm_mix_g[layer], (c_w_qkv, i), (c_w_out, i), rope_tabs, bsz, layer)

        q = _norm_matmul(h, norm_xa_g[layer], (xa_wq, layer), name=f"xa_q_{layer}")
        kv = _norm_matmul(mem2, norm_mem_g[layer], (xa_wkv, layer),
                          segs=((0, D, False, (D // XA_HEADS) ** -0.5),), name=f"xa_kv_{layer}")
        o = _cross_attn(q, kv, bsz, name=f"xa_core_{layer}")
        h = _matmul_res([o], (xa_wo, layer), h, tn=1024, name=f"xa_out_{layer}")

        hid = _swiglu_up(h, norm_ffn_g[layer], (ffn_w13, layer), name=f"ffn_up_{layer}")
        h = _matmul_res([hid], (ffn_w2, layer), h, tn=512, name=f"ffn_down_{layer}")
    return _final_norm(h, final_norm_g, name="final_norm").reshape(bsz, S, D)
```

```python
import functools
import math

import jax
import jax.numpy as jnp
from jax import lax
from jax.experimental import pallas as pl
from jax.experimental.pallas import tpu as pltpu

F32 = jnp.float32
BF16 = jnp.bfloat16

NORM_EPS = 1e-6
MASK_VALUE = -1e30
LOG2_E = math.log2(math.e)
LN_2 = math.log(2.0)
ROPE_THETA = 500000.0
ROPE_HALF = 16
HEAD_DIM = 128
XA_HEADS = 4
DILATED_BRANCHES = ((128, 1), (512, 4), (2048, 16))
S5_CHUNK = 32
S5_SEGMENTS = 8
PERM_GROUP = 256
LANES = 128
SUBLANES = 8
VMEM_CAP_BYTES = 56 * 1024 * 1024


def _pick_tile(n, pref, quantum):
    t = (min(n, pref) // quantum) * quantum
    while t >= quantum:
        if n % t == 0:
            return t
        t -= quantum
    return n


def _params(semantics, vmem_bytes):
    return pltpu.CompilerParams(
        dimension_semantics=semantics,
        vmem_limit_bytes=int(min(VMEM_CAP_BYTES, max(32 * 1024 * 1024, vmem_bytes))))


def _rms_normalize(x, g):
    ms = jnp.mean(x * x, axis=-1, keepdims=True)
    return (x * lax.rsqrt(ms + NORM_EPS)) * g


def _rope_lanes(y, cf, sf):
    outs = []
    for c in range(y.shape[1] // LANES):
        yc = y[:, c * LANES:(c + 1) * LANES]
        outs.append(yc * cf + pltpu.roll(yc, LANES // 2, 1) * sf)
    return outs[0] if len(outs) == 1 else jnp.concatenate(outs, axis=1)


def _cast_rope_columns_kernel(p_ref, w_ref, o_ref, *, lo_blk, hi_blk):
    j = pl.program_id(2)
    w = w_ref[...].astype(BF16)
    roped = jnp.logical_and(j >= lo_blk, j < hi_blk)

    @pl.when(roped)
    def _():
        o_ref[...] = jnp.dot(w, p_ref[...], preferred_element_type=F32).astype(BF16)

    @pl.when(jnp.logical_not(roped))
    def _():
        o_ref[...] = w


def _cast_rope_columns(w, lo, hi, *, tr=1024, tc=512, name):
    n_layers, K, N = w.shape
    tr = _pick_tile(K, tr, SUBLANES)
    tc = _pick_tile(math.gcd(N, lo, hi), tc, LANES)
    r, h = ROPE_HALF, LANES // 2
    src = jnp.concatenate([jnp.arange(0, r), jnp.arange(2 * r, h + r), jnp.arange(r, 2 * r),
                           jnp.arange(h + r, LANES)])
    perm = (jnp.arange(LANES)[:, None] == src[None, :]).astype(BF16)
    perm = jnp.kron(jnp.eye(tc // LANES, dtype=BF16), perm)
    return pl.pallas_call(
        functools.partial(_cast_rope_columns_kernel, lo_blk=lo // tc, hi_blk=hi // tc),
        grid=(n_layers, K // tr, N // tc),
        in_specs=[pl.BlockSpec((tc, tc), lambda l, i, j: (0, 0)),
                  pl.BlockSpec((None, tr, tc), lambda l, i, j: (l, i, j))],
        out_specs=pl.BlockSpec((None, tr, tc), lambda l, i, j: (l, i, j)),
        out_shape=jax.ShapeDtypeStruct(w.shape, BF16),
        compiler_params=_params(("parallel", "parallel", "parallel"), 8 * tr * tc * 4 + (4 << 20)),
        name=name,
    )(perm, w)


def _rope_tables(seq):
    pos = jnp.arange(seq, dtype=F32)
    inv = ROPE_THETA ** (-jnp.arange(0, 2 * ROPE_HALF, 2, dtype=F32) / (2 * ROPE_HALF))
    ang = pos[:, None] * inv[None, :]
    cos, sin = jnp.cos(ang), jnp.sin(ang)
    gap = LANES // 2 - ROPE_HALF
    cf = jnp.concatenate([cos, jnp.ones((seq, gap), F32), cos, jnp.ones((seq, gap), F32)], axis=1)
    sf = jnp.concatenate([-sin, jnp.zeros((seq, gap), F32), sin, jnp.zeros((seq, gap), F32)], axis=1)
    return cf, sf


def _norm_matmul_kernel(*refs, segs, tn, use_rope):
    if use_rope:
        x_ref, g_ref, w_ref, cf_ref, sf_ref, o_ref, xn_ref = refs
    else:
        x_ref, g_ref, w_ref, o_ref, xn_ref = refs
    j = pl.program_id(1)

    @pl.when(j == 0)
    def _():
        xn_ref[...] = _rms_normalize(x_ref[...], g_ref[...]).astype(BF16)

    y = jnp.dot(xn_ref[...], w_ref[...], preferred_element_type=F32)
    for lo, hi, rope, scale in segs:
        @pl.when(jnp.logical_and(j >= lo // tn, j < hi // tn))
        def _(rope=rope, scale=scale):
            z = y
            if rope:
                cf, sf = cf_ref[...], sf_ref[...]
                if scale != 1.0:
                    cf, sf = cf * scale, sf * scale
                z = _rope_lanes(z, cf, sf)
            elif scale != 1.0:
                z = z * scale
            o_ref[...] = z.astype(o_ref.dtype)


def _norm_matmul(x, g, w, *, segs=(), rope_tabs=None, seq=None, tm=1024, tn=1024, name):
    M, D = x.shape
    w, layer = w
    N = w.shape[2]
    tm = _pick_tile(M if seq is None else seq, tm, SUBLANES)
    tn = _pick_tile(math.gcd(N, *[b for s in segs for b in s[:2]]), tn, LANES)
    full, pos = [], 0
    for lo, hi, rope, scale in sorted(segs):
        if lo > pos:
            full.append((pos, lo, False, 1.0))
        full.append((lo, hi, rope, scale))
        pos = hi
    if pos < N:
        full.append((pos, N, False, 1.0))
    for lo, hi, _, _ in full:
        assert lo % tn == 0 and hi % tn == 0, (lo, hi, tn)
    use_rope = any(s[2] for s in full)
    in_specs = [pl.BlockSpec((tm, D), lambda i, j: (i, 0)),
                pl.BlockSpec((1, D), lambda i, j: (0, 0)),
                pl.BlockSpec((None, D, tn), lambda i, j: (layer, 0, j))]
    args = [x, g.reshape(1, D).astype(F32), w]
    if use_rope:
        nseq = seq // tm
        in_specs += [pl.BlockSpec((tm, LANES), lambda i, j: (i % nseq, 0))] * 2
        args += list(rope_tabs)
    vmem = 2 * tm * D * 4 + tm * D * 2 + 2 * D * tn * 2 + 2 * tm * tn * 2 + 3 * tm * tn * 4 + (4 << 20)
    return pl.pallas_call(
        functools.partial(_norm_matmul_kernel, segs=tuple(full), tn=tn, use_rope=use_rope),
        grid=(M // tm, N // tn),
        in_specs=in_specs,
        out_specs=pl.BlockSpec((tm, tn), lambda i, j: (i, j)),
        out_shape=jax.ShapeDtypeStruct((M, N), BF16),
        scratch_shapes=[pltpu.VMEM((tm, D), BF16)],
        compiler_params=_params(("parallel", "arbitrary"), vmem),
        name=name,
    )(*args)


def _matmul_res_kernel(*refs, n_a):
    a_refs, w_refs = refs[:n_a], refs[n_a:2 * n_a]
    res_ref, o_ref = refs[2 * n_a], refs[2 * n_a + 1]
    acc = res_ref[...]
    for a_ref, w_ref in zip(a_refs, w_refs):
        acc = acc + jnp.dot(a_ref[...], w_ref[...], preferred_element_type=F32)
    o_ref[...] = acc


def _matmul_res(a_list, w, res, *, tm=1024, tn=512, name):
    n_a = len(a_list)
    M, K = a_list[0].shape
    w, layer = w
    N = w.shape[2]
    tm = _pick_tile(M, tm, SUBLANES)
    tn = _pick_tile(N, tn, LANES)
    in_specs = [pl.BlockSpec((tm, K), lambda i, j: (i, 0)) for _ in range(n_a)]
    in_specs += [pl.BlockSpec((None, K, tn), lambda i, j, r=r: (layer, r, j)) for r in range(n_a)]
    in_specs += [pl.BlockSpec((tm, tn), lambda i, j: (i, j))]
    vmem = n_a * (2 * tm * K * 2 + 2 * K * tn * 2) + 5 * tm * tn * 4 + (4 << 20)
    return pl.pallas_call(
        functools.partial(_matmul_res_kernel, n_a=n_a),
        grid=(M // tm, N // tn),
        in_specs=in_specs,
        out_specs=pl.BlockSpec((tm, tn), lambda i, j: (i, j)),
        out_shape=jax.ShapeDtypeStruct((M, N), F32),
        compiler_params=_params(("parallel", "parallel"), vmem),
        name=name,
    )(*a_list, *([w] * n_a), res)


def _swiglu_up_kernel(x_ref, g_ref, w1_ref, w3_ref, o_ref, xn_ref):
    @pl.when(pl.program_id(1) == 0)
    def _():
        xn_ref[...] = _rms_normalize(x_ref[...], g_ref[...]).astype(BF16)

    xn = xn_ref[...]
    a = jnp.dot(xn, w1_ref[...], preferred_element_type=F32)
    b = jnp.dot(xn, w3_ref[...], preferred_element_type=F32)
    o_ref[...] = (a * jax.nn.sigmoid(a) * b).astype(o_ref.dtype)


def _swiglu_up(x, g, w13, *, tm=1024, tn=512, name):
    M, D = x.shape
    w13, layer = w13
    H = w13.shape[2] // 2
    tm = _pick_tile(M, tm, SUBLANES)
    tn = _pick_tile(H, tn, LANES)
    nj = H // tn
    vmem = 2 * tm * D * 4 + tm * D * 2 + 4 * D * tn * 2 + 2 * tm * tn * 2 + 4 * tm * tn * 4 + (4 << 20)
    return pl.pallas_call(
        _swiglu_up_kernel,
        grid=(M // tm, nj),
        in_specs=[pl.BlockSpec((tm, D), lambda i, j: (i, 0)),
                  pl.BlockSpec((1, D), lambda i, j: (0, 0)),
                  pl.BlockSpec((None, D, tn), lambda i, j: (layer, 0, j)),
                  pl.BlockSpec((None, D, tn), lambda i, j: (layer, 0, j + nj))],
        out_specs=pl.BlockSpec((tm, tn), lambda i, j: (i, j)),
        out_shape=jax.ShapeDtypeStruct((M, H), BF16),
        scratch_shapes=[pltpu.VMEM((tm, D), BF16)],
        compiler_params=_params(("parallel", "arbitrary"), vmem),
        name=name,
    )(x, g.reshape(1, D).astype(F32), w13, w13)


def _cross_attn_kernel(x_ref, g_ref, wq_ref, k_ref, v_ref, o_ref, *, nh):
    dh = o_ref.shape[1] // nh
    xn = _rms_normalize(x_ref[...], g_ref[...]).astype(BF16)
    for h in range(nh):
        sl = slice(h * dh, (h + 1) * dh)
        q = jnp.dot(xn, wq_ref[:, sl], preferred_element_type=F32).astype(BF16)
        s = lax.dot_general(q, k_ref[:, sl], (((1,), (1,)), ((), ())), preferred_element_type=F32)
        m = jnp.max(s, axis=-1, keepdims=True)
        e = jnp.exp(s - m)
        den = jnp.sum(e, axis=-1, keepdims=True)
        o = jnp.dot(e.astype(BF16), v_ref[:, sl], preferred_element_type=F32)
        o_ref[:, sl] = (o / den).astype(o_ref.dtype)


def _cross_attn(x, g, wq, kv, bsz, *, tq=1024, name):
    T, D = x.shape
    S = T // bsz
    wq, layer = wq
    mem_len = kv.shape[0] // bsz
    tq = _pick_tile(S, tq, SUBLANES)
    nq = S // tq
    kv3 = kv.reshape(bsz, mem_len, 2 * D)
    vmem = 2 * tq * D * 4 + 2 * D * D * 2 + 4 * mem_len * D * 2 + 2 * tq * D * 2 + 6 * tq * D * 2 + (8 << 20)
    return pl.pallas_call(
        functools.partial(_cross_attn_kernel, nh=XA_HEADS),
        grid=(bsz, nq),
        in_specs=[pl.BlockSpec((tq, D), lambda b, i: (b * nq + i, 0)),
                  pl.BlockSpec((1, D), lambda b, i: (0, 0)),
                  pl.BlockSpec((None, D, D), lambda b, i: (layer, 0, 0)),
                  pl.BlockSpec((None, mem_len, D), lambda b, i: (b, 0, 0)),
                  pl.BlockSpec((None, mem_len, D), lambda b, i: (b, 0, 1))],
        out_specs=pl.BlockSpec((tq, D), lambda b, i: (b * nq + i, 0)),
        out_shape=jax.ShapeDtypeStruct((T, D), BF16),
        compiler_params=_params(("parallel", "parallel"), vmem),
        name=name,
    )(x, g.reshape(1, D).astype(F32), wq, kv3, kv3)


def _diff_attn_kernel(q_ref, k_ref, v_ref, lv_ref, sg_ref, o_ref, vt_ref, *, tk, unroll, lambda_init):
    tq = q_ref.shape[0]
    seq = k_ref.shape[0]
    dv = v_ref.shape[1]
    dh = dv // 2

    @pl.when(pl.program_id(2) == 0)
    def _():
        for c in range(seq // tk):
            vt_ref[:, c * tk:(c + 1) * tk] = v_ref[c * tk:(c + 1) * tk, :].astype(F32).T.astype(BF16)

    q = q_ref[...]
    nk = seq // tk

    def scores(kk):
        kc = k_ref[pl.ds(pl.multiple_of(kk * tk, tk), tk), :]
        return tuple(lax.dot_general(kc[:, c * dh:(c + 1) * dh], q[:, c * dh:(c + 1) * dh],
                                     (((1,), (1,)), ((), ())), preferred_element_type=F32)
                     for c in range(2))

    def body(kk, carry):
        st_next = scores(jnp.minimum(kk + 1, nk - 1))
        vt = vt_ref[:, pl.ds(pl.multiple_of(kk * tk, tk), tk)]
        new = []
        for c in range(2):
            st, m, l, acc = carry[c]
            m_new = jnp.maximum(m, jnp.max(st, axis=0, keepdims=True))
            alpha = jnp.exp2(m - m_new)
            pt = jnp.exp2(st - m_new)
            l = alpha * l + jnp.sum(pt, axis=0, keepdims=True)
            acc = alpha * acc + jnp.dot(vt, pt.astype(BF16), preferred_element_type=F32)
            new.append((st_next[c], m_new, l, acc))
        return tuple(new)

    st0 = scores(0)
    init = tuple((st0[c], jnp.full((1, tq), MASK_VALUE, F32), jnp.zeros((1, tq), F32),
                  jnp.zeros((dv, tq), F32)) for c in range(2))
    (_, _, l1, a1), (_, _, l2, a2) = lax.fori_loop(0, nk, body, init, unroll=unroll)
    lv = lv_ref[...]
    lam = (jnp.exp(jnp.sum(lv[0:1] * lv[1:2], axis=-1, keepdims=True))
           - jnp.exp(jnp.sum(lv[2:3] * lv[3:4], axis=-1, keepdims=True)) + lambda_init)
    ot = a1 / l1 - lam * (a2 / l2)
    ms = jnp.mean(ot * ot, axis=0, keepdims=True)
    ot = ot * lax.rsqrt(ms + NORM_EPS)
    o_ref[...] = (ot.T * sg_ref[...] * (1.0 - lambda_init)).astype(o_ref.dtype)


def _diff_attn(proj, lam_vec, subln_g, bsz, *, s5_width, n_heads, lambda_init, tq=512, tk=512, unroll=1, name):
    T, ncol = proj.shape
    S = T // bsz
    dv = 2 * HEAD_DIM
    width = n_heads * dv
    q0, k0, v0 = s5_width // dv, (s5_width + width) // dv, (s5_width + 2 * width) // dv
    tq = _pick_tile(S, tq, SUBLANES)
    tk = _pick_tile(S, tk, LANES)
    nq = S // tq
    proj3 = proj.reshape(bsz, S, ncol)
    vmem = 4 * tq * dv * 2 + 5 * S * dv * 2 + unroll * (3 * tq * tk * 4 + 2 * tq * dv * 4) + (8 << 20)
    return pl.pallas_call(
        functools.partial(_diff_attn_kernel, tk=tk, unroll=unroll, lambda_init=lambda_init),
        grid=(bsz, n_heads, nq),
        in_specs=[pl.BlockSpec((None, tq, dv), lambda b, h, i: (b, i, q0 + h)),
                  pl.BlockSpec((None, S, dv), lambda b, h, i: (b, 0, k0 + h)),
                  pl.BlockSpec((None, S, dv), lambda b, h, i: (b, 0, v0 + h)),
                  pl.BlockSpec((4, HEAD_DIM), lambda b, h, i: (0, 0)),
                  pl.BlockSpec((1, dv), lambda b, h, i: (0, 0))],
        out_specs=pl.BlockSpec((tq, dv), lambda b, h, i: (b * nq + i, h)),
        out_shape=jax.ShapeDtypeStruct((T, width), BF16),
        scratch_shapes=[pltpu.VMEM((dv, S), BF16)],
        compiler_params=_params(("parallel", "parallel", "arbitrary"), vmem),
        name=name,
    )(proj3, proj3, proj3, lam_vec.astype(F32), subln_g.reshape(1, dv).astype(F32))


def _s5_tables(lam_re, lam_im, log_step, b_re, b_im, c_re, c_im, d_skip, n_steps):
    L = S5_CHUNK
    _, G, P = lam_re.shape
    N = b_re.shape[-1]
    lr, li = lam_re.astype(F32), lam_im.astype(F32)
    step = jnp.exp(log_step.astype(F32))[..., None]
    ar, ai = lr * step, li * step

    def cexp(t, axes=()):
        a_r, a_i = jnp.expand_dims(ar, axes), jnp.expand_dims(ai, axes)
        mag = jnp.exp(t * a_r)
        return mag * jnp.cos(t * a_i), mag * jnp.sin(t * a_i)

    def cmul(xr, xi, yr, yi):
        return xr * yr - xi * yi, xr * yi + xi * yr

    er, ei = cexp(1.0)
    den = lr * lr + li * li
    qr, qi = ((er - 1.0) * lr + ei * li) / den, (ei * lr - (er - 1.0) * li) / den
    br, bi = cmul(qr[..., None], qi[..., None], b_re.astype(F32), b_im.astype(F32))
    cr, ci = c_re.astype(F32), c_im.astype(F32)
    tau = jnp.arange(L + 1, dtype=F32)
    pr, pi = cexp(tau[None, None, :, None], axes=2)
    prt, pit = cexp(tau[None, None, None, :], axes=3)
    brt, bit = jnp.swapaxes(br, 2, 3), jnp.swapaxes(bi, 2, 3)
    crt, cit = jnp.swapaxes(cr, 2, 3), jnp.swapaxes(ci, 2, 3)

    def pow_c(d, taus):
        return cmul(prt[d][:, :, taus, None], pit[d][:, :, taus, None], crt[d][:, :, None, :], cit[d][:, :, None, :])

    kb_cat = jnp.stack([jnp.concatenate([brt[d], -bit[d]], axis=2) for d in range(2)], axis=1)
    kc_cat = jnp.stack([jnp.concatenate(pow_c(d, taus), axis=1).reshape(G, 2 * P, L * N)
                        for d, taus in ((0, slice(0, L)), (1, slice(L - 1, None, -1)))], axis=1)

    def pow_b(d, taus):
        return cmul(pr[d][:, taus, None, :], pi[d][:, taus, None, :], brt[d][:, None], bit[d][:, None])
    (fr, fi), (gr, gi) = pow_b(0, slice(L - 1, None, -1)), pow_b(1, slice(0, L))
    w_in = jnp.concatenate([fr, gr, fi, gi], axis=3).reshape(G, L * N, 4 * P)

    (of_r, of_i), (ob_r, ob_i) = pow_c(0, slice(1, L + 1)), pow_c(1, slice(L, 0, -1))
    w_out = jnp.concatenate([of_r, ob_r, -of_i, -ob_i], axis=1).reshape(G, 4 * P, L * N)

    def _lanes(z):
        return jnp.concatenate([z[0], z[1]], axis=-1)
    a_chunk, a_seg = cexp(float(L)), cexp(float(L * n_steps))
    decay = jnp.stack([_lanes(a_chunk[0]), _lanes(a_chunk[1]), _lanes(a_seg[0]), _lanes(a_seg[1])], axis=1)
    jj = float(L) * jnp.arange(n_steps, dtype=F32)
    jj = jnp.stack([jj, jj[::-1]], axis=0)[:, None, :, None]
    powers = jnp.stack([_lanes(p) for p in cexp(jj, axes=2)], axis=1)
    dvec = jnp.tile(d_skip.astype(F32).reshape(G, 1, N), (1, L, 1)).reshape(G, 1, L * N)
    return kb_cat, kc_cat, w_in.astype(BF16), w_out.astype(BF16), decay, powers, dvec


def _s5_kernel(u_ref, kb_ref, kc_ref, win_ref, wout_ref, decay_ref, pow_ref, dvec_ref, o_ref, toep_ref,
               *, n_steps, bsz, gb):
    rb = bsz * SUBLANES
    n_ch = kb_ref.shape[2]
    width = kc_ref.shape[3]
    chunk = width // n_ch
    lane = lax.broadcasted_iota(jnp.int32, (n_ch, width), 1)
    for gi in range(gb):
        kf = jnp.dot(kb_ref[gi, 0], kc_ref[gi, 0], preferred_element_type=F32, precision=lax.Precision.HIGHEST)
        kb = jnp.dot(kb_ref[gi, 1], kc_ref[gi, 1], preferred_element_type=F32, precision=lax.Precision.HIGHEST)
        for s in range(chunk):
            fwd = kf if s == 0 else jnp.where(lane >= s * n_ch, pltpu.roll(kf, s * n_ch, 1), 0.0)
            back = chunk - 1 - s
            bwd = kb if back == 0 else jnp.where(lane < (s + 1) * n_ch, pltpu.roll(kb, width - back * n_ch, 1), 0.0)
            toep_ref[gi, s * n_ch:(s + 1) * n_ch, :] = (fwd + bwd).astype(BF16)
        u = u_ref[gi]
        y = jnp.dot(u, toep_ref[gi], preferred_element_type=F32)
        z = jnp.dot(u, win_ref[gi], preferred_element_type=F32)
        half = z.shape[1] // 2
        zr, zi = z[:, :half], z[:, half:]
        dec = decay_ref[gi]
        ar, ai, sr, si = dec[0:1], dec[1:2], dec[2:3], dec[3:4]
        is_fwd = lax.broadcasted_iota(jnp.int32, (rb, half), 1) < half // 2

        xr = jnp.zeros((rb, half), F32)
        xi = jnp.zeros((rb, half), F32)
        ent_r, ent_i = [], []
        for j in range(n_steps):
            jb = n_steps - 1 - j
            ent_r.append(xr)
            ent_i.append(xi)
            zrj = jnp.where(is_fwd, zr[j * rb:(j + 1) * rb], zr[jb * rb:(jb + 1) * rb])
            zij = jnp.where(is_fwd, zi[j * rb:(j + 1) * rb], zi[jb * rb:(jb + 1) * rb])
            xr, xi = ar * xr - ai * xi + zrj, ar * xi + ai * xr + zij

        sub = lax.broadcasted_iota(jnp.int32, (SUBLANES, half), 0)
        fwd8 = lax.broadcasted_iota(jnp.int32, (SUBLANES, half), 1) < half // 2
        keep = (fwd8 & (sub > 0)) | (jnp.logical_not(fwd8) & (sub < SUBLANES - 1))
        car_r, car_i = [], []
        for b in range(bsz):
            er, ei = xr[b * SUBLANES:(b + 1) * SUBLANES], xi[b * SUBLANES:(b + 1) * SUBLANES]
            cr = jnp.zeros((SUBLANES, half), F32)
            ci = jnp.zeros((SUBLANES, half), F32)
            for _ in range(SUBLANES - 1):
                tr, ti = sr * cr - si * ci + er, sr * ci + si * cr + ei
                cr = jnp.where(keep, jnp.where(fwd8, pltpu.roll(tr, 1, 0), pltpu.roll(tr, SUBLANES - 1, 0)), 0.0)
                ci = jnp.where(keep, jnp.where(fwd8, pltpu.roll(ti, 1, 0), pltpu.roll(ti, SUBLANES - 1, 0)), 0.0)
            car_r.append(cr)
            car_i.append(ci)
        gr = car_r[0] if bsz == 1 else jnp.concatenate(car_r, axis=0)
        gi_ = car_i[0] if bsz == 1 else jnp.concatenate(car_i, axis=0)

        pr_all, pi_all = pow_ref[gi, 0], pow_ref[gi, 1]
        rows_r, rows_i = [], []
        for j in range(n_steps):
            jb = n_steps - 1 - j
            lr = jnp.where(is_fwd, ent_r[j], ent_r[jb])
            li = jnp.where(is_fwd, ent_i[j], ent_i[jb])
            pr, pi = pr_all[j:j + 1], pi_all[j:j + 1]
            rows_r.append(lr + pr * gr - pi * gi_)
            rows_i.append(li + pr * gi_ + pi * gr)
        state = jnp.concatenate([jnp.concatenate(rows_r, axis=0), jnp.concatenate(rows_i, axis=0)], axis=1)
        y = y + jnp.dot(state.astype(BF16), wout_ref[gi], preferred_element_type=F32)
        y = y + u.astype(F32) * dvec_ref[gi]
        o_ref[gi] = y.astype(o_ref.dtype)


def _s5_mix(u, tables, bsz, *, gb=4, name):
    kb_cat, kc_cat, w_in, w_out, decay, powers, dvec = tables
    T, width = u.shape
    S = T // bsz
    G = w_in.shape[0]
    N = width // G
    L = S5_CHUNK
    n_steps = S // (L * S5_SEGMENTS)
    rows = n_steps * bsz * S5_SEGMENTS
    gb = _pick_tile(G, gb, 1)
    ug = u.reshape(bsz, S5_SEGMENTS, n_steps, L, G, N)
    ug = jnp.transpose(ug, (4, 2, 0, 1, 3, 5)).reshape(G, rows, L * N)
    blk = lambda *shape: pl.BlockSpec((gb,) + shape, lambda g: (g,) + (0,) * len(shape))
    vmem = gb * (4 * rows * L * N * 2 + (L * N) ** 2 * 2 + 4 * kc_cat.shape[2] * L * N * 4
                 + 8 * L * N * w_in.shape[2] * 2) + (24 << 20)
    yg = pl.pallas_call(
        functools.partial(_s5_kernel, n_steps=n_steps, bsz=bsz, gb=gb),
        grid=(G // gb,),
        in_specs=[blk(rows, L * N), blk(*kb_cat.shape[1:]), blk(*kc_cat.shape[1:]),
                  blk(L * N, w_in.shape[2]), blk(w_out.shape[1], L * N),
                  blk(4, decay.shape[2]), blk(2, n_steps, powers.shape[3]), blk(1, L * N)],
        out_specs=blk(rows, L * N),
        out_shape=jax.ShapeDtypeStruct((G, rows, L * N), BF16),
        scratch_shapes=[pltpu.VMEM((gb, L * N, L * N), BF16)],
        compiler_params=_params(("parallel",), vmem),
        name=name,
    )(ug, kb_cat, kc_cat, w_in, w_out, decay, powers, dvec)
    yg = yg.reshape(G, n_steps, bsz, S5_SEGMENTS, L, N)
    return jnp.transpose(yg, (2, 3, 1, 4, 0, 5)).reshape(T, width)


def _s5_glu_kernel(y_ref, w_ref, b_ref, o_ref):
    g = jax.nn.gelu(y_ref[...].astype(F32), approximate=True)
    z = jnp.dot(g.astype(BF16), w_ref[...], preferred_element_type=F32) + b_ref[...]
    o_ref[...] = (g * jax.nn.sigmoid(z)).astype(o_ref.dtype)


def _s5_glu(y, w, b, *, tm=1024, name):
    M, W = y.shape
    w, layer = w
    tm = _pick_tile(M, tm, SUBLANES)
    vmem = 4 * tm * W * 2 + 2 * W * W * 2 + 4 * tm * W * 4 + (4 << 20)
    return pl.pallas_call(
        _s5_glu_kernel,
        grid=(M // tm,),
        in_specs=[pl.BlockSpec((tm, W), lambda i: (i, 0)),
                  pl.BlockSpec((None, W, W), lambda i: (layer, 0, 0)),
                  pl.BlockSpec((1, W), lambda i: (0, 0))],
        out_specs=pl.BlockSpec((tm, W), lambda i: (i, 0)),
        out_shape=jax.ShapeDtypeStruct((M, W), BF16),
        compiler_params=_params(("parallel",), vmem),
        name=name,
    )(y, w, b.reshape(1, W).astype(F32))


def _slab_start(i, tq, half, slab, sub_len):
    assert tq % half == 0 and (sub_len - slab) % half == 0
    return half * jnp.clip(i * (tq // half) - 1, 0, (sub_len - slab) // half)


def _dilated_kernel(q_ref, k_ref, v_ref, o_ref, lse_ref, *, sub_len, half, nh):
    i = pl.program_id(1)
    tq = q_ref.shape[0]
    slab = k_ref.shape[0]
    dh = q_ref.shape[1] // nh
    assert dh == LANES
    sq = 2 * half
    win = sq + 2 * half
    start = _slab_start(i, tq, half, slab, sub_len)
    head_lane = lax.broadcasted_iota(jnp.int32, (sq, nh), 1)
    ones = jnp.ones((win, LANES), BF16)
    for j in range(tq // sq):
        off = pl.multiple_of(jnp.clip(i * tq + j * sq - half - start, 0, slab - win), half)
        qpos = i * tq + j * sq + lax.broadcasted_iota(jnp.int32, (sq, win), 0)
        kpos = start + off + lax.broadcasted_iota(jnp.int32, (sq, win), 1)
        valid = jnp.abs(kpos - qpos) <= half
        rows = slice(j * sq, (j + 1) * sq)
        lse_all = jnp.zeros((sq, nh), F32)
        for h in range(nh):
            sl = slice(h * dh, (h + 1) * dh)
            s = lax.dot_general(q_ref[rows, sl], k_ref[pl.ds(off, win), sl], (((1,), (1,)), ((), ())),
                                preferred_element_type=F32)
            s = jnp.where(valid, s, MASK_VALUE)
            m = jnp.max(s, axis=-1, keepdims=True)
            e = jnp.exp2(s - m).astype(BF16)
            ov = jnp.dot(e, jnp.concatenate([v_ref[pl.ds(off, win), sl], ones], axis=1),
                         preferred_element_type=F32)
            den = ov[:, dh:]
            o_ref[rows, sl] = (ov[:, :dh] / den).astype(o_ref.dtype)
            lse_all = jnp.where(head_lane == h, m * LN_2 + jnp.log(den[:, :nh]), lse_all)
        lse_ref[rows, :] = lse_all


def _dilated_branch(q, k, v, cols, D, n_seq, *, half, nh, name):
    T = q.shape[0]
    sub_len = T // n_seq
    sq = 2 * half
    n_sub = max(n for n in (4, 2, 1) if sub_len % (n * sq) == 0 and n * sq + 2 * half <= sub_len)
    tq = n_sub * sq
    slab = tq + 2 * half
    nq = sub_len // tq
    cq, ck, cv = cols

    def kv_spec(c):
        return pl.BlockSpec((pl.Element(slab), pl.Element(D)),
                            lambda s, i: (pl.multiple_of(s * sub_len + _slab_start(i, tq, half, slab, sub_len),
                                                         half), c * D))

    vmem = 2 * (2 * tq * D * 2 + 2 * slab * D * 2) + (16 << 20)
    return pl.pallas_call(
        functools.partial(_dilated_kernel, sub_len=sub_len, half=half, nh=nh),
        grid=(n_seq, nq),
        in_specs=[pl.BlockSpec((tq, D), lambda s, i: (s * nq + i, cq)), kv_spec(ck), kv_spec(cv)],
        out_specs=[pl.BlockSpec((tq, D), lambda s, i: (s * nq + i, 0)),
                   pl.BlockSpec((tq, nh), lambda s, i: (s * nq + i, 0))],
        out_shape=[jax.ShapeDtypeStruct((T, D), BF16), jax.ShapeDtypeStruct((T, nh), F32)],
        compiler_params=_params(("parallel", "parallel"), vmem),
        name=name,
    )(q, k, v)


def _residue_perm(dil):
    per = PERM_GROUP // dil
    rows = jnp.arange(PERM_GROUP)
    src = (rows % per) * dil + rows // per
    return (src[:, None] == rows[None, :]).astype(BF16)


def _to_residue_kernel(*refs, dils, n_in):
    n_d = len(dils)
    p_refs, x_refs, o_refs = refs[:n_d], refs[n_d:n_d + n_in], refs[n_d + n_in:]
    for xi, x_ref in enumerate(x_refs):
        for g in range(x_ref.shape[0] // PERM_GROUP):
            xg = x_ref[g * PERM_GROUP:(g + 1) * PERM_GROUP, :]
            for di, dil in enumerate(dils):
                o_ref = o_refs[di * n_in + xi]
                per = PERM_GROUP // dil
                y = jnp.dot(p_refs[di][...], xg, preferred_element_type=F32).astype(o_ref.dtype)
                for r in range(dil):
                    o_ref[r, g * per:(g + 1) * per, :] = y[r * per:(r + 1) * per]


def _to_residue(x, cols, width, dils, bsz, *, tb=512, name):
    T = x.shape[0]
    S = T // bsz
    tb = _pick_tile(S, tb, PERM_GROUP)
    nb = S // tb
    n_in = len(cols)
    perm_spec = pl.BlockSpec((PERM_GROUP, PERM_GROUP), lambda b, i: (0, 0))
    outs = pl.pallas_call(
        functools.partial(_to_residue_kernel, dils=tuple(dils), n_in=n_in),
        grid=(bsz, nb),
        in_specs=[perm_spec] * len(dils)
                 + [pl.BlockSpec((tb, width), lambda b, i, c=c: (b * nb + i, c)) for c in cols],
        out_specs=[pl.BlockSpec((None, dil, tb // dil, width), lambda b, i: (b, 0, i, 0))
                   for dil in dils for _ in cols],
        out_shape=[jax.ShapeDtypeStruct((bsz, dil, S // dil, width), x.dtype) for dil in dils for _ in cols],
        compiler_params=_params(("parallel", "parallel"),
                                (1 + len(dils)) * n_in * 2 * tb * width * 2 + (12 << 20)),
        name=name,
    )(*[_residue_perm(dil) for dil in dils], *([x] * n_in))
    outs = [o.reshape(T, width) for o in outs]
    return [outs[di * n_in:(di + 1) * n_in] for di in range(len(dils))]


def _from_residue_merge_kernel(p_ref, x_ref, lx_ref, prev_ref, lp_ref, o_ref, lo_ref, *, dil, nh):
    per = PERM_GROUP // dil
    dh = o_ref.shape[1] // nh
    lp, lx = lp_ref[...], lx_ref[...]
    mx = jnp.maximum(lp, lx)
    ep, ex = jnp.exp(lp - mx), jnp.exp(lx - mx)
    wx = ex / (ep + ex)
    lo_ref[...] = mx + jnp.log(ep + ex)
    for g in range(o_ref.shape[0] // PERM_GROUP):
        rows = slice(g * PERM_GROUP, (g + 1) * PERM_GROUP)
        xg = jnp.concatenate([x_ref[r, g * per:(g + 1) * per, :] for r in range(dil)], axis=0)
        y = jnp.dot(p_ref[...], xg, preferred_element_type=F32)
        for h in range(nh):
            sl = slice(h * dh, (h + 1) * dh)
            prev = prev_ref[rows, sl].astype(F32)
            o_ref[rows, sl] = (prev + wx[rows, h:h + 1] * (y[:, sl] - prev)).astype(o_ref.dtype)


def _from_residue_merge(x, lse_x, prev, lse_prev, dil, bsz, *, nh, tb=512, name):
    T, width = x.shape
    S = T // bsz
    tb = _pick_tile(S, tb, PERM_GROUP)
    nb = S // tb
    row = lambda w: pl.BlockSpec((tb, w), lambda b, i: (b * nb + i, 0))
    return pl.pallas_call(
        functools.partial(_from_residue_merge_kernel, dil=dil, nh=nh),
        grid=(bsz, nb),
        in_specs=[pl.BlockSpec((PERM_GROUP, PERM_GROUP), lambda b, i: (0, 0)),
                  pl.BlockSpec((None, dil, tb // dil, width), lambda b, i: (b, 0, i, 0)),
                  row(nh), row(width), row(nh)],
        out_specs=[row(width), row(nh)],
        out_shape=[jax.ShapeDtypeStruct((T, width), x.dtype), jax.ShapeDtypeStruct((T, nh), F32)],
        compiler_params=_params(("parallel", "parallel"), 6 * tb * width * 2 + 4 * tb * width * 4 + (8 << 20)),
        name=name,
    )(_residue_perm(dil).T, x.reshape(bsz, dil, S // dil, width), lse_x, prev, lse_prev)


def _final_norm_kernel(x_ref, g_ref, o_ref):
    o_ref[...] = _rms_normalize(x_ref[...], g_ref[...])


def _final_norm(x, g, *, tm=1024, name):
    M, D = x.shape
    tm = _pick_tile(M, tm, SUBLANES)
    return pl.pallas_call(
        _final_norm_kernel,
        grid=(M // tm,),
        in_specs=[pl.BlockSpec((tm, D), lambda i: (i, 0)), pl.BlockSpec((1, D), lambda i: (0, 0))],
        out_specs=pl.BlockSpec((tm, D), lambda i: (i, 0)),
        out_shape=jax.ShapeDtypeStruct((M, D), F32),
        compiler_params=_params(("parallel",), 6 * tm * D * 4 + (4 << 20)),
        name=name,
    )(x, g.reshape(1, D).astype(F32))


def _even_mixer(h, g, w_in, w_out, s5_tabs, glu_w, glu_b, lam_vec, subln_g, lambda_init, rope_tabs, bsz, tag):
    T, _ = h.shape
    S = T // bsz
    s5_width = glu_w[0].shape[1]
    width = (w_in[0].shape[2] - s5_width) // 3
    n_heads = width // (2 * HEAD_DIM)
    segs = ((s5_width, s5_width + width, True, HEAD_DIM ** -0.5 * LOG2_E),
            (s5_width + width, s5_width + 2 * width, True, 1.0))
    proj = _norm_matmul(h, g, w_in, segs=segs, rope_tabs=rope_tabs, seq=S, name=f"even_in_{tag}")
    y_s5 = _s5_mix(proj[:, :s5_width], s5_tabs, bsz, name=f"s5_{tag}")
    y_s5 = _s5_glu(y_s5, glu_w, glu_b, name=f"s5_glu_{tag}")
    y_diff = _diff_attn(proj, lam_vec, subln_g, bsz, s5_width=s5_width, n_heads=n_heads,
                        lambda_init=lambda_init, tk=1024, unroll=8, name=f"diff_attn_{tag}")
    return _matmul_res([y_s5, y_diff], w_out, h, tn=1024, name=f"even_out_{tag}")


def _odd_mixer(h, g, w_qkv, w_out, rope_tabs, bsz, tag):
    T, D = h.shape
    S = T // bsz
    nh = D // HEAD_DIM
    segs = ((0, D, True, HEAD_DIM ** -0.5 * LOG2_E), (D, 2 * D, True, 1.0))
    qkv = _norm_matmul(h, g, w_qkv, segs=segs, rope_tabs=rope_tabs, seq=S, name=f"odd_in_{tag}")
    dils = [dil for _, dil in DILATED_BRANCHES if dil > 1]
    residue = dict(zip(dils, _to_residue(qkv, (0, 1, 2), D, dils, bsz, name=f"to_residue_{tag}")))
    merged = lse_merged = None
    for window, dil in sorted(DILATED_BRANCHES, key=lambda wd: wd[1]):
        half = window // (2 * dil)
        sub_len = S // dil
        if dil == 1:
            merged, lse_merged = _dilated_branch(qkv, qkv, qkv, (0, 1, 2), D, bsz, half=half, nh=nh,
                                                 name=f"dilated_{dil}_{tag}")
        else:
            qr, kr, vr = residue[dil]
            o, lse = _dilated_branch(qr, kr, vr, (0, 0, 0), D, bsz * dil, half=half, nh=nh,
                                     name=f"dilated_{dil}_{tag}")
            lse = jnp.transpose(lse.reshape(bsz, dil, sub_len, nh), (0, 2, 1, 3)).reshape(T, nh)
            merged, lse_merged = _from_residue_merge(o, lse, merged, lse_merged, dil, bsz, nh=nh,
                                                     name=f"from_residue_{dil}_{tag}")
    return _matmul_res([merged], w_out, h, tn=1024, name=f"odd_out_{tag}")


def kernel(x, mem, norm_mix_g, norm_xa_g, norm_mem_g, xa_wq, xa_wkv, xa_wo, norm_ffn_g, ffn_w13, ffn_w2, ab_w_in, ab_w_out, s5_lambda_re, s5_lambda_im, s5_log_step, s5_b_re, s5_b_im, s5_c_re, s5_c_im, s5_d, s5_glu_w, s5_glu_b, diff_lambda, diff_subln_g, c_w_qkv, c_w_out, final_norm_g):
    bsz, S, D = x.shape
    T = bsz * S
    depth = norm_mix_g.shape[0]
    mem_len = mem.shape[1]
    rope_tabs = _rope_tables(S)
    n_steps = S // (S5_CHUNK * S5_SEGMENTS)
    s5_width = s5_glu_w.shape[1]
    diff_width = (ab_w_in.shape[2] - s5_width) // 3
    ab_w_in = _cast_rope_columns(ab_w_in, s5_width, s5_width + 2 * diff_width, name="cast_even_in")
    c_w_qkv = _cast_rope_columns(c_w_qkv, 0, 2 * D, name="cast_odd_in")
    ab_w_out, s5_glu_w, c_w_out, xa_wq, xa_wkv, xa_wo, ffn_w13, ffn_w2 = (
        w.astype(BF16) for w in (ab_w_out, s5_glu_w, c_w_out, xa_wq, xa_wkv, xa_wo, ffn_w13, ffn_w2))

    h = x.reshape(T, D)
    mem2 = mem.reshape(bsz * mem_len, D)
    for layer in range(depth):
        i = layer // 2
        if layer % 2 == 0:
            lambda_init = 0.8 - 0.6 * math.exp(-0.3 * layer)
            tabs = _s5_tables(s5_lambda_re[i], s5_lambda_im[i], s5_log_step[i], s5_b_re[i], s5_b_im[i],
                              s5_c_re[i], s5_c_im[i], s5_d[i], n_steps)
            h = _even_mixer(h, norm_mix_g[layer], (ab_w_in, i), (ab_w_out, i), tabs, (s5_glu_w, i),
                            s5_glu_b[i], diff_lambda[i], diff_subln_g[i], lambda_init, rope_tabs, bsz, layer)
        else:
            h = _odd_mixer(h, norm_mix_g[layer], (c_w_qkv, i), (c_w_out, i), rope_tabs, bsz, layer)

        kv = _norm_matmul(mem2, norm_mem_g[layer], (xa_wkv, layer),
                          segs=((0, D, False, (D // XA_HEADS) ** -0.5),), name=f"xa_kv_{layer}")
        o = _cross_attn(h, norm_xa_g[layer], (xa_wq, layer), kv, bsz, name=f"xa_core_{layer}")
        h = _matmul_res([o], (xa_wo, layer), h, tn=1024, name=f"xa_out_{layer}")

        hid = _swiglu_up(h, norm_ffn_g[layer], (ffn_w13, layer), name=f"ffn_up_{layer}")
        h = _matmul_res([hid], (ffn_w2, layer), h, tn=512, name=f"ffn_down_{layer}")
    return _final_norm(h, final_norm_g, name="final_norm").reshape(bsz, S, D)
```

```python
import functools
import math

import jax
import jax.numpy as jnp
from jax import lax
from jax.experimental import pallas as pl
from jax.experimental.pallas import tpu as pltpu

F32 = jnp.float32
BF16 = jnp.bfloat16

NORM_EPS = 1e-6
MASK_VALUE = -1e30
LOG2_E = math.log2(math.e)
LN_2 = math.log(2.0)
ROPE_THETA = 500000.0
ROPE_HALF = 16
HEAD_DIM = 128
XA_HEADS = 4
DILATED_BRANCHES = ((128, 1), (512, 4), (2048, 16))
S5_CHUNK = 32
S5_SEGMENTS = 8
PERM_GROUP = 256
LANES = 128
SUBLANES = 8
VMEM_CAP_BYTES = 56 * 1024 * 1024


def _pick_tile(n, pref, quantum):
    t = (min(n, pref) // quantum) * quantum
    while t >= quantum:
        if n % t == 0:
            return t
        t -= quantum
    return n


def _params(semantics, vmem_bytes):
    return pltpu.CompilerParams(
        dimension_semantics=semantics,
        vmem_limit_bytes=int(min(VMEM_CAP_BYTES, max(32 * 1024 * 1024, vmem_bytes))))


def _rms_normalize(x, g):
    ms = jnp.mean(x * x, axis=-1, keepdims=True)
    return (x * lax.rsqrt(ms + NORM_EPS)) * g


def _rope_lanes(y, cf, sf):
    outs = []
    for c in range(y.shape[1] // LANES):
        yc = y[:, c * LANES:(c + 1) * LANES]
        outs.append(yc * cf + pltpu.roll(yc, LANES // 2, 1) * sf)
    return outs[0] if len(outs) == 1 else jnp.concatenate(outs, axis=1)


def _cast_rope_columns_kernel(p_ref, w_ref, o_ref, *, lo_blk, hi_blk):
    j = pl.program_id(2)
    w = w_ref[...].astype(BF16)
    roped = jnp.logical_and(j >= lo_blk, j < hi_blk)

    @pl.when(roped)
    def _():
        o_ref[...] = jnp.dot(w, p_ref[...], preferred_element_type=F32).astype(BF16)

    @pl.when(jnp.logical_not(roped))
    def _():
        o_ref[...] = w


def _cast_rope_columns(w, lo, hi, *, tr=1024, tc=512, name):
    n_layers, K, N = w.shape
    tr = _pick_tile(K, tr, SUBLANES)
    tc = _pick_tile(math.gcd(N, lo, hi), tc, LANES)
    r, h = ROPE_HALF, LANES // 2
    src = jnp.concatenate([jnp.arange(0, r), jnp.arange(2 * r, h + r), jnp.arange(r, 2 * r),
                           jnp.arange(h + r, LANES)])
    perm = (jnp.arange(LANES)[:, None] == src[None, :]).astype(BF16)
    perm = jnp.kron(jnp.eye(tc // LANES, dtype=BF16), perm)
    return pl.pallas_call(
        functools.partial(_cast_rope_columns_kernel, lo_blk=lo // tc, hi_blk=hi // tc),
        grid=(n_layers, K // tr, N // tc),
        in_specs=[pl.BlockSpec((tc, tc), lambda l, i, j: (0, 0)),
                  pl.BlockSpec((None, tr, tc), lambda l, i, j: (l, i, j))],
        out_specs=pl.BlockSpec((None, tr, tc), lambda l, i, j: (l, i, j)),
        out_shape=jax.ShapeDtypeStruct(w.shape, BF16),
        compiler_params=_params(("parallel", "parallel", "parallel"), 8 * tr * tc * 4 + (4 << 20)),
        name=name,
    )(perm, w)


def _rope_tables(seq):
    pos = jnp.arange(seq, dtype=F32)
    inv = ROPE_THETA ** (-jnp.arange(0, 2 * ROPE_HALF, 2, dtype=F32) / (2 * ROPE_HALF))
    ang = pos[:, None] * inv[None, :]
    cos, sin = jnp.cos(ang), jnp.sin(ang)
    gap = LANES // 2 - ROPE_HALF
    cf = jnp.concatenate([cos, jnp.ones((seq, gap), F32), cos, jnp.ones((seq, gap), F32)], axis=1)
    sf = jnp.concatenate([-sin, jnp.zeros((seq, gap), F32), sin, jnp.zeros((seq, gap), F32)], axis=1)
    return cf, sf


def _norm_matmul_kernel(*refs, segs, tn, use_rope):
    if use_rope:
        x_ref, g_ref, w_ref, cf_ref, sf_ref, o_ref, xn_ref = refs
    else:
        x_ref, g_ref, w_ref, o_ref, xn_ref = refs
    j = pl.program_id(1)

    @pl.when(j == 0)
    def _():
        xn_ref[...] = _rms_normalize(x_ref[...], g_ref[...]).astype(BF16)

    y = jnp.dot(xn_ref[...], w_ref[...], preferred_element_type=F32)
    for lo, hi, rope, scale in segs:
        @pl.when(jnp.logical_and(j >= lo // tn, j < hi // tn))
        def _(rope=rope, scale=scale):
            z = y
            if rope:
                cf, sf = cf_ref[...], sf_ref[...]
                if scale != 1.0:
                    cf, sf = cf * scale, sf * scale
                z = _rope_lanes(z, cf, sf)
            elif scale != 1.0:
                z = z * scale
            o_ref[...] = z.astype(o_ref.dtype)


def _norm_matmul(x, g, w, *, segs=(), rope_tabs=None, seq=None, tm=1024, tn=1024, name):
    M, D = x.shape
    w, layer = w
    N = w.shape[2]
    tm = _pick_tile(M if seq is None else seq, tm, SUBLANES)
    tn = _pick_tile(math.gcd(N, *[b for s in segs for b in s[:2]]), tn, LANES)
    full, pos = [], 0
    for lo, hi, rope, scale in sorted(segs):
        if lo > pos:
            full.append((pos, lo, False, 1.0))
        full.append((lo, hi, rope, scale))
        pos = hi
    if pos < N:
        full.append((pos, N, False, 1.0))
    for lo, hi, _, _ in full:
        assert lo % tn == 0 and hi % tn == 0, (lo, hi, tn)
    use_rope = any(s[2] for s in full)
    in_specs = [pl.BlockSpec((tm, D), lambda i, j: (i, 0)),
                pl.BlockSpec((1, D), lambda i, j: (0, 0)),
                pl.BlockSpec((None, D, tn), lambda i, j: (layer, 0, j))]
    args = [x, g.reshape(1, D).astype(F32), w]
    if use_rope:
        nseq = seq // tm
        in_specs += [pl.BlockSpec((tm, LANES), lambda i, j: (i % nseq, 0))] * 2
        args += list(rope_tabs)
    vmem = 2 * tm * D * 4 + tm * D * 2 + 2 * D * tn * 2 + 2 * tm * tn * 2 + 3 * tm * tn * 4 + (4 << 20)
    return pl.pallas_call(
        functools.partial(_norm_matmul_kernel, segs=tuple(full), tn=tn, use_rope=use_rope),
        grid=(M // tm, N // tn),
        in_specs=in_specs,
        out_specs=pl.BlockSpec((tm, tn), lambda i, j: (i, j)),
        out_shape=jax.ShapeDtypeStruct((M, N), BF16),
        scratch_shapes=[pltpu.VMEM((tm, D), BF16)],
        compiler_params=_params(("parallel", "arbitrary"), vmem),
        name=name,
    )(*args)


def _matmul_res_kernel(*refs, n_a):
    a_refs, w_refs = refs[:n_a], refs[n_a:2 * n_a]
    res_ref, o_ref = refs[2 * n_a], refs[2 * n_a + 1]
    acc = res_ref[...]
    for a_ref, w_ref in zip(a_refs, w_refs):
        acc = acc + jnp.dot(a_ref[...], w_ref[...], preferred_element_type=F32)
    o_ref[...] = acc


def _matmul_res(a_list, w, res, *, tm=1024, tn=512, name):
    n_a = len(a_list)
    M, K = a_list[0].shape
    w, layer = w
    N = w.shape[2]
    tm = _pick_tile(M, tm, SUBLANES)
    tn = _pick_tile(N, tn, LANES)
    in_specs = [pl.BlockSpec((tm, K), lambda i, j: (i, 0)) for _ in range(n_a)]
    in_specs += [pl.BlockSpec((None, K, tn), lambda i, j, r=r: (layer, r, j)) for r in range(n_a)]
    in_specs += [pl.BlockSpec((tm, tn), lambda i, j: (i, j))]
    vmem = n_a * (2 * tm * K * 2 + 2 * K * tn * 2) + 5 * tm * tn * 4 + (4 << 20)
    return pl.pallas_call(
        functools.partial(_matmul_res_kernel, n_a=n_a),
        grid=(M // tm, N // tn),
        in_specs=in_specs,
        out_specs=pl.BlockSpec((tm, tn), lambda i, j: (i, j)),
        out_shape=jax.ShapeDtypeStruct((M, N), F32),
        compiler_params=_params(("parallel", "parallel"), vmem),
        name=name,
    )(*a_list, *([w] * n_a), res)


def _swiglu_up_kernel(x_ref, g_ref, w1_ref, w3_ref, o_ref, xn_ref):
    @pl.when(pl.program_id(1) == 0)
    def _():
        xn_ref[...] = _rms_normalize(x_ref[...], g_ref[...]).astype(BF16)

    xn = xn_ref[...]
    a = jnp.dot(xn, w1_ref[...], preferred_element_type=F32)
    b = jnp.dot(xn, w3_ref[...], preferred_element_type=F32)
    o_ref[...] = (a * jax.nn.sigmoid(a) * b).astype(o_ref.dtype)


def _swiglu_up(x, g, w13, *, tm=1024, tn=512, name):
    M, D = x.shape
    w13, layer = w13
    H = w13.shape[2] // 2
    tm = _pick_tile(M, tm, SUBLANES)
    tn = _pick_tile(H, tn, LANES)
    nj = H // tn
    vmem = 2 * tm * D * 4 + tm * D * 2 + 4 * D * tn * 2 + 2 * tm * tn * 2 + 4 * tm * tn * 4 + (4 << 20)
    return pl.pallas_call(
        _swiglu_up_kernel,
        grid=(M // tm, nj),
        in_specs=[pl.BlockSpec((tm, D), lambda i, j: (i, 0)),
                  pl.BlockSpec((1, D), lambda i, j: (0, 0)),
                  pl.BlockSpec((None, D, tn), lambda i, j: (layer, 0, j)),
                  pl.BlockSpec((None, D, tn), lambda i, j: (layer, 0, j + nj))],
        out_specs=pl.BlockSpec((tm, tn), lambda i, j: (i, j)),
        out_shape=jax.ShapeDtypeStruct((M, H), BF16),
        scratch_shapes=[pltpu.VMEM((tm, D), BF16)],
        compiler_params=_params(("parallel", "arbitrary"), vmem),
        name=name,
    )(x, g.reshape(1, D).astype(F32), w13, w13)


def _cross_attn_kernel(x_ref, g_ref, wq_ref, k_ref, v_ref, o_ref, *, nh):
    dh = o_ref.shape[1] // nh
    xn = _rms_normalize(x_ref[...], g_ref[...]).astype(BF16)
    for h in range(nh):
        sl = slice(h * dh, (h + 1) * dh)
        q = jnp.dot(xn, wq_ref[:, sl], preferred_element_type=F32).astype(BF16)
        s = lax.dot_general(q, k_ref[:, sl], (((1,), (1,)), ((), ())), preferred_element_type=F32)
        m = jnp.max(s, axis=-1, keepdims=True)
        e = jnp.exp(s - m)
        den = jnp.sum(e, axis=-1, keepdims=True)
        o = jnp.dot(e.astype(BF16), v_ref[:, sl], preferred_element_type=F32)
        o_ref[:, sl] = (o / den).astype(o_ref.dtype)


def _cross_attn(x, g, wq, kv, bsz, *, tq=1024, name):
    T, D = x.shape
    S = T // bsz
    wq, layer = wq
    mem_len = kv.shape[0] // bsz
    tq = _pick_tile(S, tq, SUBLANES)
    nq = S // tq
    kv3 = kv.reshape(bsz, mem_len, 2 * D)
    vmem = 2 * tq * D * 4 + 2 * D * D * 2 + 4 * mem_len * D * 2 + 2 * tq * D * 2 + 6 * tq * D * 2 + (8 << 20)
    return pl.pallas_call(
        functools.partial(_cross_attn_kernel, nh=XA_HEADS),
        grid=(bsz, nq),
        in_specs=[pl.BlockSpec((tq, D), lambda b, i: (b * nq + i, 0)),
                  pl.BlockSpec((1, D), lambda b, i: (0, 0)),
                  pl.BlockSpec((None, D, D), lambda b, i: (layer, 0, 0)),
                  pl.BlockSpec((None, mem_len, D), lambda b, i: (b, 0, 0)),
                  pl.BlockSpec((None, mem_len, D), lambda b, i: (b, 0, 1))],
        out_specs=pl.BlockSpec((tq, D), lambda b, i: (b * nq + i, 0)),
        out_shape=jax.ShapeDtypeStruct((T, D), BF16),
        compiler_params=_params(("parallel", "parallel"), vmem),
        name=name,
    )(x, g.reshape(1, D).astype(F32), wq, kv3, kv3)


def _diff_attn_kernel(q_ref, k_ref, v_ref, lv_ref, sg_ref, o_ref, vt_ref, *, tk, unroll, lambda_init):
    tq = q_ref.shape[0]
    seq = k_ref.shape[0]
    dv = v_ref.shape[1]
    dh = dv // 2

    @pl.when(pl.program_id(2) == 0)
    def _():
        for c in range(seq // tk):
            vt_ref[:, c * tk:(c + 1) * tk] = v_ref[c * tk:(c + 1) * tk, :].astype(F32).T.astype(BF16)

    q = q_ref[...]
    nk = seq // tk

    def scores(kk):
        kc = k_ref[pl.ds(pl.multiple_of(kk * tk, tk), tk), :]
        return tuple(lax.dot_general(kc[:, c * dh:(c + 1) * dh], q[:, c * dh:(c + 1) * dh],
                                     (((1,), (1,)), ((), ())), preferred_element_type=F32)
                     for c in range(2))

    def body(kk, carry):
        st_next = scores(jnp.minimum(kk + 1, nk - 1))
        vt = vt_ref[:, pl.ds(pl.multiple_of(kk * tk, tk), tk)]
        new = []
        for c in range(2):
            st, m, l, acc = carry[c]
            m_new = jnp.maximum(m, jnp.max(st, axis=0, keepdims=True))
            alpha = jnp.exp2(m - m_new)
            pt = jnp.exp2(st - m_new)
            l = alpha * l + jnp.sum(pt, axis=0, keepdims=True)
            acc = alpha * acc + jnp.dot(vt, pt.astype(BF16), preferred_element_type=F32)
            new.append((st_next[c], m_new, l, acc))
        return tuple(new)

    st0 = scores(0)
    init = tuple((st0[c], jnp.full((1, tq), MASK_VALUE, F32), jnp.zeros((1, tq), F32),
                  jnp.zeros((dv, tq), F32)) for c in range(2))
    (_, _, l1, a1), (_, _, l2, a2) = lax.fori_loop(0, nk, body, init, unroll=unroll)
    lv = lv_ref[...]
    lam = (jnp.exp(jnp.sum(lv[0:1] * lv[1:2], axis=-1, keepdims=True))
           - jnp.exp(jnp.sum(lv[2:3] * lv[3:4], axis=-1, keepdims=True)) + lambda_init)
    ot = a1 / l1 - lam * (a2 / l2)
    ms = jnp.mean(ot * ot, axis=0, keepdims=True)
    ot = ot * lax.rsqrt(ms + NORM_EPS)
    o_ref[...] = (ot.T * sg_ref[...] * (1.0 - lambda_init)).astype(o_ref.dtype)


def _diff_attn(proj, lam_vec, subln_g, bsz, *, s5_width, n_heads, lambda_init, tq=512, tk=512, unroll=1, name):
    T, ncol = proj.shape
    S = T // bsz
    dv = 2 * HEAD_DIM
    width = n_heads * dv
    q0, k0, v0 = s5_width // dv, (s5_width + width) // dv, (s5_width + 2 * width) // dv
    tq = _pick_tile(S, tq, SUBLANES)
    tk = _pick_tile(S, tk, LANES)
    nq = S // tq
    proj3 = proj.reshape(bsz, S, ncol)
    vmem = 4 * tq * dv * 2 + 5 * S * dv * 2 + unroll * (3 * tq * tk * 4 + 2 * tq * dv * 4) + (8 << 20)
    return pl.pallas_call(
        functools.partial(_diff_attn_kernel, tk=tk, unroll=unroll, lambda_init=lambda_init),
        grid=(bsz, n_heads, nq),
        in_specs=[pl.BlockSpec((None, tq, dv), lambda b, h, i: (b, i, q0 + h)),
                  pl.BlockSpec((None, S, dv), lambda b, h, i: (b, 0, k0 + h)),
                  pl.BlockSpec((None, S, dv), lambda b, h, i: (b, 0, v0 + h)),
                  pl.BlockSpec((4, HEAD_DIM), lambda b, h, i: (0, 0)),
                  pl.BlockSpec((1, dv), lambda b, h, i: (0, 0))],
        out_specs=pl.BlockSpec((tq, dv), lambda b, h, i: (b * nq + i, h)),
        out_shape=jax.ShapeDtypeStruct((T, width), BF16),
        scratch_shapes=[pltpu.VMEM((dv, S), BF16)],
        compiler_params=_params(("parallel", "parallel", "arbitrary"), vmem),
        name=name,
    )(proj3, proj3, proj3, lam_vec.astype(F32), subln_g.reshape(1, dv).astype(F32))


def _s5_tables(lam_re, lam_im, log_step, b_re, b_im, c_re, c_im, d_skip, n_steps):
    L = S5_CHUNK
    _, G, P = lam_re.shape
    N = b_re.shape[-1]
    lr, li = lam_re.astype(F32), lam_im.astype(F32)
    step = jnp.exp(log_step.astype(F32))[..., None]
    ar, ai = lr * step, li * step

    def cexp(t, axes=()):
        a_r, a_i = jnp.expand_dims(ar, axes), jnp.expand_dims(ai, axes)
        mag = jnp.exp(t * a_r)
        return mag * jnp.cos(t * a_i), mag * jnp.sin(t * a_i)

    def cmul(xr, xi, yr, yi):
        return xr * yr - xi * yi, xr * yi + xi * yr

    er, ei = cexp(1.0)
    den = lr * lr + li * li
    qr, qi = ((er - 1.0) * lr + ei * li) / den, (ei * lr - (er - 1.0) * li) / den
    br, bi = cmul(qr[..., None], qi[..., None], b_re.astype(F32), b_im.astype(F32))
    cr, ci = c_re.astype(F32), c_im.astype(F32)
    tau = jnp.arange(L + 1, dtype=F32)
    pr, pi = cexp(tau[None, None, :, None], axes=2)
    prt, pit = cexp(tau[None, None, None, :], axes=3)
    brt, bit = jnp.swapaxes(br, 2, 3), jnp.swapaxes(bi, 2, 3)
    crt, cit = jnp.swapaxes(cr, 2, 3), jnp.swapaxes(ci, 2, 3)

    def pow_c(d, taus):
        return cmul(prt[d][:, :, taus, None], pit[d][:, :, taus, None], crt[d][:, :, None, :], cit[d][:, :, None, :])

    kb_cat = jnp.stack([jnp.concatenate([brt[d], -bit[d]], axis=2) for d in range(2)], axis=1)
    kc_cat = jnp.stack([jnp.concatenate(pow_c(d, taus), axis=1).reshape(G, 2 * P, L * N)
                        for d, taus in ((0, slice(0, L)), (1, slice(L - 1, None, -1)))], axis=1)

    def pow_b(d, taus):
        return cmul(pr[d][:, taus, None, :], pi[d][:, taus, None, :], brt[d][:, None], bit[d][:, None])
    (fr, fi), (gr, gi) = pow_b(0, slice(L - 1, None, -1)), pow_b(1, slice(0, L))
    w_in = jnp.concatenate([fr, gr, fi, gi], axis=3).reshape(G, L * N, 4 * P)

    (of_r, of_i), (ob_r, ob_i) = pow_c(0, slice(1, L + 1)), pow_c(1, slice(L, 0, -1))
    w_out = jnp.concatenate([of_r, ob_r, -of_i, -ob_i], axis=1).reshape(G, 4 * P, L * N)

    def _lanes(z):
        return jnp.concatenate([z[0], z[1]], axis=-1)
    a_chunk, a_seg = cexp(float(L)), cexp(float(L * n_steps))
    decay = jnp.stack([_lanes(a_chunk[0]), _lanes(a_chunk[1]), _lanes(a_seg[0]), _lanes(a_seg[1])], axis=1)
    jj = float(L) * jnp.arange(n_steps, dtype=F32)
    jj = jnp.stack([jj, jj[::-1]], axis=0)[:, None, :, None]
    powers = jnp.stack([_lanes(p) for p in cexp(jj, axes=2)], axis=1)
    dvec = jnp.tile(d_skip.astype(F32).reshape(G, 1, N), (1, L, 1)).reshape(G, 1, L * N)
    return kb_cat, kc_cat, w_in.astype(BF16), w_out.astype(BF16), decay, powers, dvec


def _s5_kernel(u_ref, kb_ref, kc_ref, win_ref, wout_ref, decay_ref, pow_ref, dvec_ref, o_ref, toep_ref,
               *, n_steps, bsz, gb):
    rb = bsz * SUBLANES
    n_ch = kb_ref.shape[2]
    width = kc_ref.shape[3]
    chunk = width // n_ch
    lane = lax.broadcasted_iota(jnp.int32, (n_ch, width), 1)
    for gi in range(gb):
        kf = jnp.dot(kb_ref[gi, 0], kc_ref[gi, 0], preferred_element_type=F32, precision=lax.Precision.HIGHEST)
        kb = jnp.dot(kb_ref[gi, 1], kc_ref[gi, 1], preferred_element_type=F32, precision=lax.Precision.HIGHEST)
        for s in range(chunk):
            fwd = kf if s == 0 else jnp.where(lane >= s * n_ch, pltpu.roll(kf, s * n_ch, 1), 0.0)
            back = chunk - 1 - s
            bwd = kb if back == 0 else jnp.where(lane < (s + 1) * n_ch, pltpu.roll(kb, width - back * n_ch, 1), 0.0)
            toep_ref[gi, s * n_ch:(s + 1) * n_ch, :] = (fwd + bwd).astype(BF16)
        u = u_ref[gi]
        y = jnp.dot(u, toep_ref[gi], preferred_element_type=F32)
        z = jnp.dot(u, win_ref[gi], preferred_element_type=F32)
        half = z.shape[1] // 2
        zr, zi = z[:, :half], z[:, half:]
        dec = decay_ref[gi]
        ar, ai, sr, si = dec[0:1], dec[1:2], dec[2:3], dec[3:4]
        is_fwd = lax.broadcasted_iota(jnp.int32, (rb, half), 1) < half // 2

        xr = jnp.zeros((rb, half), F32)
        xi = jnp.zeros((rb, half), F32)
        ent_r, ent_i = [], []
        for j in range(n_steps):
            jb = n_steps - 1 - j
            ent_r.append(xr)
            ent_i.append(xi)
            zrj = jnp.where(is_fwd, zr[j * rb:(j + 1) * rb], zr[jb * rb:(jb + 1) * rb])
            zij = jnp.where(is_fwd, zi[j * rb:(j + 1) * rb], zi[jb * rb:(jb + 1) * rb])
            xr, xi = ar * xr - ai * xi + zrj, ar * xi + ai * xr + zij

        sub = lax.broadcasted_iota(jnp.int32, (SUBLANES, half), 0)
        fwd8 = lax.broadcasted_iota(jnp.int32, (SUBLANES, half), 1) < half // 2
        keep = (fwd8 & (sub > 0)) | (jnp.logical_not(fwd8) & (sub < SUBLANES - 1))
        car_r, car_i = [], []
        for b in range(bsz):
            er, ei = xr[b * SUBLANES:(b + 1) * SUBLANES], xi[b * SUBLANES:(b + 1) * SUBLANES]
            cr = jnp.zeros((SUBLANES, half), F32)
            ci = jnp.zeros((SUBLANES, half), F32)
            for _ in range(SUBLANES - 1):
                tr, ti = sr * cr - si * ci + er, sr * ci + si * cr + ei
                cr = jnp.where(keep, jnp.where(fwd8, pltpu.roll(tr, 1, 0), pltpu.roll(tr, SUBLANES - 1, 0)), 0.0)
                ci = jnp.where(keep, jnp.where(fwd8, pltpu.roll(ti, 1, 0), pltpu.roll(ti, SUBLANES - 1, 0)), 0.0)
            car_r.append(cr)
            car_i.append(ci)
        gr = car_r[0] if bsz == 1 else jnp.concatenate(car_r, axis=0)
        gi_ = car_i[0] if bsz == 1 else jnp.concatenate(car_i, axis=0)

        pr_all, pi_all = pow_ref[gi, 0], pow_ref[gi, 1]
        rows_r, rows_i = [], []
        for j in range(n_steps):
            jb = n_steps - 1 - j
            lr = jnp.where(is_fwd, ent_r[j], ent_r[jb])
            li = jnp.where(is_fwd, ent_i[j], ent_i[jb])
            pr, pi = pr_all[j:j + 1], pi_all[j:j + 1]
            rows_r.append(lr + pr * gr - pi * gi_)
            rows_i.append(li + pr * gi_ + pi * gr)
        state = jnp.concatenate([jnp.concatenate(rows_r, axis=0), jnp.concatenate(rows_i, axis=0)], axis=1)
        y = y + jnp.dot(state.astype(BF16), wout_ref[gi], preferred_element_type=F32)
        y = y + u.astype(F32) * dvec_ref[gi]
        o_ref[gi] = y.astype(o_ref.dtype)


def _s5_mix(u, tables, bsz, *, gb=4, name):
    kb_cat, kc_cat, w_in, w_out, decay, powers, dvec = tables
    T, width = u.shape
    S = T // bsz
    G = w_in.shape[0]
    N = width // G
    L = S5_CHUNK
    n_steps = S // (L * S5_SEGMENTS)
    rows = n_steps * bsz * S5_SEGMENTS
    gb = _pick_tile(G, gb, 1)
    ug = u.reshape(bsz, S5_SEGMENTS, n_steps, L, G, N)
    ug = jnp.transpose(ug, (4, 2, 0, 1, 3, 5)).reshape(G, rows, L * N)
    blk = lambda *shape: pl.BlockSpec((gb,) + shape, lambda g: (g,) + (0,) * len(shape))
    vmem = gb * (4 * rows * L * N * 2 + (L * N) ** 2 * 2 + 4 * kc_cat.shape[2] * L * N * 4
                 + 8 * L * N * w_in.shape[2] * 2) + (24 << 20)
    yg = pl.pallas_call(
        functools.partial(_s5_kernel, n_steps=n_steps, bsz=bsz, gb=gb),
        grid=(G // gb,),
        in_specs=[blk(rows, L * N), blk(*kb_cat.shape[1:]), blk(*kc_cat.shape[1:]),
                  blk(L * N, w_in.shape[2]), blk(w_out.shape[1], L * N),
                  blk(4, decay.shape[2]), blk(2, n_steps, powers.shape[3]), blk(1, L * N)],
        out_specs=blk(rows, L * N),
        out_shape=jax.ShapeDtypeStruct((G, rows, L * N), BF16),
        scratch_shapes=[pltpu.VMEM((gb, L * N, L * N), BF16)],
        compiler_params=_params(("parallel",), vmem),
        name=name,
    )(ug, kb_cat, kc_cat, w_in, w_out, decay, powers, dvec)
    yg = yg.reshape(G, n_steps, bsz, S5_SEGMENTS, L, N)
    return jnp.transpose(yg, (2, 3, 1, 4, 0, 5)).reshape(T, width)


def _s5_glu_kernel(y_ref, w_ref, b_ref, o_ref):
    g = jax.nn.gelu(y_ref[...].astype(F32), approximate=True)
    z = jnp.dot(g.astype(BF16), w_ref[...], preferred_element_type=F32) + b_ref[...]
    o_ref[...] = (g * jax.nn.sigmoid(z)).astype(o_ref.dtype)


def _s5_glu(y, w, b, *, tm=1024, name):
    M, W = y.shape
    w, layer = w
    tm = _pick_tile(M, tm, SUBLANES)
    vmem = 4 * tm * W * 2 + 2 * W * W * 2 + 4 * tm * W * 4 + (4 << 20)
    return pl.pallas_call(
        _s5_glu_kernel,
        grid=(M // tm,),
        in_specs=[pl.BlockSpec((tm, W), lambda i: (i, 0)),
                  pl.BlockSpec((None, W, W), lambda i: (layer, 0, 0)),
                  pl.BlockSpec((1, W), lambda i: (0, 0))],
        out_specs=pl.BlockSpec((tm, W), lambda i: (i, 0)),
        out_shape=jax.ShapeDtypeStruct((M, W), BF16),
        compiler_params=_params(("parallel",), vmem),
        name=name,
    )(y, w, b.reshape(1, W).astype(F32))


def _slab_start(i, tq, half, slab, sub_len):
    assert tq % half == 0 and (sub_len - slab) % half == 0
    return half * jnp.clip(i * (tq // half) - 1, 0, (sub_len - slab) // half)


def _dilated_kernel(q_ref, k_ref, v_ref, o_ref, lse_ref, *, sub_len, half, nh):
    i = pl.program_id(1)
    tq = q_ref.shape[0]
    slab = k_ref.shape[0]
    dh = q_ref.shape[1] // nh
    assert dh == LANES
    sq = 2 * half
    win = sq + 2 * half
    start = _slab_start(i, tq, half, slab, sub_len)
    head_lane = lax.broadcasted_iota(jnp.int32, (sq, nh), 1)
    ones = jnp.ones((win, LANES), BF16)
    for j in range(tq // sq):
        off = pl.multiple_of(jnp.clip(i * tq + j * sq - half - start, 0, slab - win), half)
        qpos = i * tq + j * sq + lax.broadcasted_iota(jnp.int32, (sq, win), 0)
        kpos = start + off + lax.broadcasted_iota(jnp.int32, (sq, win), 1)
        valid = jnp.abs(kpos - qpos) <= half
        rows = slice(j * sq, (j + 1) * sq)
        lse_all = jnp.zeros((sq, nh), F32)
        for h in range(nh):
            sl = slice(h * dh, (h + 1) * dh)
            s = lax.dot_general(q_ref[rows, sl], k_ref[pl.ds(off, win), sl], (((1,), (1,)), ((), ())),
                                preferred_element_type=F32)
            s = jnp.where(valid, s, MASK_VALUE)
            m = jnp.max(s, axis=-1, keepdims=True)
            e = jnp.exp2(s - m).astype(BF16)
            ov = jnp.dot(e, jnp.concatenate([v_ref[pl.ds(off, win), sl], ones], axis=1),
                         preferred_element_type=F32)
            den = ov[:, dh:]
            o_ref[rows, sl] = (ov[:, :dh] / den).astype(o_ref.dtype)
            lse_all = jnp.where(head_lane == h, m * LN_2 + jnp.log(den[:, :nh]), lse_all)
        lse_ref[rows, :] = lse_all


def _dilated_branch(q, k, v, cols, D, n_seq, *, half, nh, name):
    T = q.shape[0]
    sub_len = T // n_seq
    sq = 2 * half
    n_sub = max(n for n in (4, 2, 1) if sub_len % (n * sq) == 0 and n * sq + 2 * half <= sub_len)
    tq = n_sub * sq
    slab = tq + 2 * half
    nq = sub_len // tq
    cq, ck, cv = cols

    def kv_spec(c):
        return pl.BlockSpec((pl.Element(slab), pl.Element(D)),
                            lambda s, i: (pl.multiple_of(s * sub_len + _slab_start(i, tq, half, slab, sub_len),
                                                         half), c * D))

    vmem = 2 * (2 * tq * D * 2 + 2 * slab * D * 2) + (16 << 20)
    return pl.pallas_call(
        functools.partial(_dilated_kernel, sub_len=sub_len, half=half, nh=nh),
        grid=(n_seq, nq),
        in_specs=[pl.BlockSpec((tq, D), lambda s, i: (s * nq + i, cq)), kv_spec(ck), kv_spec(cv)],
        out_specs=[pl.BlockSpec((tq, D), lambda s, i: (s * nq + i, 0)),
                   pl.BlockSpec((tq, nh), lambda s, i: (s * nq + i, 0))],
        out_shape=[jax.ShapeDtypeStruct((T, D), BF16), jax.ShapeDtypeStruct((T, nh), F32)],
        compiler_params=_params(("parallel", "parallel"), vmem),
        name=name,
    )(q, k, v)


def _residue_perm(dil):
    per = PERM_GROUP // dil
    rows = jnp.arange(PERM_GROUP)
    src = (rows % per) * dil + rows // per
    return (src[:, None] == rows[None, :]).astype(BF16)


def _to_residue_kernel(*refs, dils, n_in):
    n_d = len(dils)
    p_refs, x_refs, o_refs = refs[:n_d], refs[n_d:n_d + n_in], refs[n_d + n_in:]
    for xi, x_ref in enumerate(x_refs):
        for g in range(x_ref.shape[0] // PERM_GROUP):
            xg = x_ref[g * PERM_GROUP:(g + 1) * PERM_GROUP, :]
            for di, dil in enumerate(dils):
                o_ref = o_refs[di * n_in + xi]
                per = PERM_GROUP // dil
                y = jnp.dot(p_refs[di][...], xg, preferred_element_type=F32).astype(o_ref.dtype)
                for r in range(dil):
                    o_ref[r, g * per:(g + 1) * per, :] = y[r * per:(r + 1) * per]


def _to_residue(x, cols, width, dils, bsz, *, tb=512, name):
    T = x.shape[0]
    S = T // bsz
    tb = _pick_tile(S, tb, PERM_GROUP)
    nb = S // tb
    n_in = len(cols)
    perm_spec = pl.BlockSpec((PERM_GROUP, PERM_GROUP), lambda b, i: (0, 0))
    outs = pl.pallas_call(
        functools.partial(_to_residue_kernel, dils=tuple(dils), n_in=n_in),
        grid=(bsz, nb),
        in_specs=[perm_spec] * len(dils)
                 + [pl.BlockSpec((tb, width), lambda b, i, c=c: (b * nb + i, c)) for c in cols],
        out_specs=[pl.BlockSpec((None, dil, tb // dil, width), lambda b, i: (b, 0, i, 0))
                   for dil in dils for _ in cols],
        out_shape=[jax.ShapeDtypeStruct((bsz, dil, S // dil, width), x.dtype) for dil in dils for _ in cols],
        compiler_params=_params(("parallel", "parallel"),
                                (1 + len(dils)) * n_in * 2 * tb * width * 2 + (12 << 20)),
        name=name,
    )(*[_residue_perm(dil) for dil in dils], *([x] * n_in))
    outs = [o.reshape(T, width) for o in outs]
    return [outs[di * n_in:(di + 1) * n_in] for di in range(len(dils))]


def _from_residue_merge_kernel(p_ref, x_ref, lx_ref, prev_ref, lp_ref, o_ref, lo_ref, *, dil, nh):
    per = PERM_GROUP // dil
    dh = o_ref.shape[1] // nh
    lp, lx = lp_ref[...], lx_ref[...]
    mx = jnp.maximum(lp, lx)
    ep, ex = jnp.exp(lp - mx), jnp.exp(lx - mx)
    wx = ex / (ep + ex)
    lo_ref[...] = mx + jnp.log(ep + ex)
    for g in range(o_ref.shape[0] // PERM_GROUP):
        rows = slice(g * PERM_GROUP, (g + 1) * PERM_GROUP)
        xg = jnp.concatenate([x_ref[r, g * per:(g + 1) * per, :] for r in range(dil)], axis=0)
        y = jnp.dot(p_ref[...], xg, preferred_element_type=F32)
        for h in range(nh):
            sl = slice(h * dh, (h + 1) * dh)
            prev = prev_ref[rows, sl].astype(F32)
            o_ref[rows, sl] = (prev + wx[rows, h:h + 1] * (y[:, sl] - prev)).astype(o_ref.dtype)


def _from_residue_merge(x, lse_x, prev, lse_prev, dil, bsz, *, nh, tb=512, name):
    T, width = x.shape
    S = T // bsz
    tb = _pick_tile(S, tb, PERM_GROUP)
    nb = S // tb
    row = lambda w: pl.BlockSpec((tb, w), lambda b, i: (b * nb + i, 0))
    return pl.pallas_call(
        functools.partial(_from_residue_merge_kernel, dil=dil, nh=nh),
        grid=(bsz, nb),
        in_specs=[pl.BlockSpec((PERM_GROUP, PERM_GROUP), lambda b, i: (0, 0)),
                  pl.BlockSpec((None, dil, tb // dil, width), lambda b, i: (b, 0, i, 0)),
                  row(nh), row(width), row(nh)],
        out_specs=[row(width), row(nh)],
        out_shape=[jax.ShapeDtypeStruct((T, width), x.dtype), jax.ShapeDtypeStruct((T, nh), F32)],
        compiler_params=_params(("parallel", "parallel"), 6 * tb * width * 2 + 4 * tb * width * 4 + (8 << 20)),
        name=name,
    )(_residue_perm(dil).T, x.reshape(bsz, dil, S // dil, width), lse_x, prev, lse_prev)


def _final_norm_kernel(x_ref, g_ref, o_ref):
    o_ref[...] = _rms_normalize(x_ref[...], g_ref[...])


def _final_norm(x, g, *, tm=1024, name):
    M, D = x.shape
    tm = _pick_tile(M, tm, SUBLANES)
    return pl.pallas_call(
        _final_norm_kernel,
        grid=(M // tm,),
        in_specs=[pl.BlockSpec((tm, D), lambda i: (i, 0)), pl.BlockSpec((1, D), lambda i: (0, 0))],
        out_specs=pl.BlockSpec((tm, D), lambda i: (i, 0)),
        out_shape=jax.ShapeDtypeStruct((M, D), F32),
        compiler_params=_params(("parallel",), 6 * tm * D * 4 + (4 << 20)),
        name=name,
    )(x, g.reshape(1, D).astype(F32))


def _even_mixer(h, g, w_in, w_out, s5_tabs, glu_w, glu_b, lam_vec, subln_g, lambda_init, rope_tabs, bsz, tag):
    T, _ = h.shape
    S = T // bsz
    s5_width = glu_w[0].shape[1]
    width = (w_in[0].shape[2] - s5_width) // 3
    n_heads = width // (2 * HEAD_DIM)
    segs = ((s5_width, s5_width + width, True, HEAD_DIM ** -0.5 * LOG2_E),
            (s5_width + width, s5_width + 2 * width, True, 1.0))
    proj = _norm_matmul(h, g, w_in, segs=segs, rope_tabs=rope_tabs, seq=S, name=f"even_in_{tag}")
    y_s5 = _s5_mix(proj[:, :s5_width], s5_tabs, bsz, name=f"s5_{tag}")
    y_s5 = _s5_glu(y_s5, glu_w, glu_b, name=f"s5_glu_{tag}")
    y_diff = _diff_attn(proj, lam_vec, subln_g, bsz, s5_width=s5_width, n_heads=n_heads,
                        lambda_init=lambda_init, tk=1024, unroll=8, name=f"diff_attn_{tag}")
    return _matmul_res([y_s5, y_diff], w_out, h, tm=512, tn=2048, name=f"even_out_{tag}")


def _odd_mixer(h, g, w_qkv, w_out, rope_tabs, bsz, tag):
    T, D = h.shape
    S = T // bsz
    nh = D // HEAD_DIM
    segs = ((0, D, True, HEAD_DIM ** -0.5 * LOG2_E), (D, 2 * D, True, 1.0))
    qkv = _norm_matmul(h, g, w_qkv, segs=segs, rope_tabs=rope_tabs, seq=S, name=f"odd_in_{tag}")
    dils = [dil for _, dil in DILATED_BRANCHES if dil > 1]
    residue = dict(zip(dils, _to_residue(qkv, (0, 1, 2), D, dils, bsz, name=f"to_residue_{tag}")))
    merged = lse_merged = None
    for window, dil in sorted(DILATED_BRANCHES, key=lambda wd: wd[1]):
        half = window // (2 * dil)
        sub_len = S // dil
        if dil == 1:
            merged, lse_merged = _dilated_branch(qkv, qkv, qkv, (0, 1, 2), D, bsz, half=half, nh=nh,
                                                 name=f"dilated_{dil}_{tag}")
        else:
            qr, kr, vr = residue[dil]
            o, lse = _dilated_branch(qr, kr, vr, (0, 0, 0), D, bsz * dil, half=half, nh=nh,
                                     name=f"dilated_{dil}_{tag}")
            lse = jnp.transpose(lse.reshape(bsz, dil, sub_len, nh), (0, 2, 1, 3)).reshape(T, nh)
            merged, lse_merged = _from_residue_merge(o, lse, merged, lse_merged, dil, bsz, nh=nh,
                                                     name=f"from_residue_{dil}_{tag}")
    return _matmul_res([merged], w_out, h, tm=512, tn=2048, name=f"odd_out_{tag}")


def kernel(x, mem, norm_mix_g, norm_xa_g, norm_mem_g, xa_wq, xa_wkv, xa_wo, norm_ffn_g, ffn_w13, ffn_w2, ab_w_in, ab_w_out, s5_lambda_re, s5_lambda_im, s5_log_step, s5_b_re, s5_b_im, s5_c_re, s5_c_im, s5_d, s5_glu_w, s5_glu_b, diff_lambda, diff_subln_g, c_w_qkv, c_w_out, final_norm_g):
    bsz, S, D = x.shape
    T = bsz * S
    depth = norm_mix_g.shape[0]
    mem_len = mem.shape[1]
    rope_tabs = _rope_tables(S)
    n_steps = S // (S5_CHUNK * S5_SEGMENTS)
    s5_width = s5_glu_w.shape[1]
    diff_width = (ab_w_in.shape[2] - s5_width) // 3
    ab_w_in = _cast_rope_columns(ab_w_in, s5_width, s5_width + 2 * diff_width, name="cast_even_in")
    c_w_qkv = _cast_rope_columns(c_w_qkv, 0, 2 * D, name="cast_odd_in")
    ab_w_out, s5_glu_w, c_w_out, xa_wq, xa_wkv, xa_wo, ffn_w13, ffn_w2 = (
        w.astype(BF16) for w in (ab_w_out, s5_glu_w, c_w_out, xa_wq, xa_wkv, xa_wo, ffn_w13, ffn_w2))

    h = x.reshape(T, D)
    mem2 = mem.reshape(bsz * mem_len, D)
    for layer in range(depth):
        i = layer // 2
        if layer % 2 == 0:
            lambda_init = 0.8 - 0.6 * math.exp(-0.3 * layer)
            tabs = _s5_tables(s5_lambda_re[i], s5_lambda_im[i], s5_log_step[i], s5_b_re[i], s5_b_im[i],
                              s5_c_re[i], s5_c_im[i], s5_d[i], n_steps)
            h = _even_mixer(h, norm_mix_g[layer], (ab_w_in, i), (ab_w_out, i), tabs, (s5_glu_w, i),
                            s5_glu_b[i], diff_lambda[i], diff_subln_g[i], lambda_init, rope_tabs, bsz, layer)
        else:
            h = _odd_mixer(h, norm_mix_g[layer], (c_w_qkv, i), (c_w_out, i), rope_tabs, bsz, layer)

        kv = _norm_matmul(mem2, norm_mem_g[layer], (xa_wkv, layer),
                          segs=((0, D, False, (D // XA_HEADS) ** -0.5),), name=f"xa_kv_{layer}")
        o = _cross_attn(h, norm_xa_g[layer], (xa_wq, layer), kv, bsz, name=f"xa_core_{layer}")
        h = _matmul_res([o], (xa_wo, layer), h, tm=512, tn=2048, name=f"xa_out_{layer}")

        hid = _swiglu_up(h, norm_ffn_g[layer], (ffn_w13, layer), name=f"ffn_up_{layer}")
        h = _matmul_res([hid], (ffn_w2, layer), h, tn=512, name=f"ffn_down_{layer}")
    return _final_norm(h, final_norm_g, name="final_norm").reshape(bsz, S, D)
```

```python
import functools
import math

import jax
import jax.numpy as jnp
from jax import lax
from jax.experimental import pallas as pl
from jax.experimental.pallas import tpu as pltpu

F32 = jnp.float32
BF16 = jnp.bfloat16

NORM_EPS = 1e-6
MASK_VALUE = -1e30
LOG2_E = math.log2(math.e)
LN_2 = math.log(2.0)
ROPE_THETA = 500000.0
ROPE_HALF = 16
HEAD_DIM = 128
XA_HEADS = 4
DILATED_BRANCHES = ((128, 1), (512, 4), (2048, 16))
S5_CHUNK = 32
S5_SEGMENTS = 8
PERM_GROUP = 256
LANES = 128
SUBLANES = 8
VMEM_CAP_BYTES = 56 * 1024 * 1024


def _pick_tile(n, pref, quantum):
    t = (min(n, pref) // quantum) * quantum
    while t >= quantum:
        if n % t == 0:
            return t
        t -= quantum
    return n


def _params(semantics, vmem_bytes):
    return pltpu.CompilerParams(
        dimension_semantics=semantics,
        vmem_limit_bytes=int(min(VMEM_CAP_BYTES, max(32 * 1024 * 1024, vmem_bytes))))


def _rms_normalize(x, g):
    ms = jnp.mean(x * x, axis=-1, keepdims=True)
    return (x * lax.rsqrt(ms + NORM_EPS)) * g


def _rope_lanes(y, cf, sf):
    outs = []
    for c in range(y.shape[1] // LANES):
        yc = y[:, c * LANES:(c + 1) * LANES]
        outs.append(yc * cf + pltpu.roll(yc, LANES // 2, 1) * sf)
    return outs[0] if len(outs) == 1 else jnp.concatenate(outs, axis=1)


def _cast_rope_columns_kernel(p_ref, w_ref, o_ref, *, lo_blk, hi_blk):
    j = pl.program_id(2)
    w = w_ref[...].astype(BF16)
    roped = jnp.logical_and(j >= lo_blk, j < hi_blk)

    @pl.when(roped)
    def _():
        o_ref[...] = jnp.dot(w, p_ref[...], preferred_element_type=F32).astype(BF16)

    @pl.when(jnp.logical_not(roped))
    def _():
        o_ref[...] = w


def _cast_rope_columns(w, lo, hi, *, tr=1024, tc=512, name):
    n_layers, K, N = w.shape
    tr = _pick_tile(K, tr, SUBLANES)
    tc = _pick_tile(math.gcd(N, lo, hi), tc, LANES)
    r, h = ROPE_HALF, LANES // 2
    src = jnp.concatenate([jnp.arange(0, r), jnp.arange(2 * r, h + r), jnp.arange(r, 2 * r),
                           jnp.arange(h + r, LANES)])
    perm = (jnp.arange(LANES)[:, None] == src[None, :]).astype(BF16)
    perm = jnp.kron(jnp.eye(tc // LANES, dtype=BF16), perm)
    return pl.pallas_call(
        functools.partial(_cast_rope_columns_kernel, lo_blk=lo // tc, hi_blk=hi // tc),
        grid=(n_layers, K // tr, N // tc),
        in_specs=[pl.BlockSpec((tc, tc), lambda l, i, j: (0, 0)),
                  pl.BlockSpec((None, tr, tc), lambda l, i, j: (l, i, j))],
        out_specs=pl.BlockSpec((None, tr, tc), lambda l, i, j: (l, i, j)),
        out_shape=jax.ShapeDtypeStruct(w.shape, BF16),
        compiler_params=_params(("parallel", "parallel", "parallel"), 8 * tr * tc * 4 + (4 << 20)),
        name=name,
    )(perm, w)


def _rope_tables(seq):
    pos = jnp.arange(seq, dtype=F32)
    inv = ROPE_THETA ** (-jnp.arange(0, 2 * ROPE_HALF, 2, dtype=F32) / (2 * ROPE_HALF))
    ang = pos[:, None] * inv[None, :]
    cos, sin = jnp.cos(ang), jnp.sin(ang)
    gap = LANES // 2 - ROPE_HALF
    cf = jnp.concatenate([cos, jnp.ones((seq, gap), F32), cos, jnp.ones((seq, gap), F32)], axis=1)
    sf = jnp.concatenate([-sin, jnp.zeros((seq, gap), F32), sin, jnp.zeros((seq, gap), F32)], axis=1)
    return cf, sf


def _norm_matmul_kernel(*refs, segs, tn, use_rope):
    if use_rope:
        x_ref, g_ref, w_ref, cf_ref, sf_ref, o_ref, xn_ref = refs
    else:
        x_ref, g_ref, w_ref, o_ref, xn_ref = refs
    j = pl.program_id(1)

    @pl.when(j == 0)
    def _():
        xn_ref[...] = _rms_normalize(x_ref[...], g_ref[...]).astype(BF16)

    y = jnp.dot(xn_ref[...], w_ref[...], preferred_element_type=F32)
    for lo, hi, rope, scale in segs:
        @pl.when(jnp.logical_and(j >= lo // tn, j < hi // tn))
        def _(rope=rope, scale=scale):
            z = y
            if rope:
                cf, sf = cf_ref[...], sf_ref[...]
                if scale != 1.0:
                    cf, sf = cf * scale, sf * scale
                z = _rope_lanes(z, cf, sf)
            elif scale != 1.0:
                z = z * scale
            o_ref[...] = z.astype(o_ref.dtype)


def _norm_matmul(x, g, w, *, segs=(), rope_tabs=None, seq=None, tm=1024, tn=1024, name):
    M, D = x.shape
    w, layer = w
    N = w.shape[2]
    tm = _pick_tile(M if seq is None else seq, tm, SUBLANES)
    tn = _pick_tile(math.gcd(N, *[b for s in segs for b in s[:2]]), tn, LANES)
    full, pos = [], 0
    for lo, hi, rope, scale in sorted(segs):
        if lo > pos:
            full.append((pos, lo, False, 1.0))
        full.append((lo, hi, rope, scale))
        pos = hi
    if pos < N:
        full.append((pos, N, False, 1.0))
    for lo, hi, _, _ in full:
        assert lo % tn == 0 and hi % tn == 0, (lo, hi, tn)
    use_rope = any(s[2] for s in full)
    in_specs = [pl.BlockSpec((tm, D), lambda i, j: (i, 0)),
                pl.BlockSpec((1, D), lambda i, j: (0, 0)),
                pl.BlockSpec((None, D, tn), lambda i, j: (layer, 0, j))]
    args = [x, g.reshape(1, D).astype(F32), w]
    if use_rope:
        nseq = seq // tm
        in_specs += [pl.BlockSpec((tm, LANES), lambda i, j: (i % nseq, 0))] * 2
        args += list(rope_tabs)
    vmem = 2 * tm * D * 4 + tm * D * 2 + 2 * D * tn * 2 + 2 * tm * tn * 2 + 3 * tm * tn * 4 + (4 << 20)
    return pl.pallas_call(
        functools.partial(_norm_matmul_kernel, segs=tuple(full), tn=tn, use_rope=use_rope),
        grid=(M // tm, N // tn),
        in_specs=in_specs,
        out_specs=pl.BlockSpec((tm, tn), lambda i, j: (i, j)),
        out_shape=jax.ShapeDtypeStruct((M, N), BF16),
        scratch_shapes=[pltpu.VMEM((tm, D), BF16)],
        compiler_params=_params(("parallel", "arbitrary"), vmem),
        name=name,
    )(*args)


def _matmul_res_kernel(*refs, n_a):
    a_refs, w_refs = refs[:n_a], refs[n_a:2 * n_a]
    res_ref, o_ref = refs[2 * n_a], refs[2 * n_a + 1]
    acc = res_ref[...]
    for a_ref, w_ref in zip(a_refs, w_refs):
        acc = acc + jnp.dot(a_ref[...], w_ref[...], preferred_element_type=F32)
    o_ref[...] = acc


def _matmul_res(a_list, w, res, *, tm=1024, tn=512, name):
    n_a = len(a_list)
    M, K = a_list[0].shape
    w, layer = w
    N = w.shape[2]
    tm = _pick_tile(M, tm, SUBLANES)
    tn = _pick_tile(N, tn, LANES)
    in_specs = [pl.BlockSpec((tm, K), lambda i, j: (i, 0)) for _ in range(n_a)]
    in_specs += [pl.BlockSpec((None, K, tn), lambda i, j, r=r: (layer, r, j)) for r in range(n_a)]
    in_specs += [pl.BlockSpec((tm, tn), lambda i, j: (i, j))]
    vmem = n_a * (2 * tm * K * 2 + 2 * K * tn * 2) + 5 * tm * tn * 4 + (4 << 20)
    return pl.pallas_call(
        functools.partial(_matmul_res_kernel, n_a=n_a),
        grid=(M // tm, N // tn),
        in_specs=in_specs,
        out_specs=pl.BlockSpec((tm, tn), lambda i, j: (i, j)),
        out_shape=jax.ShapeDtypeStruct((M, N), F32),
        compiler_params=_params(("parallel", "parallel"), vmem),
        name=name,
    )(*a_list, *([w] * n_a), res)


def _swiglu_up_kernel(x_ref, g_ref, w1_ref, w3_ref, o_ref, xn_ref):
    @pl.when(pl.program_id(1) == 0)
    def _():
        xn_ref[...] = _rms_normalize(x_ref[...], g_ref[...]).astype(BF16)

    xn = xn_ref[...]
    a = jnp.dot(xn, w1_ref[...], preferred_element_type=F32)
    b = jnp.dot(xn, w3_ref[...], preferred_element_type=F32)
    o_ref[...] = (a * jax.nn.sigmoid(a) * b).astype(o_ref.dtype)


def _swiglu_up(x, g, w13, *, tm=1024, tn=512, name):
    M, D = x.shape
    w13, layer = w13
    H = w13.shape[2] // 2
    tm = _pick_tile(M, tm, SUBLANES)
    tn = _pick_tile(H, tn, LANES)
    nj = H // tn
    vmem = 2 * tm * D * 4 + tm * D * 2 + 4 * D * tn * 2 + 2 * tm * tn * 2 + 4 * tm * tn * 4 + (4 << 20)
    return pl.pallas_call(
        _swiglu_up_kernel,
        grid=(M // tm, nj),
        in_specs=[pl.BlockSpec((tm, D), lambda i, j: (i, 0)),
                  pl.BlockSpec((1, D), lambda i, j: (0, 0)),
                  pl.BlockSpec((None, D, tn), lambda i, j: (layer, 0, j)),
                  pl.BlockSpec((None, D, tn), lambda i, j: (layer, 0, j + nj))],
        out_specs=pl.BlockSpec((tm, tn), lambda i, j: (i, j)),
        out_shape=jax.ShapeDtypeStruct((M, H), BF16),
        scratch_shapes=[pltpu.VMEM((tm, D), BF16)],
        compiler_params=_params(("parallel", "arbitrary"), vmem),
        name=name,
    )(x, g.reshape(1, D).astype(F32), w13, w13)


def _cross_attn_kernel(x_ref, g_ref, wq_ref, k_ref, v_ref, o_ref, *, nh):
    dh = o_ref.shape[1] // nh
    xn = _rms_normalize(x_ref[...], g_ref[...]).astype(BF16)
    for h in range(nh):
        sl = slice(h * dh, (h + 1) * dh)
        q = jnp.dot(xn, wq_ref[:, sl], preferred_element_type=F32).astype(BF16)
        s = lax.dot_general(q, k_ref[:, sl], (((1,), (1,)), ((), ())), preferred_element_type=F32)
        m = jnp.max(s, axis=-1, keepdims=True)
        e = jnp.exp(s - m)
        den = jnp.sum(e, axis=-1, keepdims=True)
        o = jnp.dot(e.astype(BF16), v_ref[:, sl], preferred_element_type=F32)
        o_ref[:, sl] = (o / den).astype(o_ref.dtype)


def _cross_attn(x, g, wq, kv, bsz, *, tq=1024, name):
    T, D = x.shape
    S = T // bsz
    wq, layer = wq
    mem_len = kv.shape[0] // bsz
    tq = _pick_tile(S, tq, SUBLANES)
    nq = S // tq
    kv3 = kv.reshape(bsz, mem_len, 2 * D)
    vmem = 2 * tq * D * 4 + 2 * D * D * 2 + 4 * mem_len * D * 2 + 2 * tq * D * 2 + 6 * tq * D * 2 + (8 << 20)
    return pl.pallas_call(
        functools.partial(_cross_attn_kernel, nh=XA_HEADS),
        grid=(bsz, nq),
        in_specs=[pl.BlockSpec((tq, D), lambda b, i: (b * nq + i, 0)),
                  pl.BlockSpec((1, D), lambda b, i: (0, 0)),
                  pl.BlockSpec((None, D, D), lambda b, i: (layer, 0, 0)),
                  pl.BlockSpec((None, mem_len, D), lambda b, i: (b, 0, 0)),
                  pl.BlockSpec((None, mem_len, D), lambda b, i: (b, 0, 1))],
        out_specs=pl.BlockSpec((tq, D), lambda b, i: (b * nq + i, 0)),
        out_shape=jax.ShapeDtypeStruct((T, D), BF16),
        compiler_params=_params(("parallel", "parallel"), vmem),
        name=name,
    )(x, g.reshape(1, D).astype(F32), wq, kv3, kv3)


def _diff_attn_kernel(q_ref, k_ref, v_ref, lv_ref, sg_ref, o_ref, vt_ref, *, tk, unroll, lambda_init):
    tq = q_ref.shape[0]
    seq = k_ref.shape[0]
    dv = v_ref.shape[1]
    dh = dv // 2

    @pl.when(pl.program_id(2) == 0)
    def _():
        for c in range(seq // tk):
            vt_ref[:, c * tk:(c + 1) * tk] = v_ref[c * tk:(c + 1) * tk, :].astype(F32).T.astype(BF16)

    q = q_ref[...]
    nk = seq // tk

    def scores(kk):
        kc = k_ref[pl.ds(pl.multiple_of(kk * tk, tk), tk), :]
        return tuple(lax.dot_general(kc[:, c * dh:(c + 1) * dh], q[:, c * dh:(c + 1) * dh],
                                     (((1,), (1,)), ((), ())), preferred_element_type=F32)
                     for c in range(2))

    def body(kk, carry):
        st_next = scores(jnp.minimum(kk + 1, nk - 1))
        vt = vt_ref[:, pl.ds(pl.multiple_of(kk * tk, tk), tk)]
        new = []
        for c in range(2):
            st, m, l, acc = carry[c]
            m_new = jnp.maximum(m, jnp.max(st, axis=0, keepdims=True))
            alpha = jnp.exp2(m - m_new)
            pt = jnp.exp2(st - m_new)
            l = alpha * l + jnp.sum(pt, axis=0, keepdims=True)
            acc = alpha * acc + jnp.dot(vt, pt.astype(BF16), preferred_element_type=F32)
            new.append((st_next[c], m_new, l, acc))
        return tuple(new)

    st0 = scores(0)
    init = tuple((st0[c], jnp.full((1, tq), MASK_VALUE, F32), jnp.zeros((1, tq), F32),
                  jnp.zeros((dv, tq), F32)) for c in range(2))
    (_, _, l1, a1), (_, _, l2, a2) = lax.fori_loop(0, nk, body, init, unroll=unroll)
    lv = lv_ref[...]
    lam = (jnp.exp(jnp.sum(lv[0:1] * lv[1:2], axis=-1, keepdims=True))
           - jnp.exp(jnp.sum(lv[2:3] * lv[3:4], axis=-1, keepdims=True)) + lambda_init)
    ot = a1 * (1.0 / l1) - a2 * (lam * (1.0 / l2))
    ms = jnp.mean(ot * ot, axis=0, keepdims=True)
    ot = ot * (lax.rsqrt(ms + NORM_EPS) * (1.0 - lambda_init))
    o_ref[...] = (ot.T * sg_ref[...]).astype(o_ref.dtype)


def _diff_attn(proj, lam_vec, subln_g, bsz, *, s5_width, n_heads, lambda_init, tq=512, tk=512, unroll=1, name):
    T, ncol = proj.shape
    S = T // bsz
    dv = 2 * HEAD_DIM
    width = n_heads * dv
    q0, k0, v0 = s5_width // dv, (s5_width + width) // dv, (s5_width + 2 * width) // dv
    tq = _pick_tile(S, tq, SUBLANES)
    tk = _pick_tile(S, tk, LANES)
    nq = S // tq
    proj3 = proj.reshape(bsz, S, ncol)
    vmem = 4 * tq * dv * 2 + 5 * S * dv * 2 + unroll * (3 * tq * tk * 4 + 2 * tq * dv * 4) + (8 << 20)
    return pl.pallas_call(
        functools.partial(_diff_attn_kernel, tk=tk, unroll=unroll, lambda_init=lambda_init),
        grid=(bsz, n_heads, nq),
        in_specs=[pl.BlockSpec((None, tq, dv), lambda b, h, i: (b, i, q0 + h)),
                  pl.BlockSpec((None, S, dv), lambda b, h, i: (b, 0, k0 + h)),
                  pl.BlockSpec((None, S, dv), lambda b, h, i: (b, 0, v0 + h)),
                  pl.BlockSpec((4, HEAD_DIM), lambda b, h, i: (0, 0)),
                  pl.BlockSpec((1, dv), lambda b, h, i: (0, 0))],
        out_specs=pl.BlockSpec((tq, dv), lambda b, h, i: (b * nq + i, h)),
        out_shape=jax.ShapeDtypeStruct((T, width), BF16),
        scratch_shapes=[pltpu.VMEM((dv, S), BF16)],
        compiler_params=_params(("parallel", "parallel", "arbitrary"), vmem),
        name=name,
    )(proj3, proj3, proj3, lam_vec.astype(F32), subln_g.reshape(1, dv).astype(F32))


def _s5_tables(lam_re, lam_im, log_step, b_re, b_im, c_re, c_im, d_skip, n_steps):
    L = S5_CHUNK
    _, G, P = lam_re.shape
    N = b_re.shape[-1]
    lr, li = lam_re.astype(F32), lam_im.astype(F32)
    step = jnp.exp(log_step.astype(F32))[..., None]
    ar, ai = lr * step, li * step

    def cexp(t, axes=()):
        a_r, a_i = jnp.expand_dims(ar, axes), jnp.expand_dims(ai, axes)
        mag = jnp.exp(t * a_r)
        return mag * jnp.cos(t * a_i), mag * jnp.sin(t * a_i)

    def cmul(xr, xi, yr, yi):
        return xr * yr - xi * yi, xr * yi + xi * yr

    er, ei = cexp(1.0)
    den = lr * lr + li * li
    qr, qi = ((er - 1.0) * lr + ei * li) / den, (ei * lr - (er - 1.0) * li) / den
    br, bi = cmul(qr[..., None], qi[..., None], b_re.astype(F32), b_im.astype(F32))
    cr, ci = c_re.astype(F32), c_im.astype(F32)
    tau = jnp.arange(L + 1, dtype=F32)
    pr, pi = cexp(tau[None, None, :, None], axes=2)
    prt, pit = cexp(tau[None, None, None, :], axes=3)
    brt, bit = jnp.swapaxes(br, 2, 3), jnp.swapaxes(bi, 2, 3)
    crt, cit = jnp.swapaxes(cr, 2, 3), jnp.swapaxes(ci, 2, 3)

    def pow_c(d, taus):
        return cmul(prt[d][:, :, taus, None], pit[d][:, :, taus, None], crt[d][:, :, None, :], cit[d][:, :, None, :])

    kb_cat = jnp.stack([jnp.concatenate([brt[d], -bit[d]], axis=2) for d in range(2)], axis=1)
    kc_cat = jnp.stack([jnp.concatenate(pow_c(d, taus), axis=1).reshape(G, 2 * P, L * N)
                        for d, taus in ((0, slice(0, L)), (1, slice(L - 1, None, -1)))], axis=1)

    def pow_b(d, taus):
        return cmul(pr[d][:, taus, None, :], pi[d][:, taus, None, :], brt[d][:, None], bit[d][:, None])
    (fr, fi), (gr, gi) = pow_b(0, slice(L - 1, None, -1)), pow_b(1, slice(0, L))
    w_in = jnp.concatenate([fr, gr, fi, gi], axis=3).reshape(G, L * N, 4 * P)

    (of_r, of_i), (ob_r, ob_i) = pow_c(0, slice(1, L + 1)), pow_c(1, slice(L, 0, -1))
    w_out = jnp.concatenate([of_r, ob_r, -of_i, -ob_i], axis=1).reshape(G, 4 * P, L * N)

    def _lanes(z):
        return jnp.concatenate([z[0], z[1]], axis=-1)
    a_chunk, a_seg = cexp(float(L)), cexp(float(L * n_steps))
    decay = jnp.stack([_lanes(a_chunk[0]), _lanes(a_chunk[1]), _lanes(a_seg[0]), _lanes(a_seg[1])], axis=1)
    jj = float(L) * jnp.arange(n_steps, dtype=F32)
    jj = jnp.stack([jj, jj[::-1]], axis=0)[:, None, :, None]
    powers = jnp.stack([_lanes(p) for p in cexp(jj, axes=2)], axis=1)
    dvec = jnp.tile(d_skip.astype(F32).reshape(G, 1, N), (1, L, 1)).reshape(G, 1, L * N)
    return kb_cat, kc_cat, w_in.astype(BF16), w_out.astype(BF16), decay, powers, dvec


def _s5_kernel(u_ref, kb_ref, kc_ref, win_ref, wout_ref, decay_ref, pow_ref, dvec_ref, o_ref, toep_ref,
               *, n_steps, bsz, gb):
    rb = bsz * SUBLANES
    n_ch = kb_ref.shape[2]
    width = kc_ref.shape[3]
    chunk = width // n_ch
    lane = lax.broadcasted_iota(jnp.int32, (n_ch, width), 1)
    for gi in range(gb):
        kf = jnp.dot(kb_ref[gi, 0], kc_ref[gi, 0], preferred_element_type=F32, precision=lax.Precision.HIGHEST)
        kb = jnp.dot(kb_ref[gi, 1], kc_ref[gi, 1], preferred_element_type=F32, precision=lax.Precision.HIGHEST)
        for s in range(chunk):
            fwd = kf if s == 0 else jnp.where(lane >= s * n_ch, pltpu.roll(kf, s * n_ch, 1), 0.0)
            back = chunk - 1 - s
            bwd = kb if back == 0 else jnp.where(lane < (s + 1) * n_ch, pltpu.roll(kb, width - back * n_ch, 1), 0.0)
            toep_ref[gi, s * n_ch:(s + 1) * n_ch, :] = (fwd + bwd).astype(BF16)
        u = u_ref[gi]
        y = jnp.dot(u, toep_ref[gi], preferred_element_type=F32)
        z = jnp.dot(u, win_ref[gi], preferred_element_type=F32)
        half = z.shape[1] // 2
        zr, zi = z[:, :half], z[:, half:]
        dec = decay_ref[gi]
        ar, ai, sr, si = dec[0:1], dec[1:2], dec[2:3], dec[3:4]
        is_fwd = lax.broadcasted_iota(jnp.int32, (rb, half), 1) < half // 2

        xr = jnp.zeros((rb, half), F32)
        xi = jnp.zeros((rb, half), F32)
        ent_r, ent_i = [], []
        for j in range(n_steps):
            jb = n_steps - 1 - j
            ent_r.append(xr)
            ent_i.append(xi)
            zrj = jnp.where(is_fwd, zr[j * rb:(j + 1) * rb], zr[jb * rb:(jb + 1) * rb])
            zij = jnp.where(is_fwd, zi[j * rb:(j + 1) * rb], zi[jb * rb:(jb + 1) * rb])
            xr, xi = ar * xr - ai * xi + zrj, ar * xi + ai * xr + zij

        sub = lax.broadcasted_iota(jnp.int32, (SUBLANES, half), 0)
        fwd8 = lax.broadcasted_iota(jnp.int32, (SUBLANES, half), 1) < half // 2
        keep = (fwd8 & (sub > 0)) | (jnp.logical_not(fwd8) & (sub < SUBLANES - 1))
        car_r, car_i = [], []
        for b in range(bsz):
            er, ei = xr[b * SUBLANES:(b + 1) * SUBLANES], xi[b * SUBLANES:(b + 1) * SUBLANES]
            cr = jnp.zeros((SUBLANES, half), F32)
            ci = jnp.zeros((SUBLANES, half), F32)
            for _ in range(SUBLANES - 1):
                tr, ti = sr * cr - si * ci + er, sr * ci + si * cr + ei
                cr = jnp.where(keep, jnp.where(fwd8, pltpu.roll(tr, 1, 0), pltpu.roll(tr, SUBLANES - 1, 0)), 0.0)
                ci = jnp.where(keep, jnp.where(fwd8, pltpu.roll(ti, 1, 0), pltpu.roll(ti, SUBLANES - 1, 0)), 0.0)
            car_r.append(cr)
            car_i.append(ci)
        gr = car_r[0] if bsz == 1 else jnp.concatenate(car_r, axis=0)
        gi_ = car_i[0] if bsz == 1 else jnp.concatenate(car_i, axis=0)

        pr_all, pi_all = pow_ref[gi, 0], pow_ref[gi, 1]
        rows_r, rows_i = [], []
        for j in range(n_steps):
            jb = n_steps - 1 - j
            lr = jnp.where(is_fwd, ent_r[j], ent_r[jb])
            li = jnp.where(is_fwd, ent_i[j], ent_i[jb])
            pr, pi = pr_all[j:j + 1], pi_all[j:j + 1]
            rows_r.append(lr + pr * gr - pi * gi_)
            rows_i.append(li + pr * gi_ + pi * gr)
        state = jnp.concatenate([jnp.concatenate(rows_r, axis=0), jnp.concatenate(rows_i, axis=0)], axis=1)
        y = y + jnp.dot(state.astype(BF16), wout_ref[gi], preferred_element_type=F32)
        y = y + u.astype(F32) * dvec_ref[gi]
        o_ref[gi] = y.astype(o_ref.dtype)


def _s5_mix(u, tables, bsz, *, gb=4, name):
    kb_cat, kc_cat, w_in, w_out, decay, powers, dvec = tables
    T, width = u.shape
    S = T // bsz
    G = w_in.shape[0]
    N = width // G
    L = S5_CHUNK
    n_steps = S // (L * S5_SEGMENTS)
    rows = n_steps * bsz * S5_SEGMENTS
    gb = _pick_tile(G, gb, 1)
    ug = u.reshape(bsz, S5_SEGMENTS, n_steps, L, G, N)
    ug = jnp.transpose(ug, (4, 2, 0, 1, 3, 5)).reshape(G, rows, L * N)
    blk = lambda *shape: pl.BlockSpec((gb,) + shape, lambda g: (g,) + (0,) * len(shape))
    vmem = gb * (4 * rows * L * N * 2 + (L * N) ** 2 * 2 + 4 * kc_cat.shape[2] * L * N * 4
                 + 8 * L * N * w_in.shape[2] * 2) + (24 << 20)
    yg = pl.pallas_call(
        functools.partial(_s5_kernel, n_steps=n_steps, bsz=bsz, gb=gb),
        grid=(G // gb,),
        in_specs=[blk(rows, L * N), blk(*kb_cat.shape[1:]), blk(*kc_cat.shape[1:]),
                  blk(L * N, w_in.shape[2]), blk(w_out.shape[1], L * N),
                  blk(4, decay.shape[2]), blk(2, n_steps, powers.shape[3]), blk(1, L * N)],
        out_specs=blk(rows, L * N),
        out_shape=jax.ShapeDtypeStruct((G, rows, L * N), BF16),
        scratch_shapes=[pltpu.VMEM((gb, L * N, L * N), BF16)],
        compiler_params=_params(("parallel",), vmem),
        name=name,
    )(ug, kb_cat, kc_cat, w_in, w_out, decay, powers, dvec)
    yg = yg.reshape(G, n_steps, bsz, S5_SEGMENTS, L, N)
    return jnp.transpose(yg, (2, 3, 1, 4, 0, 5)).reshape(T, width)


def _s5_glu_kernel(y_ref, w_ref, b_ref, o_ref):
    g = jax.nn.gelu(y_ref[...].astype(F32), approximate=True)
    z = jnp.dot(g.astype(BF16), w_ref[...], preferred_element_type=F32) + b_ref[...]
    o_ref[...] = (g * jax.nn.sigmoid(z)).astype(o_ref.dtype)


def _s5_glu(y, w, b, *, tm=1024, name):
    M, W = y.shape
    w, layer = w
    tm = _pick_tile(M, tm, SUBLANES)
    vmem = 4 * tm * W * 2 + 2 * W * W * 2 + 4 * tm * W * 4 + (4 << 20)
    return pl.pallas_call(
        _s5_glu_kernel,
        grid=(M // tm,),
        in_specs=[pl.BlockSpec((tm, W), lambda i: (i, 0)),
                  pl.BlockSpec((None, W, W), lambda i: (layer, 0, 0)),
                  pl.BlockSpec((1, W), lambda i: (0, 0))],
        out_specs=pl.BlockSpec((tm, W), lambda i: (i, 0)),
        out_shape=jax.ShapeDtypeStruct((M, W), BF16),
        compiler_params=_params(("parallel",), vmem),
        name=name,
    )(y, w, b.reshape(1, W).astype(F32))


def _slab_start(i, tq, half, slab, sub_len):
    assert tq % half == 0 and (sub_len - slab) % half == 0
    return half * jnp.clip(i * (tq // half) - 1, 0, (sub_len - slab) // half)


def _dilated_kernel(q_ref, k_ref, v_ref, o_ref, lse_ref, *, sub_len, half, nh):
    i = pl.program_id(1)
    tq = q_ref.shape[0]
    slab = k_ref.shape[0]
    dh = q_ref.shape[1] // nh
    assert dh == LANES
    sq = 2 * half
    win = sq + 2 * half
    start = _slab_start(i, tq, half, slab, sub_len)
    head_lane = lax.broadcasted_iota(jnp.int32, (sq, nh), 1)
    ones = jnp.ones((win, LANES), BF16)
    for j in range(tq // sq):
        off = pl.multiple_of(jnp.clip(i * tq + j * sq - half - start, 0, slab - win), half)
        qpos = i * tq + j * sq + lax.broadcasted_iota(jnp.int32, (sq, win), 0)
        kpos = start + off + lax.broadcasted_iota(jnp.int32, (sq, win), 1)
        valid = jnp.abs(kpos - qpos) <= half
        rows = slice(j * sq, (j + 1) * sq)
        lse_all = jnp.zeros((sq, nh), F32)
        for h in range(nh):
            sl = slice(h * dh, (h + 1) * dh)
            s = lax.dot_general(q_ref[rows, sl], k_ref[pl.ds(off, win), sl], (((1,), (1,)), ((), ())),
                                preferred_element_type=F32)
            s = jnp.where(valid, s, MASK_VALUE)
            m = jnp.max(s, axis=-1, keepdims=True)
            e = jnp.exp2(s - m).astype(BF16)
            ov = jnp.dot(e, jnp.concatenate([v_ref[pl.ds(off, win), sl], ones], axis=1),
                         preferred_element_type=F32)
            den = ov[:, dh:]
            o_ref[rows, sl] = (ov[:, :dh] / den).astype(o_ref.dtype)
            lse_all = jnp.where(head_lane == h, m * LN_2 + jnp.log(den[:, :nh]), lse_all)
        lse_ref[rows, :] = lse_all


def _dilated_branch(q, k, v, cols, D, n_seq, *, half, nh, name):
    T = q.shape[0]
    sub_len = T // n_seq
    sq = 2 * half
    n_sub = max(n for n in (4, 2, 1) if sub_len % (n * sq) == 0 and n * sq + 2 * half <= sub_len)
    tq = n_sub * sq
    slab = tq + 2 * half
    nq = sub_len // tq
    cq, ck, cv = cols

    def kv_spec(c):
        return pl.BlockSpec((pl.Element(slab), pl.Element(D)),
                            lambda s, i: (pl.multiple_of(s * sub_len + _slab_start(i, tq, half, slab, sub_len),
                                                         half), c * D))

    vmem = 2 * (2 * tq * D * 2 + 2 * slab * D * 2) + (16 << 20)
    return pl.pallas_call(
        functools.partial(_dilated_kernel, sub_len=sub_len, half=half, nh=nh),
        grid=(n_seq, nq),
        in_specs=[pl.BlockSpec((tq, D), lambda s, i: (s * nq + i, cq)), kv_spec(ck), kv_spec(cv)],
        out_specs=[pl.BlockSpec((tq, D), lambda s, i: (s * nq + i, 0)),
                   pl.BlockSpec((tq, nh), lambda s, i: (s * nq + i, 0))],
        out_shape=[jax.ShapeDtypeStruct((T, D), BF16), jax.ShapeDtypeStruct((T, nh), F32)],
        compiler_params=_params(("parallel", "parallel"), vmem),
        name=name,
    )(q, k, v)


def _residue_perm(dil):
    per = PERM_GROUP // dil
    rows = jnp.arange(PERM_GROUP)
    src = (rows % per) * dil + rows // per
    return (src[:, None] == rows[None, :]).astype(BF16)


def _to_residue_kernel(*refs, dils, n_in):
    n_d = len(dils)
    p_refs, x_refs, o_refs = refs[:n_d], refs[n_d:n_d + n_in], refs[n_d + n_in:]
    for xi, x_ref in enumerate(x_refs):
        for g in range(x_ref.shape[0] // PERM_GROUP):
            xg = x_ref[g * PERM_GROUP:(g + 1) * PERM_GROUP, :]
            for di, dil in enumerate(dils):
                o_ref = o_refs[di * n_in + xi]
                per = PERM_GROUP // dil
                y = jnp.dot(p_refs[di][...], xg, preferred_element_type=F32).astype(o_ref.dtype)
                for r in range(dil):
                    o_ref[r, g * per:(g + 1) * per, :] = y[r * per:(r + 1) * per]


def _to_residue(x, cols, width, dils, bsz, *, tb=512, name):
    T = x.shape[0]
    S = T // bsz
    tb = _pick_tile(S, tb, PERM_GROUP)
    nb = S // tb
    n_in = len(cols)
    perm_spec = pl.BlockSpec((PERM_GROUP, PERM_GROUP), lambda b, i: (0, 0))
    outs = pl.pallas_call(
        functools.partial(_to_residue_kernel, dils=tuple(dils), n_in=n_in),
        grid=(bsz, nb),
        in_specs=[perm_spec] * len(dils)
                 + [pl.BlockSpec((tb, width), lambda b, i, c=c: (b * nb + i, c)) for c in cols],
        out_specs=[pl.BlockSpec((None, dil, tb // dil, width), lambda b, i: (b, 0, i, 0))
                   for dil in dils for _ in cols],
        out_shape=[jax.ShapeDtypeStruct((bsz, dil, S // dil, width), x.dtype) for dil in dils for _ in cols],
        compiler_params=_params(("parallel", "parallel"),
                                (1 + len(dils)) * n_in * 2 * tb * width * 2 + (12 << 20)),
        name=name,
    )(*[_residue_perm(dil) for dil in dils], *([x] * n_in))
    outs = [o.reshape(T, width) for o in outs]
    return [outs[di * n_in:(di + 1) * n_in] for di in range(len(dils))]


def _from_residue_merge_kernel(p_ref, x_ref, lx_ref, prev_ref, lp_ref, o_ref, lo_ref, *, dil, nh):
    per = PERM_GROUP // dil
    dh = o_ref.shape[1] // nh
    lp, lx = lp_ref[...], lx_ref[...]
    mx = jnp.maximum(lp, lx)
    ep, ex = jnp.exp(lp - mx), jnp.exp(lx - mx)
    wx = ex / (ep + ex)
    lo_ref[...] = mx + jnp.log(ep + ex)
    for g in range(o_ref.shape[0] // PERM_GROUP):
        rows = slice(g * PERM_GROUP, (g + 1) * PERM_GROUP)
        xg = jnp.concatenate([x_ref[r, g * per:(g + 1) * per, :] for r in range(dil)], axis=0)
        y = jnp.dot(p_ref[...], xg, preferred_element_type=F32)
        for h in range(nh):
            sl = slice(h * dh, (h + 1) * dh)
            prev = prev_ref[rows, sl].astype(F32)
            o_ref[rows, sl] = (prev + wx[rows, h:h + 1] * (y[:, sl] - prev)).astype(o_ref.dtype)


def _from_residue_merge(x, lse_x, prev, lse_prev, dil, bsz, *, nh, tb=512, name):
    T, width = x.shape
    S = T // bsz
    tb = _pick_tile(S, tb, PERM_GROUP)
    nb = S // tb
    row = lambda w: pl.BlockSpec((tb, w), lambda b, i: (b * nb + i, 0))
    return pl.pallas_call(
        functools.partial(_from_residue_merge_kernel, dil=dil, nh=nh),
        grid=(bsz, nb),
        in_specs=[pl.BlockSpec((PERM_GROUP, PERM_GROUP), lambda b, i: (0, 0)),
                  pl.BlockSpec((None, dil, tb // dil, width), lambda b, i: (b, 0, i, 0)),
                  row(nh), row(width), row(nh)],
        out_specs=[row(width), row(nh)],
        out_shape=[jax.ShapeDtypeStruct((T, width), x.dtype), jax.ShapeDtypeStruct((T, nh), F32)],
        compiler_params=_params(("parallel", "parallel"), 6 * tb * width * 2 + 4 * tb * width * 4 + (8 << 20)),
        name=name,
    )(_residue_perm(dil).T, x.reshape(bsz, dil, S // dil, width), lse_x, prev, lse_prev)


def _final_norm_kernel(x_ref, g_ref, o_ref):
    o_ref[...] = _rms_normalize(x_ref[...], g_ref[...])


def _final_norm(x, g, *, tm=1024, name):
    M, D = x.shape
    tm = _pick_tile(M, tm, SUBLANES)
    return pl.pallas_call(
        _final_norm_kernel,
        grid=(M // tm,),
        in_specs=[pl.BlockSpec((tm, D), lambda i: (i, 0)), pl.BlockSpec((1, D), lambda i: (0, 0))],
        out_specs=pl.BlockSpec((tm, D), lambda i: (i, 0)),
        out_shape=jax.ShapeDtypeStruct((M, D), F32),
        compiler_params=_params(("parallel",), 6 * tm * D * 4 + (4 << 20)),
        name=name,
    )(x, g.reshape(1, D).astype(F32))


def _even_mixer(h, g, w_in, w_out, s5_tabs, glu_w, glu_b, lam_vec, subln_g, lambda_init, rope_tabs, bsz, tag):
    T, _ = h.shape
    S = T // bsz
    s5_width = glu_w[0].shape[1]
    width = (w_in[0].shape[2] - s5_width) // 3
    n_heads = width // (2 * HEAD_DIM)
    segs = ((s5_width, s5_width + width, True, HEAD_DIM ** -0.5 * LOG2_E),
            (s5_width + width, s5_width + 2 * width, True, 1.0))
    proj = _norm_matmul(h, g, w_in, segs=segs, rope_tabs=rope_tabs, seq=S, name=f"even_in_{tag}")
    y_s5 = _s5_mix(proj[:, :s5_width], s5_tabs, bsz, name=f"s5_{tag}")
    y_s5 = _s5_glu(y_s5, glu_w, glu_b, name=f"s5_glu_{tag}")
    y_diff = _diff_attn(proj, lam_vec, subln_g, bsz, s5_width=s5_width, n_heads=n_heads,
                        lambda_init=lambda_init, tk=1024, unroll=8, name=f"diff_attn_{tag}")
    return _matmul_res([y_s5, y_diff], w_out, h, tm=512, tn=2048, name=f"even_out_{tag}")


def _odd_mixer(h, g, w_qkv, w_out, rope_tabs, bsz, tag):
    T, D = h.shape
    S = T // bsz
    nh = D // HEAD_DIM
    segs = ((0, D, True, HEAD_DIM ** -0.5 * LOG2_E), (D, 2 * D, True, 1.0))
    qkv = _norm_matmul(h, g, w_qkv, segs=segs, rope_tabs=rope_tabs, seq=S, name=f"odd_in_{tag}")
    dils = [dil for _, dil in DILATED_BRANCHES if dil > 1]
    residue = dict(zip(dils, _to_residue(qkv, (0, 1, 2), D, dils, bsz, name=f"to_residue_{tag}")))
    merged = lse_merged = None
    for window, dil in sorted(DILATED_BRANCHES, key=lambda wd: wd[1]):
        half = window // (2 * dil)
        sub_len = S // dil
        if dil == 1:
            merged, lse_merged = _dilated_branch(qkv, qkv, qkv, (0, 1, 2), D, bsz, half=half, nh=nh,
                                                 name=f"dilated_{dil}_{tag}")
        else:
            qr, kr, vr = residue[dil]
            o, lse = _dilated_branch(qr, kr, vr, (0, 0, 0), D, bsz * dil, half=half, nh=nh,
                                     name=f"dilated_{dil}_{tag}")
            lse = jnp.transpose(lse.reshape(bsz, dil, sub_len, nh), (0, 2, 1, 3)).reshape(T, nh)
            merged, lse_merged = _from_residue_merge(o, lse, merged, lse_merged, dil, bsz, nh=nh,
                                                     name=f"from_residue_{dil}_{tag}")
    return _matmul_res([merged], w_out, h, tm=512, tn=2048, name=f"odd_out_{tag}")


def kernel(x, mem, norm_mix_g, norm_xa_g, norm_mem_g, xa_wq, xa_wkv, xa_wo, norm_ffn_g, ffn_w13, ffn_w2, ab_w_in, ab_w_out, s5_lambda_re, s5_lambda_im, s5_log_step, s5_b_re, s5_b_im, s5_c_re, s5_c_im, s5_d, s5_glu_w, s5_glu_b, diff_lambda, diff_subln_g, c_w_qkv, c_w_out, final_norm_g):
    bsz, S, D = x.shape
    T = bsz * S
    depth = norm_mix_g.shape[0]
    mem_len = mem.shape[1]
    rope_tabs = _rope_tables(S)
    n_steps = S // (S5_CHUNK * S5_SEGMENTS)
    s5_width = s5_glu_w.shape[1]
    diff_width = (ab_w_in.shape[2] - s5_width) // 3
    ab_w_in = _cast_rope_columns(ab_w_in, s5_width, s5_width + 2 * diff_width, name="cast_even_in")
    c_w_qkv = _cast_rope_columns(c_w_qkv, 0, 2 * D, name="cast_odd_in")
    ab_w_out, s5_glu_w, c_w_out, xa_wq, xa_wkv, xa_wo, ffn_w13, ffn_w2 = (
        w.astype(BF16) for w in (ab_w_out, s5_glu_w, c_w_out, xa_wq, xa_wkv, xa_wo, ffn_w13, ffn_w2))

    h = x.reshape(T, D)
    mem2 = mem.reshape(bsz * mem_len, D)
    for layer in range(depth):
        i = layer // 2
        if layer % 2 == 0:
            lambda_init = 0.8 - 0.6 * math.exp(-0.3 * layer)
            tabs = _s5_tables(s5_lambda_re[i], s5_lambda_im[i], s5_log_step[i], s5_b_re[i], s5_b_im[i],
                              s5_c_re[i], s5_c_im[i], s5_d[i], n_steps)
            h = _even_mixer(h, norm_mix_g[layer], (ab_w_in, i), (ab_w_out, i), tabs, (s5_glu_w, i),
                            s5_glu_b[i], diff_lambda[i], diff_subln_g[i], lambda_init, rope_tabs, bsz, layer)
        else:
            h = _odd_mixer(h, norm_mix_g[layer], (c_w_qkv, i), (c_w_out, i), rope_tabs, bsz, layer)

        kv = _norm_matmul(mem2, norm_mem_g[layer], (xa_wkv, layer),
                          segs=((0, D, False, (D // XA_HEADS) ** -0.5),), name=f"xa_kv_{layer}")
        o = _cross_attn(h, norm_xa_g[layer], (xa_wq, layer), kv, bsz, name=f"xa_core_{layer}")
        h = _matmul_res([o], (xa_wo, layer), h, tm=512, tn=2048, name=f"xa_out_{layer}")

        hid = _swiglu_up(h, norm_ffn_g[layer], (ffn_w13, layer), name=f"ffn_up_{layer}")
        h = _matmul_res([hid], (ffn_w2, layer), h, tn=512, name=f"ffn_down_{layer}")
    return _final_norm(h, final_norm_g, name="final_norm").reshape(bsz, S, D)
```
